```python
import math
import jax
import jax.numpy as jnp
from jax import lax
import numpy as np

D_MODEL = 4096
BATCH = 4
SEQ = 2048
DEPTH = 4
DEC_BATCH = 128
DEC_SEQ = 8
PAST_LEN = 16384
PAGE_SIZE = 128

N_MEM = 256
NORM_EPS = 1e-6

S5_WIDTH = D_MODEL // 4
S5_GROUP = 16
S5_GROUPS = S5_WIDTH // S5_GROUP
S5_STATE = 64

GLA_WIDTH = D_MODEL // 4
GLA_HEADS = 4
GLA_DV = GLA_WIDTH // GLA_HEADS
GLA_DK = GLA_DV // 2
GLA_QK = GLA_HEADS * GLA_DK
GLA_GATE_RANK = 16
GLA_GATE_TAU = 16.0
GLA_CHUNK = 16

RWKV_WIDTH = 3 * D_MODEL // 8
RWKV_HEAD = 64
RWKV_HEADS = RWKV_WIDTH // RWKV_HEAD
RWKV_DECAY_RANK = 64
RWKV_A_RANK = 64
RWKV_SHIFT_COLS = 3 * RWKV_WIDTH + RWKV_DECAY_RANK + RWKV_A_RANK
RWKV_GN_EPS = RWKV_HEAD * 1e-5

XA_HEADS = 4
XA_HEAD_DIM = 128
XA_WIDTH = XA_HEADS * XA_HEAD_DIM

MIX_WIDTH = S5_WIDTH + GLA_WIDTH + RWKV_WIDTH + XA_WIDTH

IN_SPLIT = (S5_WIDTH, S5_WIDTH, GLA_QK, GLA_QK, GLA_WIDTH, GLA_WIDTH, GLA_GATE_RANK,
            RWKV_SHIFT_COLS, RWKV_WIDTH, XA_WIDTH)
IN_COLS = sum(IN_SPLIT)

kernel_name = "hymba_s5_gla_rwkv7_memxattn_step"


def _split(h, sizes):
    idx = [int(i) for i in np.cumsum(sizes)[:-1]]
    return jnp.split(h, idx, axis=-1)


def rms_norm(x, g):
    xf = x.astype(jnp.float32)
    y = xf * lax.rsqrt(jnp.mean(xf * xf, axis=-1, keepdims=True) + NORM_EPS)
    return (y * g.astype(jnp.float32)).astype(x.dtype)


def _complex_affine_combine(e1, e2):
    a1r, a1i, b1r, b1i = e1
    a2r, a2i, b2r, b2i = e2
    return (a1r * a2r - a1i * a2i,
            a1r * a2i + a1i * a2r,
            a2r * b1r - a2i * b1i + b2r,
            a2r * b1i + a2i * b1r + b2i)


def s5_mixer(u, h0, lam_re, lam_im, log_step, b_re, b_im, c_re, c_im, d_skip, w_glu, b_glu):
    f32 = jnp.float32
    bsz, t, _ = u.shape
    uf = u.astype(f32)
    ug = uf.reshape(bsz, t, S5_GROUPS, S5_GROUP)
    lr = lam_re.astype(f32)
    li = lam_im.astype(f32)
    step = jnp.exp(log_step.astype(f32))[:, None]
    mag = jnp.exp(lr * step)
    ang = li * step
    ab_re = mag * jnp.cos(ang)
    ab_im = mag * jnp.sin(ang)
    den = lr * lr + li * li
    f_re = ((ab_re - 1.0) * lr + ab_im * li) / den
    f_im = (ab_im * lr - (ab_re - 1.0) * li) / den
    br = b_re.astype(f32)
    bi = b_im.astype(f32)
    bb_re = f_re[..., None] * br - f_im[..., None] * bi
    bb_im = f_re[..., None] * bi + f_im[..., None] * br
    bu_re = jnp.einsum("btgn,gpn->btgp", ug, bb_re)
    bu_im = jnp.einsum("btgn,gpn->btgp", ug, bb_im)
    a_re = jnp.broadcast_to(ab_re, bu_re.shape)
    a_im = jnp.broadcast_to(ab_im, bu_im.shape)
    cum_re, cum_im, x_re, x_im = lax.associative_scan(
        _complex_affine_combine, (a_re, a_im, bu_re, bu_im), axis=1)
    h0r = h0[..., 0].astype(f32)[:, None]
    h0i = h0[..., 1].astype(f32)[:, None]
    s_re = cum_re * h0r - cum_im * h0i + x_re
    s_im = cum_re * h0i + cum_im * h0r + x_im
    y = (jnp.einsum("btgp,gnp->btgn", s_re, c_re.astype(f32))
         - jnp.einsum("btgp,gnp->btgn", s_im, c_im.astype(f32))).reshape(bsz, t, S5_WIDTH)
    y = y + d_skip.astype(f32) * uf
    z = jax.nn.gelu(y)
    out = z * jax.nn.sigmoid(z @ w_glu.astype(f32) + b_glu.astype(f32))
    h_last = jnp.stack([s_re[:, -1], s_im[:, -1]], axis=-1)
    return out, h_last


def gla_mixer(q, k, v, g_lr, s0, w_gate, b_gate, norm_g):
    f32 = jnp.float32
    bsz, t, _ = q.shape
    log_a = jax.nn.log_sigmoid(g_lr.astype(f32) @ w_gate.astype(f32) + b_gate.astype(f32)) / GLA_GATE_TAU
    qf = q.astype(f32) * GLA_DK ** -0.5
    pad = (-t) % GLA_CHUNK
    n_chunks = (t + pad) // GLA_CHUNK

    def to_chunks(z, dh):
        z = jnp.pad(z, ((0, 0), (0, pad), (0, 0)))
        z = z.reshape(bsz, n_chunks, GLA_CHUNK, GLA_HEADS, dh)
        return z.transpose(1, 0, 3, 2, 4)

    qc = to_chunks(qf, GLA_DK)
    kc = to_chunks(k.astype(f32), GLA_DK)
    vc = to_chunks(v.astype(f32), GLA_DV)
    lc = to_chunks(log_a, GLA_DK)
    causal = jnp.tril(jnp.ones((GLA_CHUNK, GLA_CHUNK), dtype=bool))

    def chunk_step(s, inp):
        qb, kb, vb, lb = inp
        b = jnp.cumsum(lb, axis=2)
        rel = b[:, :, :, None, :] - b[:, :, None, :, :]
        dec = jnp.exp(jnp.where(causal[:, :, None], rel, -jnp.inf))
        att = jnp.einsum("bhid,bhjd,bhijd->bhij", qb, kb, dec)
        o = (jnp.einsum("bhij,bhjv->bhiv", att, vb)
             + jnp.einsum("bhid,bhdv->bhiv", qb * jnp.exp(b), s))
        b_last = b[:, :, -1:, :]
        s_new = (jnp.exp(b_last[:, :, 0, :])[..., None] * s
                 + jnp.einsum("bhjd,bhjv->bhdv", kb * jnp.exp(b_last - b), vb))
        return s_new, o

    s_fin, o = lax.scan(chunk_step, s0.astype(f32), (qc, kc, vc, lc))
    o = o.transpose(1, 0, 3, 2, 4).reshape(bsz, n_chunks * GLA_CHUNK, GLA_HEADS, GLA_DV)[:, :t]
    o = o * lax.rsqrt(jnp.mean(o * o, axis=-1, keepdims=True) + NORM_EPS)
    o = o.reshape(bsz, t, GLA_WIDTH) * norm_g.astype(f32)
    return o, s_fin


def rwkv7_mixer(blk, prev, s0, mu, w0, w2, a0, a2, k_k, k_a, r_k, ln_w, ln_b):
    f32 = jnp.float32
    bsz, t, _ = blk.shape
    bf = blk.astype(f32)
    shifted = jnp.concatenate([prev.astype(f32)[:, None], bf[:, :-1]], axis=1)
    xs = bf + (shifted - bf) * mu.astype(f32)
    c = RWKV_WIDTH
    r, k, v, wl, al = _split(xs, (c, c, c, RWKV_DECAY_RANK, RWKV_A_RANK))
    w_log = -jax.nn.softplus(-(w0.astype(f32) + jnp.tanh(wl) @ w2.astype(f32))) - 0.5
    decay = jnp.exp(-jnp.exp(w_log))
    a = jax.nn.sigmoid(a0.astype(f32) + al @ a2.astype(f32))

    def heads(z):
        return z.reshape(bsz, t, RWKV_HEADS, RWKV_HEAD)

    kk = heads(k * k_k.astype(f32))
    kk = kk / jnp.maximum(jnp.sqrt(jnp.sum(kk * kk, axis=-1, keepdims=True)), 1e-12)
    k = k * (1.0 + (a - 1.0) * k_a.astype(f32))
    rh, kh, vh, wh, ah = heads(r), heads(k), heads(v), heads(decay), heads(a)

    def tm(z):
        return jnp.swapaxes(z, 0, 1)

    def tok_step(s, inp):
        rt, wt, kt, vt, kkt, at = inp
        sa = jnp.einsum("bhvk,bhk->bhv", s, kkt)
        s = (s * wt[:, :, None, :]
             - sa[..., None] * (kkt * at)[:, :, None, :]
             + vt[..., None] * kt[:, :, None, :])
        return s, jnp.einsum("bhvk,bhk->bhv", s, rt)

    s_fin, y = lax.scan(tok_step, s0.astype(f32), (tm(rh), tm(wh), tm(kh), tm(vh), tm(kk), tm(ah)))
    y = tm(y)
    mean = jnp.mean(y, axis=-1, keepdims=True)
    var = jnp.mean(jnp.square(y - mean), axis=-1, keepdims=True)
    y = ((y - mean) * lax.rsqrt(var + RWKV_GN_EPS) * ln_w.astype(f32).reshape(RWKV_HEADS, RWKV_HEAD)
         + ln_b.astype(f32).reshape(RWKV_HEADS, RWKV_HEAD))
    y = y + jnp.sum(rh * kh * r_k.astype(f32), axis=-1, keepdims=True) * vh
    return y.reshape(bsz, t, RWKV_WIDTH), s_fin, bf[:, -1]


def memory_kv(mem, g_mem, w_mk, w_mv):
    bsz = mem.shape[0]
    mn = rms_norm(mem, g_mem)
    k = (mn @ w_mk).reshape(bsz, N_MEM, XA_HEADS, XA_HEAD_DIM)
    v = (mn @ w_mv).reshape(bsz, N_MEM, XA_HEADS, XA_HEAD_DIM)
    return k, v


def cross_attention(q, mem_k, mem_v):
    f32 = jnp.float32
    bsz, t, _ = q.shape
    qh = q.astype(f32).reshape(bsz, t, XA_HEADS, XA_HEAD_DIM) * XA_HEAD_DIM ** -0.5
    s = jnp.einsum("bthd,bmhd->bhtm", qh, mem_k.astype(f32))
    p = jax.nn.softmax(s, axis=-1)
    o = jnp.einsum("bhtm,bmhd->bthd", p, mem_v.astype(f32))
    return o.reshape(bsz, t, XA_WIDTH)


def mixer_layer(x, mem_k, mem_v, h_s5, s_gla, s_rw, prev_rw, lp):
    f32 = jnp.float32
    xn = rms_norm(x, lp["g_pre"])
    h = jnp.einsum("btd,dc->btc", xn, lp["w_in"])
    s5_u, s5_gate, g_q, g_k, g_v, g_gate, g_lr, rw_blk, rw_gate, xa_q = _split(h, IN_SPLIT)
    y_s5, h_s5_new = s5_mixer(s5_u, h_s5, lp["s5_lam_re"], lp["s5_lam_im"], lp["s5_log_step"],
                              lp["s5_b_re"], lp["s5_b_im"], lp["s5_c_re"], lp["s5_c_im"],
                              lp["s5_d"], lp["s5_w_glu"], lp["s5_b_glu"])
    y_gla, s_gla_new = gla_mixer(g_q, g_k, g_v, g_lr, s_gla, lp["gla_w_gate"], lp["gla_b_gate"],
                                 lp["gla_norm_g"])
    y_rw, s_rw_new, prev_new = rwkv7_mixer(rw_blk, prev_rw, s_rw, lp["rw_mu"], lp["rw_w0"], lp["rw_w2"],
                                           lp["rw_a0"], lp["rw_a2"], lp["rw_k_k"], lp["rw_k_a"],
                                           lp["rw_r_k"], lp["rw_ln_w"], lp["rw_ln_b"])
    y_xa = cross_attention(xa_q, mem_k, mem_v)
    mixed = jnp.concatenate([
        y_s5 * jax.nn.silu(s5_gate.astype(f32)),
        y_gla * jax.nn.silu(g_gate.astype(f32)),
        y_rw * jax.nn.silu(rw_gate.astype(f32)),
        y_xa], axis=-1).astype(x.dtype)
    out = jnp.einsum("btm,md->btd", mixed, lp["w_out"])
    x = x + rms_norm(out, lp["g_post"])
    return x, h_s5_new, s_gla_new, s_rw_new, prev_new


def setup_inputs(seed: int = 0) -> dict:
    key = jax.random.key(seed)
    ks = iter(jax.random.split(key, 48))
    f32 = jnp.float32
    L, D = DEPTH, D_MODEL

    def nrm(shape, scale):
        return jax.random.normal(next(ks), shape, f32) * scale

    def uni(shape, lo, hi):
        return jax.random.uniform(next(ks), shape, f32, lo, hi)

    return {
        "x_prompt": nrm((BATCH, SEQ, D), 1.0),
        "x_sample": nrm((DEC_BATCH, DEC_SEQ, D), 1.0),
        "mem_prompt": nrm((BATCH, N_MEM, D), 1.0),
        "cache_mem_k": nrm((L, DEC_BATCH, N_MEM, XA_HEADS, XA_HEAD_DIM), 1.0),
        "cache_mem_v": nrm((L, DEC_BATCH, N_MEM, XA_HEADS, XA_HEAD_DIM), 1.0),
        "state_s5": nrm((L, DEC_BATCH, S5_GROUPS, S5_STATE, 2), 1.0),
        "state_gla": nrm((L, DEC_BATCH, GLA_HEADS, GLA_DK, GLA_DV), 1.0),
        "state_rwkv": nrm((L, DEC_BATCH, RWKV_HEADS, RWKV_HEAD, RWKV_HEAD), 0.5),
        "state_rwkv_shift": nrm((L, DEC_BATCH, RWKV_SHIFT_COLS), 1.0),
        "g_pre": 1.0 + nrm((L, D), 0.05),
        "g_post": 1.0 + nrm((L, D), 0.05),
        "w_in": nrm((L, D, IN_COLS), D ** -0.5),
        "w_out": nrm((L, MIX_WIDTH, D), MIX_WIDTH ** -0.5),
        "s5_lam_re": -0.5 + nrm((L, S5_GROUPS, S5_STATE), 0.01),
        "s5_lam_im": math.pi * jnp.arange(S5_STATE, dtype=f32) + nrm((L, S5_GROUPS, S5_STATE), 0.01),
        "s5_log_step": uni((L, S5_GROUPS), math.log(1e-3), math.log(1e-1)),
        "s5_b_re": nrm((L, S5_GROUPS, S5_STATE, S5_GROUP), (2 * S5_GROUP) ** -0.5),
        "s5_b_im": nrm((L, S5_GROUPS, S5_STATE, S5_GROUP), (2 * S5_GROUP) ** -0.5),
        "s5_c_re": nrm((L, S5_GROUPS, S5_GROUP, S5_STATE), (2 * S5_STATE) ** -0.5),
        "s5_c_im": nrm((L, S5_GROUPS, S5_GROUP, S5_STATE), (2 * S5_STATE) ** -0.5),
        "s5_d": nrm((L, S5_WIDTH), 1.0),
        "s5_w_glu": nrm((L, S5_WIDTH, S5_WIDTH), S5_WIDTH ** -0.5),
        "s5_b_glu": nrm((L, S5_WIDTH), 0.01),
        "gla_w_gate": nrm((L, GLA_GATE_RANK, GLA_QK), GLA_GATE_RANK ** -0.5),
        "gla_b_gate": nrm((L, GLA_QK), 0.1),
        "gla_norm_g": 1.0 + nrm((L, GLA_WIDTH), 0.05),
        "rw_mu": uni((L, RWKV_SHIFT_COLS), 0.0, 1.0),
        "rw_w0": uni((L, RWKV_WIDTH), -6.0, 1.0),
        "rw_w2": nrm((L, RWKV_DECAY_RANK, RWKV_WIDTH), 0.1),
        "rw_a0": nrm((L, RWKV_WIDTH), 0.1),
        "rw_a2": nrm((L, RWKV_A_RANK, RWKV_WIDTH), 0.5 * RWKV_A_RANK ** -0.5),
        "rw_k_k": 0.85 + nrm((L, RWKV_WIDTH), 0.05),
        "rw_k_a": 1.0 + nrm((L, RWKV_WIDTH), 0.05),
        "rw_r_k": nrm((L, RWKV_HEADS, RWKV_HEAD), 0.1),
        "rw_ln_w": 1.0 + nrm((L, RWKV_WIDTH), 0.05),
        "rw_ln_b": nrm((L, RWKV_WIDTH), 0.01),
        "mem_g": 1.0 + nrm((L, D), 0.05),
        "w_mk": nrm((L, D, XA_WIDTH), D ** -0.5),
        "w_mv": nrm((L, D, XA_WIDTH), D ** -0.5),
    }


def reference(x_prompt, x_sample, mem_prompt, cache_mem_k, cache_mem_v, state_s5, state_gla,
              state_rwkv, state_rwkv_shift, g_pre, g_post, w_in, w_out, s5_lam_re, s5_lam_im,
              s5_log_step, s5_b_re, s5_b_im, s5_c_re, s5_c_im, s5_d, s5_w_glu, s5_b_glu,
              gla_w_gate, gla_b_gate, gla_norm_g, rw_mu, rw_w0, rw_w2, rw_a0, rw_a2, rw_k_k,
              rw_k_a, rw_r_k, rw_ln_w, rw_ln_b, mem_g, w_mk, w_mv):
    f32 = jnp.float32
    bp = x_prompt.shape[0]
    zero_s5 = jnp.zeros((bp, S5_GROUPS, S5_STATE, 2), f32)
    zero_gla = jnp.zeros((bp, GLA_HEADS, GLA_DK, GLA_DV), f32)
    zero_rw = jnp.zeros((bp, RWKV_HEADS, RWKV_HEAD, RWKV_HEAD), f32)
    zero_shift = jnp.zeros((bp, RWKV_SHIFT_COLS), f32)

    xp = x_prompt
    xs = x_sample
    mk_p, mv_p = [], []
    s5_p, s5_s, gla_p, gla_s, rw_p, rw_s, sh_p, sh_s = [], [], [], [], [], [], [], []
    for l in range(DEPTH):
        lp = {
            "g_pre": g_pre[l], "g_post": g_post[l], "w_in": w_in[l], "w_out": w_out[l],
            "s5_lam_re": s5_lam_re[l], "s5_lam_im": s5_lam_im[l], "s5_log_step": s5_log_step[l],
            "s5_b_re": s5_b_re[l], "s5_b_im": s5_b_im[l], "s5_c_re": s5_c_re[l], "s5_c_im": s5_c_im[l],
            "s5_d": s5_d[l], "s5_w_glu": s5_w_glu[l], "s5_b_glu": s5_b_glu[l],
            "gla_w_gate": gla_w_gate[l], "gla_b_gate": gla_b_gate[l], "gla_norm_g": gla_norm_g[l],
            "rw_mu": rw_mu[l], "rw_w0": rw_w0[l], "rw_w2": rw_w2[l], "rw_a0": rw_a0[l],
            "rw_a2": rw_a2[l], "rw_k_k": rw_k_k[l], "rw_k_a": rw_k_a[l], "rw_r_k": rw_r_k[l],
            "rw_ln_w": rw_ln_w[l], "rw_ln_b": rw_ln_b[l],
        }
        k_mem, v_mem = memory_kv(mem_prompt, mem_g[l], w_mk[l], w_mv[l])
        xp, h1, g1, r1, p1 = mixer_layer(xp, k_mem, v_mem, zero_s5, zero_gla, zero_rw, zero_shift, lp)
        xs, h2, g2, r2, p2 = mixer_layer(xs, cache_mem_k[l], cache_mem_v[l], state_s5[l], state_gla[l],
                                         state_rwkv[l], state_rwkv_shift[l], lp)
        mk_p.append(k_mem)
        mv_p.append(v_mem)
        s5_p.append(h1)
        s5_s.append(h2)
        gla_p.append(g1)
        gla_s.append(g2)
        rw_p.append(r1)
        rw_s.append(r2)
        sh_p.append(p1)
        sh_s.append(p2)

    return (xp, xs,
            jnp.stack(mk_p).astype(cache_mem_k.dtype), jnp.stack(mv_p).astype(cache_mem_v.dtype),
            jnp.stack(s5_p).astype(state_s5.dtype), jnp.stack(s5_s).astype(state_s5.dtype),
            jnp.stack(gla_p).astype(state_gla.dtype), jnp.stack(gla_s).astype(state_gla.dtype),
            jnp.stack(rw_p).astype(state_rwkv.dtype), jnp.stack(rw_s).astype(state_rwkv.dtype),
            jnp.stack(sh_p).astype(state_rwkv_shift.dtype), jnp.stack(sh_s).astype(state_rwkv_shift.dtype))
```

```python
import functools

import jax
import jax.numpy as jnp
from jax import lax
from jax.experimental import pallas as pl
from jax.experimental.pallas import tpu as pltpu

f32 = jnp.float32
bf16 = jnp.bfloat16
HI = lax.Precision.HIGHEST

D_MODEL = 4096
DEPTH = 4
N_MEM = 256
NORM_EPS = 1e-6

S5_WIDTH = 1024
S5_GROUP = 16
S5_GROUPS = 64
S5_STATE = 64
S5_NS = S5_GROUPS * S5_STATE
S5_BLOCKS = 8
S5_LANES = 512

GLA_WIDTH = 1024
GLA_HEADS = 4
GLA_DV = 256
GLA_DK = 128
GLA_QK = 512
GLA_GATE_RANK = 16
GLA_GATE_TAU = 16.0

RWKV_WIDTH = 1536
RWKV_HEAD = 64
RWKV_HEADS = 24
RWKV_PAIRS = 12
RWKV_GN_EPS = RWKV_HEAD * 1e-5
RWKV_SHIFT_COLS = 3 * RWKV_WIDTH + 128

XA_HEADS = 4
XA_HEAD_DIM = 128
XA_WIDTH = 512

LANE = 128
SUBLANE = 8
VMEM_LIMIT = 56 * 1024 * 1024

H_COLS = 12288
COL_RW_R, COL_RW_K, COL_RW_V, COL_RW_GATE = 0, 1536, 3072, 4608
COL_S5_U, COL_S5_GATE, COL_G_V, COL_G_GATE = 6144, 7168, 8192, 9216
COL_G_Q, COL_G_K, COL_XA_Q = 10240, 10752, 11264
COL_RW_WA, COL_G_LR = 11776, 11904


def _mm(a, b):
    return jnp.dot(a.astype(bf16), b.astype(bf16), preferred_element_type=f32)


def _mm_nt(a, b):
    return lax.dot_general(a.astype(bf16), b.astype(bf16), (((1,), (1,)), ((), ())), preferred_element_type=f32)


def _mm_tn(a, b):
    return lax.dot_general(a.astype(bf16), b.astype(bf16), (((0,), (0,)), ((), ())), preferred_element_type=f32)


def _mm_hi(a, b):
    return jnp.dot(a, b, precision=HI, preferred_element_type=f32)


def _params(*sem):
    return pltpu.CompilerParams(dimension_semantics=sem, vmem_limit_bytes=VMEM_LIMIT)


def _chunk_tril(n, c, strict=False):
    r = lax.broadcasted_iota(jnp.int32, (n, n), 0)
    q = lax.broadcasted_iota(jnp.int32, (n, n), 1)
    same = (r // c) == (q // c)
    return same & ((q < r) if strict else (q <= r))


def _norm_mm_kernel(x_ref, g_ref, w_ref, o_ref, xn_ref):
    @pl.when(pl.program_id(1) == 0)
    def _():
        x = x_ref[...]
        ms = jnp.mean(x * x, axis=-1, keepdims=True)
        xn_ref[...] = (x * lax.rsqrt(ms + NORM_EPS) * g_ref[...]).astype(bf16)

    o_ref[...] = jnp.dot(xn_ref[...], w_ref[...], preferred_element_type=f32)


def _norm_mm(x, g, w, tm=512, tn=512):
    m, d = x.shape
    n = w.shape[1]
    return pl.pallas_call(
        _norm_mm_kernel,
        grid=(m // tm, n // tn),
        in_specs=[pl.BlockSpec((tm, d), lambda i, j: (i, 0)),
                  pl.BlockSpec((1, d), lambda i, j: (0, 0)),
                  pl.BlockSpec((d, tn), lambda i, j: (0, j))],
        out_specs=pl.BlockSpec((tm, tn), lambda i, j: (i, j)),
        out_shape=jax.ShapeDtypeStruct((m, n), f32),
        scratch_shapes=[pltpu.VMEM((tm, d), bf16)],
        compiler_params=_params("parallel", "arbitrary"),
        name="norm_mm",
    )(x, g, w)


def _outproj_kernel(a1, a2, a3, a4, w1, w2, w3, w4, g_ref, x_ref, o_ref, acc_ref, *, nj, tn):
    j = pl.program_id(1)
    acc = _mm(a1[...], w1[...]) + _mm(a2[...], w2[...]) + _mm(a3[...], w3[...]) + _mm(a4[...], w4[...])
    acc_ref[j] = acc

    @pl.when(j == nj - 1)
    def _():
        ssq = jnp.sum(acc_ref[0] * acc_ref[0], axis=-1, keepdims=True)
        for k in range(1, nj):
            ssq += jnp.sum(acc_ref[k] * acc_ref[k], axis=-1, keepdims=True)
        inv = lax.rsqrt(ssq * (1.0 / D_MODEL) + NORM_EPS)
        for k in range(nj):
            sl = slice(k * tn, (k + 1) * tn)
            o_ref[:, sl] = x_ref[:, sl] + acc_ref[k] * inv * g_ref[:, sl]


def _outproj(parts, ws, g, x, tm=256, tn=512):
    m, d = x.shape
    nj = d // tn
    in_specs = [pl.BlockSpec((tm, p.shape[1]), lambda i, j: (i, 0)) for p in parts]
    in_specs += [pl.BlockSpec((w.shape[0], tn), lambda i, j: (0, j)) for w in ws]
    in_specs += [pl.BlockSpec((1, d), lambda i, j: (0, 0)), pl.BlockSpec((tm, d), lambda i, j: (i, 0))]
    return pl.pallas_call(
        functools.partial(_outproj_kernel, nj=nj, tn=tn),
        grid=(m // tm, nj),
        in_specs=in_specs,
        out_specs=pl.BlockSpec((tm, d), lambda i, j: (i, 0)),
        out_shape=jax.ShapeDtypeStruct((m, d), f32),
        scratch_shapes=[pltpu.VMEM((nj, tm, tn), f32)],
        compiler_params=_params("parallel", "arbitrary"),
        name="outproj",
    )(*parts, *ws, g, x)


def _s5_kernel(u_ref, gate_ref, h0r_ref, h0i_ref, lre_ref, lim_ref, step_ref, bre_ref, bim_ref, cre_ref, cim_ref,
               d_ref, wglu_ref, bglu_ref, y_ref, hr_ref, hi_ref, sre, sim, car, cai, *, tt, chain):
    j = pl.program_id(1)
    nrt = tt // SUBLANE
    lr = lre_ref[...]
    li = lim_ref[...]
    st = jnp.exp(step_ref[...])
    kidx = (lax.broadcasted_iota(jnp.int32, (SUBLANE, S5_NS), 0) + 1).astype(f32)
    mag = jnp.exp(kidx * (lr * st))
    ang = kidx * (li * st)
    pw_re = mag * jnp.cos(ang)
    pw_im = mag * jnp.sin(ang)
    ab_re = pw_re[0:1]
    ab_im = pw_im[0:1]
    den = lr * lr + li * li
    f_re = ((ab_re - 1.0) * lr + ab_im * li) / den
    f_im = (ab_im * lr - (ab_re - 1.0) * li) / den

    u = u_ref[...]
    for blk in range(S5_BLOCKS):
        cols = slice(blk * S5_LANES, (blk + 1) * S5_LANES)
        fr = f_re[:, cols]
        fi = f_im[:, cols]
        b_re = bre_ref[blk]
        b_im = bim_ref[blk]
        ub = u[:, blk * LANE:(blk + 1) * LANE].astype(bf16)
        sre[:, cols] = _mm(ub, fr * b_re - fi * b_im)
        sim[:, cols] = _mm(ub, fr * b_im + fi * b_re)

    if chain:
        @pl.when(j == 0)
        def _():
            car[...] = jnp.broadcast_to(h0r_ref[0], (SUBLANE, S5_NS))
            cai[...] = jnp.broadcast_to(h0i_ref[0], (SUBLANE, S5_NS))

    rowi = lax.broadcasted_iota(jnp.int32, (SUBLANE, S5_LANES), 0)
    for ch in range(S5_NS // S5_LANES):
        cols = slice(ch * S5_LANES, (ch + 1) * S5_LANES)
        p_re = pw_re[:, cols]
        p_im = pw_im[:, cols]
        steps = [(d, p_re[d - 1:d], p_im[d - 1:d]) for d in (1, 2, 4)]

        def body(r, carry, cols=cols, p_re=p_re, p_im=p_im, steps=steps):
            rows = pl.ds(pl.multiple_of(r * SUBLANE, SUBLANE), SUBLANE)
            xr = sre[rows, cols]
            xi = sim[rows, cols]
            for d, ar, ai in steps:
                sr = jnp.where(rowi >= d, pltpu.roll(xr, d, 0), 0.0)
                si = jnp.where(rowi >= d, pltpu.roll(xi, d, 0), 0.0)
                xr, xi = xr + ar * sr - ai * si, xi + ar * si + ai * sr
            if chain:
                cr, ci = carry
            else:
                cr = jnp.broadcast_to(h0r_ref[pl.ds(r, 1), cols], (SUBLANE, S5_LANES))
                ci = jnp.broadcast_to(h0i_ref[pl.ds(r, 1), cols], (SUBLANE, S5_LANES))
            xr, xi = xr + p_re * cr - p_im * ci, xi + p_re * ci + p_im * cr
            sre[rows, cols] = xr
            sim[rows, cols] = xi
            if chain:
                return (jnp.broadcast_to(xr[SUBLANE - 1:SUBLANE], (SUBLANE, S5_LANES)),
                        jnp.broadcast_to(xi[SUBLANE - 1:SUBLANE], (SUBLANE, S5_LANES)))
            hr_ref[pl.ds(r, 1), cols] = xr[SUBLANE - 1:SUBLANE]
            hi_ref[pl.ds(r, 1), cols] = xi[SUBLANE - 1:SUBLANE]
            return carry

        if chain:
            cr, ci = lax.fori_loop(0, nrt, body, (car[:, cols], cai[:, cols]))
            car[:, cols] = cr
            cai[:, cols] = ci
        else:
            lax.fori_loop(0, nrt, body, 0)

    if chain:
        hr_ref[0] = car[0:1, :]
        hi_ref[0] = cai[0:1, :]

    ys = []
    for blk in range(S5_BLOCKS):
        cols = slice(blk * S5_LANES, (blk + 1) * S5_LANES)
        ys.append(_mm(sre[:, cols], cre_ref[blk]) - _mm(sim[:, cols], cim_ref[blk]))
    y = jnp.concatenate(ys, axis=1) + d_ref[...] * u
    z = jax.nn.gelu(y)
    gl = jax.nn.sigmoid(_mm(z, wglu_ref[...]) + bglu_ref[...])
    y_ref[...] = (z * gl * jax.nn.silu(gate_ref[...])).astype(y_ref.dtype)


def _s5(h, h0r, h0i, p, nseq, t, chain):
    m = nseq * t
    if chain:
        tt = min(t, 256)
        nj = t // tt
        grid = (nseq, nj)
        row = lambda b, j: b * nj + j
        h0r = h0r.reshape(nseq, 1, S5_NS)
        h0i = h0i.reshape(nseq, 1, S5_NS)
        h_spec = pl.BlockSpec((1, 1, S5_NS), lambda b, j: (b, 0, 0))
        h_shape = jax.ShapeDtypeStruct((nseq, 1, S5_NS), f32)
    else:
        assert t == SUBLANE
        tt = 64
        nj = m // tt
        grid = (1, nj)
        row = lambda b, j: j
        h_spec = pl.BlockSpec((tt // SUBLANE, S5_NS), lambda b, j: (j, 0))
        h_shape = jax.ShapeDtypeStruct((nseq, S5_NS), f32)
    cu, cg = COL_S5_U // S5_WIDTH, COL_S5_GATE // S5_WIDTH
    vec = lambda n: pl.BlockSpec((1, n), lambda b, j: (0, 0))
    full3 = lambda s: pl.BlockSpec(s, lambda b, j: (0, 0, 0))
    y, hr, hi = pl.pallas_call(
        functools.partial(_s5_kernel, tt=tt, chain=chain),
        grid=grid,
        in_specs=[pl.BlockSpec((tt, S5_WIDTH), lambda b, j: (row(b, j), cu)),
                  pl.BlockSpec((tt, S5_WIDTH), lambda b, j: (row(b, j), cg)),
                  h_spec, h_spec, vec(S5_NS), vec(S5_NS), vec(S5_NS),
                  full3((S5_BLOCKS, LANE, S5_LANES)), full3((S5_BLOCKS, LANE, S5_LANES)),
                  full3((S5_BLOCKS, S5_LANES, LANE)), full3((S5_BLOCKS, S5_LANES, LANE)),
                  vec(S5_WIDTH), pl.BlockSpec((S5_WIDTH, S5_WIDTH), lambda b, j: (0, 0)), vec(S5_WIDTH)],
        out_specs=[pl.BlockSpec((tt, S5_WIDTH), lambda b, j: (row(b, j), 0)), h_spec, h_spec],
        out_shape=[jax.ShapeDtypeStruct((m, S5_WIDTH), f32), h_shape, h_shape],
        scratch_shapes=[pltpu.VMEM((tt, S5_NS), f32), pltpu.VMEM((tt, S5_NS), f32),
                        pltpu.VMEM((SUBLANE, S5_NS), f32), pltpu.VMEM((SUBLANE, S5_NS), f32)],
        compiler_params=_params("parallel" if chain else "arbitrary", "arbitrary"),
        name="s5_chain" if chain else "s5_tiles",
    )(h, h, h0r, h0i, p["lam_re"], p["lam_im"], p["log_step"], p["b_re"], p["b_im"], p["c_re"], p["c_im"],
      p["d"], p["w_glu"], p["b_glu"])
    return y, hr.reshape(nseq, S5_NS), hi.reshape(nseq, S5_NS)


def _gla_kernel(q_ref, k_ref, v_ref, gate_ref, glr_ref, wg_ref, bg_ref, ng_ref, s0_ref, y_ref, s_ref,
                st_ref, b_ref, *, tt, c, nj):
    j = pl.program_id(1)

    @pl.when(j == 0)
    def _():
        for hd in range(GLA_HEADS):
            st_ref[hd] = s0_ref[0, hd].T

    la = jax.nn.log_sigmoid(_mm_hi(glr_ref[...], wg_ref[...]) + bg_ref[...]) * (1.0 / GLA_GATE_TAU)
    tri = jnp.where(_chunk_tril(tt, c), 1.0, 0.0).astype(f32)
    b_ref[...] = _mm_hi(tri, la)

    rowi = lax.broadcasted_iota(jnp.int32, (c, GLA_DK), 0)
    scale = GLA_DK ** -0.5

    def chunk(ci, carry):
        rows = pl.ds(pl.multiple_of(ci * c, c), c)
        for hd in range(GLA_HEADS):
            kl = slice(hd * GLA_DK, (hd + 1) * GLA_DK)
            vl = slice(hd * GLA_DV, (hd + 1) * GLA_DV)
            qh = q_ref[rows, kl] * scale
            kh = k_ref[rows, kl]
            vh = v_ref[rows, vl]
            bh = b_ref[rows, kl]
            o = jnp.zeros((c, GLA_DV), f32)
            for jj in range(c):
                dec = jnp.exp(jnp.where(rowi >= jj, bh - bh[jj:jj + 1], -jnp.inf))
                col = jnp.sum(qh * kh[jj:jj + 1] * dec, axis=-1, keepdims=True)
                o = o + col * vh[jj:jj + 1]
            s_t = st_ref[hd]
            o = o + _mm_nt(qh * jnp.exp(bh), s_t)
            b_last = bh[c - 1:c]
            st_ref[hd] = s_t * jnp.exp(b_last) + _mm_tn(vh, kh * jnp.exp(b_last - bh))
            on = o * lax.rsqrt(jnp.mean(o * o, axis=-1, keepdims=True) + NORM_EPS) * ng_ref[:, vl]
            y_ref[rows, vl] = (on * jax.nn.silu(gate_ref[rows, vl])).astype(y_ref.dtype)
        return carry

    lax.fori_loop(0, tt // c, chunk, 0)

    @pl.when(j == nj - 1)
    def _():
        for hd in range(GLA_HEADS):
            s_ref[0, hd] = st_ref[hd].T


def _gla(h, s0, p, nseq, t):
    m = nseq * t
    c = min(t, 16)
    tt = min(t, 128)
    nj = t // tt
    row = lambda b, j: b * nj + j
    vec = lambda n: pl.BlockSpec((1, n), lambda b, j: (0, 0))
    s_spec = pl.BlockSpec((1, GLA_HEADS, GLA_DK, GLA_DV), lambda b, j: (b, 0, 0, 0))
    return pl.pallas_call(
        functools.partial(_gla_kernel, tt=tt, c=c, nj=nj),
        grid=(nseq, nj),
        in_specs=[pl.BlockSpec((tt, GLA_QK), lambda b, j: (row(b, j), COL_G_Q // GLA_QK)),
                  pl.BlockSpec((tt, GLA_QK), lambda b, j: (row(b, j), COL_G_K // GLA_QK)),
                  pl.BlockSpec((tt, GLA_WIDTH), lambda b, j: (row(b, j), COL_G_V // GLA_WIDTH)),
                  pl.BlockSpec((tt, GLA_WIDTH), lambda b, j: (row(b, j), COL_G_GATE // GLA_WIDTH)),
                  pl.BlockSpec((tt, LANE), lambda b, j: (row(b, j), COL_G_LR // LANE)),
                  pl.BlockSpec((LANE, GLA_QK), lambda b, j: (0, 0)), vec(GLA_QK), vec(GLA_WIDTH), s_spec],
        out_specs=[pl.BlockSpec((tt, GLA_WIDTH), lambda b, j: (row(b, j), 0)), s_spec],
        out_shape=[jax.ShapeDtypeStruct((m, GLA_WIDTH), f32), jax.ShapeDtypeStruct(s0.shape, f32)],
        scratch_shapes=[pltpu.VMEM((GLA_HEADS, GLA_DV, GLA_DK), f32), pltpu.VMEM((tt, GLA_QK), f32)],
        compiler_params=_params("parallel", "arbitrary"),
        name="gla",
    )(h, h, h, h, h, p["w_gate"], p["b_gate"], p["norm_g"], s0)


def _seg_sum(x, e2):
    return jnp.concatenate([_mm_hi(x[:, i * LANE:(i + 1) * LANE], e2) for i in range(RWKV_PAIRS)], axis=1)


def _rwkv_kernel(r_ref, k_ref, v_ref, gate_ref, wa_ref, pr_ref, pk_ref, pv_ref, pwa_ref,
                 mur_ref, muk_ref, muv_ref, muwa_ref, w0_ref, w2_ref, a0_ref, a2_ref, kk_ref, ka_ref, rk_ref,
                 lnw_ref, lnb_ref, s0_ref, y_ref, s_ref,
                 st, c_r, c_k, c_v, c_wa, x_r, x_kk, x_bv, x_k2, x_v, x_lw, x_cl, x_y, *, tt, c, nj):
    j = pl.program_id(1)

    @pl.when(j == 0)
    def _():
        st[...] = s0_ref[0]
        c_r[...] = jnp.broadcast_to(pr_ref[0], c_r.shape)
        c_k[...] = jnp.broadcast_to(pk_ref[0], c_k.shape)
        c_v[...] = jnp.broadcast_to(pv_ref[0], c_v.shape)
        c_wa[...] = jnp.broadcast_to(pwa_ref[0], c_wa.shape)

    def shift(x_ref, c_ref, mu_ref):
        x = x_ref[...]
        first = lax.broadcasted_iota(jnp.int32, x.shape, 0) == 0
        prev = jnp.where(first, c_ref[0:1, :], pltpu.roll(x, 1, 0))
        c_ref[...] = jnp.broadcast_to(x[tt - 1:tt, :], c_ref.shape)
        return x + (prev - x) * mu_ref[...]

    r = shift(r_ref, c_r, mur_ref)
    k = shift(k_ref, c_k, muk_ref)
    v = shift(v_ref, c_v, muv_ref)
    wa = shift(wa_ref, c_wa, muwa_ref)

    w_log = -jax.nn.softplus(-(w0_ref[...] + _mm(jnp.tanh(wa), w2_ref[...]))) - 0.5
    lw = -jnp.exp(w_log)
    a = jax.nn.sigmoid(a0_ref[...] + _mm(wa, a2_ref[...]))

    lane = lax.broadcasted_iota(jnp.int32, (LANE, LANE), 1)
    sub = lax.broadcasted_iota(jnp.int32, (LANE, LANE), 0)
    bd = (lane // RWKV_HEAD) == (sub // RWKV_HEAD)
    e2 = jnp.where(bd, 1.0, 0.0).astype(f32)

    kk = k * kk_ref[...]
    kk = kk / jnp.maximum(jnp.sqrt(_seg_sum(kk * kk, e2)), 1e-12)
    k2 = k * (1.0 + (a - 1.0) * ka_ref[...])
    x_r[...] = r
    x_kk[...] = kk
    x_bv[...] = kk * a
    x_k2[...] = k2
    x_v[...] = v
    x_lw[...] = lw
    x_cl[...] = _mm_hi(jnp.where(_chunk_tril(tt, c), 1.0, 0.0).astype(f32), lw)

    ri = lax.broadcasted_iota(jnp.int32, (c, c), 0)
    qi = lax.broadcasted_iota(jnp.int32, (c, c), 1)
    strict = qi < ri
    incl = qi <= ri
    eye = jnp.where(qi == ri, 1.0, 0.0).astype(f32)
    head_a = lax.broadcasted_iota(jnp.int32, (c, LANE), 1) < RWKV_HEAD

    def sel(xa, xb):
        return jnp.where(head_a, xa, xb)

    def tri_inv(am):
        n = -am
        t = eye + n
        width = 2
        while width < c:
            n = _mm_hi(n, n)
            t = t + _mm_hi(t, n)
            width *= 2
        return t

    for p in range(RWKV_PAIRS):
        ln = slice(p * LANE, (p + 1) * LANE)

        def chunk(ci, carry, ln=ln, p=p):
            rows = pl.ds(pl.multiple_of(ci * c, c), c)
            rr = x_r[rows, ln]
            kkc = x_kk[rows, ln]
            bv = x_bv[rows, ln]
            k2c = x_k2[rows, ln]
            vc = x_v[rows, ln]
            lwc = x_lw[rows, ln]
            cl = x_cl[rows, ln]
            cl_last = cl[c - 1:c]
            kkg = kkc * jnp.exp(cl - lwc)
            rg = rr * jnp.exp(cl)
            ginv = jnp.exp(-cl)
            bi = bv * ginv
            ki = k2c * ginv
            gend = jnp.exp(cl_last - cl)
            lhs = jnp.concatenate([jnp.where(head_a, kkg, 0.0), jnp.where(head_a, 0.0, kkg),
                                   jnp.where(head_a, rg, 0.0), jnp.where(head_a, 0.0, rg)], axis=0)
            pb = _mm_nt(lhs, bi)
            pk = _mm_nt(lhs, ki)
            z = jnp.zeros((c, c), f32)
            a_a = jnp.where(strict, pb[0:c], z)
            a_b = jnp.where(strict, pb[c:2 * c], z)
            ar_a = jnp.where(incl, pb[2 * c:3 * c], z)
            ar_b = jnp.where(incl, pb[3 * c:4 * c], z)
            bm_a = jnp.where(strict, pk[0:c], z)
            bm_b = jnp.where(strict, pk[c:2 * c], z)
            br_a = jnp.where(incl, pk[2 * c:3 * c], z)
            br_b = jnp.where(incl, pk[3 * c:4 * c], z)
            s = st[p]
            rhs = -(_mm_nt(kkg, s) + sel(_mm(bm_a, vc), _mm(bm_b, vc)))
            u = sel(_mm_hi(tri_inv(a_a), rhs), _mm_hi(tri_inv(a_b), rhs))
            y = _mm_nt(rg, s) + sel(_mm(ar_a, u), _mm(ar_b, u)) + sel(_mm(br_a, vc), _mm(br_b, vc))
            x_y[rows, ln] = y
            upd = _mm_tn(u, bv * gend) + _mm_tn(vc, k2c * gend)
            st[p] = s * jnp.exp(cl_last) + jnp.where(bd, upd, 0.0)
            return carry

        lax.fori_loop(0, tt // c, chunk, 0)

    y = x_y[...]
    mean = _seg_sum(y, e2) * (1.0 / RWKV_HEAD)
    yc = y - mean
    var = _seg_sum(yc * yc, e2) * (1.0 / RWKV_HEAD)
    yn = yc * lax.rsqrt(var + RWKV_GN_EPS) * lnw_ref[...] + lnb_ref[...]
    yn = yn + _seg_sum(r * k2 * rk_ref[...], e2) * v
    y_ref[...] = (yn * jax.nn.silu(gate_ref[...])).astype(y_ref.dtype)

    @pl.when(j == nj - 1)
    def _():
        s_ref[0] = st[...]


def _rwkv(h, prev, s0, p, nseq, t):
    m = nseq * t
    c = min(t, 16)
    tt = min(t, 128)
    nj = t // tt
    row = lambda b, j: b * nj + j
    w = RWKV_WIDTH
    vec = lambda n: pl.BlockSpec((1, n), lambda b, j: (0, 0))
    prow = lambda n: pl.BlockSpec((1, 1, n), lambda b, j: (b, 0, 0))
    tok = lambda col: pl.BlockSpec((tt, w), lambda b, j: (row(b, j), col // w))
    s_spec = pl.BlockSpec((1, RWKV_PAIRS, LANE, LANE), lambda b, j: (b, 0, 0, 0))
    big = lambda: pltpu.VMEM((tt, w), f32)
    return pl.pallas_call(
        functools.partial(_rwkv_kernel, tt=tt, c=c, nj=nj),
        grid=(nseq, nj),
        in_specs=[tok(COL_RW_R), tok(COL_RW_K), tok(COL_RW_V), tok(COL_RW_GATE),
                  pl.BlockSpec((tt, LANE), lambda b, j: (row(b, j), COL_RW_WA // LANE)),
                  prow(w), prow(w), prow(w), prow(LANE),
                  vec(w), vec(w), vec(w), vec(LANE),
                  vec(w), pl.BlockSpec((LANE, w), lambda b, j: (0, 0)), vec(w),
                  pl.BlockSpec((LANE, w), lambda b, j: (0, 0)), vec(w), vec(w), vec(w), vec(w), vec(w), s_spec],
        out_specs=[pl.BlockSpec((tt, w), lambda b, j: (row(b, j), 0)), s_spec],
        out_shape=[jax.ShapeDtypeStruct((m, w), f32), jax.ShapeDtypeStruct(s0.shape, f32)],
        scratch_shapes=[pltpu.VMEM((RWKV_PAIRS, LANE, LANE), f32),
                        pltpu.VMEM((SUBLANE, w), f32), pltpu.VMEM((SUBLANE, w), f32),
                        pltpu.VMEM((SUBLANE, w), f32), pltpu.VMEM((SUBLANE, LANE), f32),
                        big(), big(), big(), big(), big(), big(), big(), big()],
        compiler_params=_params("parallel", "arbitrary"),
        name="rwkv",
    )(h, h, h, h, h, *prev, p["mu_r"], p["mu_k"], p["mu_v"], p["mu_wa"], p["w0"], p["w2"], p["a0"], p["a2"],
      p["k_k"], p["k_a"], p["r_k"], p["ln_w"], p["ln_b"], s0)


def _xattn_kernel(q_ref, k_ref, v_ref, o_ref):
    scale = XA_HEAD_DIM ** -0.5
    for hd in range(XA_HEADS):
        ln = slice(hd * XA_HEAD_DIM, (hd + 1) * XA_HEAD_DIM)
        s = _mm_nt(q_ref[:, ln] * scale, k_ref[0, :, ln])
        s = s - jnp.max(s, axis=-1, keepdims=True)
        e = jnp.exp(s)
        pr = e / jnp.sum(e, axis=-1, keepdims=True)
        o_ref[:, ln] = _mm(pr, v_ref[0, :, ln]).astype(o_ref.dtype)


def _xattn(h, mem_k, mem_v, nseq, t):
    m = nseq * t
    tq = min(t, 512)
    nj = t // tq
    mem_spec = pl.BlockSpec((1, N_MEM, XA_WIDTH), lambda b, j: (b, 0, 0))
    return pl.pallas_call(
        _xattn_kernel,
        grid=(nseq, nj),
        in_specs=[pl.BlockSpec((tq, XA_WIDTH), lambda b, j: (b * nj + j, COL_XA_Q // XA_WIDTH)), mem_spec, mem_spec],
        out_specs=pl.BlockSpec((tq, XA_WIDTH), lambda b, j: (b * nj + j, 0)),
        out_shape=jax.ShapeDtypeStruct((m, XA_WIDTH), f32),
        compiler_params=_params("parallel", "arbitrary"),
        name="xattn",
    )(h, mem_k, mem_v)


def _prep_w_in(w_in):
    o_rw = 5136
    seg = [(o_rw, o_rw + 1536), (o_rw + 1536, o_rw + 3072), (o_rw + 3072, o_rw + 4608), (9872, 11408),
           (0, 1024), (1024, 2048), (3072, 4096), (4096, 5120), (2048, 2560), (2560, 3072), (11408, 11920),
           (o_rw + 4608, o_rw + 4736), (5120, 5136)]
    parts = [w_in[:, :, a:b] for a, b in seg]
    used = sum(b - a for a, b in seg)
    parts.append(jnp.zeros(w_in.shape[:2] + (H_COLS - used,), w_in.dtype))
    return jnp.concatenate(parts, axis=2).astype(bf16)


def _block_diag_s5(b, c):
    gl = S5_GROUPS // S5_BLOCKS
    eye = jnp.eye(gl, dtype=b.dtype)
    bb = b.reshape(S5_BLOCKS, gl, S5_STATE, S5_GROUP)
    bbd = jnp.einsum("kgpn,gh->kgnhp", bb, eye).reshape(S5_BLOCKS, gl * S5_GROUP, gl * S5_STATE)
    cc = c.reshape(S5_BLOCKS, gl, S5_GROUP, S5_STATE)
    cbd = jnp.einsum("kgnp,gh->kgphn", cc, eye).reshape(S5_BLOCKS, gl * S5_STATE, gl * S5_GROUP)
    return bbd, cbd


def _layer_params(l, a):
    row = lambda x: x.reshape(1, -1)
    bre, cre = _block_diag_s5(a["s5_b_re"][l], a["s5_c_re"][l])
    bim, cim = _block_diag_s5(a["s5_b_im"][l], a["s5_c_im"][l])
    s5 = dict(lam_re=row(a["s5_lam_re"][l]), lam_im=row(a["s5_lam_im"][l]),
              log_step=row(jnp.broadcast_to(a["s5_log_step"][l][:, None], (S5_GROUPS, S5_STATE))),
              b_re=bre, b_im=bim, c_re=cre.astype(bf16), c_im=cim.astype(bf16),
              d=row(a["s5_d"][l]), w_glu=a["s5_w_glu"][l].astype(bf16), b_glu=row(a["s5_b_glu"][l]))
    wg = jnp.zeros((LANE, GLA_QK), f32).at[:GLA_GATE_RANK].set(a["gla_w_gate"][l])
    gla = dict(w_gate=wg, b_gate=row(a["gla_b_gate"][l]), norm_g=row(a["gla_norm_g"][l]))
    mu = a["rw_mu"][l]
    w = RWKV_WIDTH
    z64 = jnp.zeros((64, w), f32)
    rw = dict(mu_r=row(mu[0:w]), mu_k=row(mu[w:2 * w]), mu_v=row(mu[2 * w:3 * w]), mu_wa=row(mu[3 * w:]),
              w0=row(a["rw_w0"][l]), w2=jnp.concatenate([a["rw_w2"][l], z64], 0).astype(bf16),
              a0=row(a["rw_a0"][l]), a2=jnp.concatenate([z64, a["rw_a2"][l]], 0).astype(bf16),
              k_k=row(a["rw_k_k"][l]), k_a=row(a["rw_k_a"][l]), r_k=row(a["rw_r_k"][l]),
              ln_w=row(a["rw_ln_w"][l]), ln_b=row(a["rw_ln_b"][l]))
    return s5, gla, rw


def _pack_rwkv_state(s):
    b = s.shape[0]
    s = s.reshape(b, RWKV_PAIRS, 2, RWKV_HEAD, RWKV_HEAD)
    z = jnp.zeros_like(s[:, :, 0])
    top = jnp.concatenate([s[:, :, 0], z], axis=-1)
    bot = jnp.concatenate([z, s[:, :, 1]], axis=-1)
    return jnp.concatenate([top, bot], axis=-2)


def _unpack_rwkv_state(s):
    b = s.shape[0]
    s = s.reshape(b, RWKV_PAIRS, 2, RWKV_HEAD, 2, RWKV_HEAD)
    return jnp.stack([s[:, :, 0, :, 0, :], s[:, :, 1, :, 1, :]], axis=2).reshape(b, RWKV_HEADS, RWKV_HEAD, RWKV_HEAD)


def _split_shift(x):
    w = RWKV_WIDTH
    return tuple(x[:, None, a:b] for a, b in ((0, w), (w, 2 * w), (2 * w, 3 * w), (3 * w, 3 * w + LANE)))


def _mixer_layer(x, nseq, t, mem_k, mem_v, h0r, h0i, s_gla, prev, s_rw, chain, lp, w_in_l, w_out_parts, g_pre, g_post):
    s5p, glap, rwp = lp
    h = _norm_mm(x, g_pre, w_in_l)
    y_s5, hr, hi = _s5(h, h0r, h0i, s5p, nseq, t, chain)
    y_gla, s_gla_new = _gla(h, s_gla, glap, nseq, t)
    y_rw, s_rw_new = _rwkv(h, prev, s_rw, rwp, nseq, t)
    y_xa = _xattn(h, mem_k, mem_v, nseq, t)
    x_new = _outproj((y_s5, y_gla, y_rw, y_xa), w_out_parts, g_post, x)
    last = h.reshape(nseq, t, H_COLS)[:, t - 1]
    w = RWKV_WIDTH
    shift = jnp.concatenate([last[:, COL_RW_R:COL_RW_R + w], last[:, COL_RW_K:COL_RW_K + w],
                             last[:, COL_RW_V:COL_RW_V + w], last[:, COL_RW_WA:COL_RW_WA + LANE]], axis=1)
    return x_new, jnp.stack([hr, hi], axis=-1).reshape(nseq, S5_GROUPS, S5_STATE, 2), s_gla_new, s_rw_new, shift


def kernel(x_prompt, x_sample, mem_prompt, cache_mem_k, cache_mem_v, state_s5, state_gla, state_rwkv, state_rwkv_shift, g_pre, g_post, w_in, w_out, s5_lam_re, s5_lam_im, s5_log_step, s5_b_re, s5_b_im, s5_c_re, s5_c_im, s5_d, s5_w_glu, s5_b_glu, gla_w_gate, gla_b_gate, gla_norm_g, rw_mu, rw_w0, rw_w2, rw_a0, rw_a2, rw_k_k, rw_k_a, rw_r_k, rw_ln_w, rw_ln_b, mem_g, w_mk, w_mv):
    raw = dict(s5_lam_re=s5_lam_re, s5_lam_im=s5_lam_im, s5_log_step=s5_log_step, s5_b_re=s5_b_re, s5_b_im=s5_b_im,
               s5_c_re=s5_c_re, s5_c_im=s5_c_im, s5_d=s5_d, s5_w_glu=s5_w_glu, s5_b_glu=s5_b_glu,
               gla_w_gate=gla_w_gate, gla_b_gate=gla_b_gate, gla_norm_g=gla_norm_g, rw_mu=rw_mu, rw_w0=rw_w0,
               rw_w2=rw_w2, rw_a0=rw_a0, rw_a2=rw_a2, rw_k_k=rw_k_k, rw_k_a=rw_k_a, rw_r_k=rw_r_k,
               rw_ln_w=rw_ln_w, rw_ln_b=rw_ln_b)
    bp, tp, d = x_prompt.shape
    bs, ts, _ = x_sample.shape
    depth = w_in.shape[0]
    w_in_p = _prep_w_in(w_in)
    w_out_b = w_out.astype(bf16)
    w_kv = jnp.concatenate([w_mk, w_mv], axis=2).astype(bf16)
    bounds = (0, S5_WIDTH, S5_WIDTH + GLA_WIDTH, S5_WIDTH + GLA_WIDTH + RWKV_WIDTH, D_MODEL)

    xp = x_prompt.reshape(bp * tp, d)
    xs = x_sample.reshape(bs * ts, d)
    mem2d = mem_prompt.reshape(bp * N_MEM, d)
    zp = lambda *s: jnp.zeros(s, f32)
    prev_p = _split_shift(zp(bp, RWKV_SHIFT_COLS))

    outs = [[] for _ in range(10)]
    for l in range(depth):
        lp = _layer_params(l, raw)
        w_out_parts = tuple(w_out_b[l, bounds[i]:bounds[i + 1]] for i in range(4))
        gp, gq = g_pre[l].reshape(1, d), g_post[l].reshape(1, d)
        kv = _norm_mm(mem2d, mem_g[l].reshape(1, d), w_kv[l])
        k_mem = kv[:, :XA_WIDTH].reshape(bp, N_MEM, XA_WIDTH)
        v_mem = kv[:, XA_WIDTH:].reshape(bp, N_MEM, XA_WIDTH)
        xp, h1, g1, r1, p1 = _mixer_layer(
            xp, bp, tp, k_mem, v_mem, zp(bp, S5_NS), zp(bp, S5_NS), zp(bp, GLA_HEADS, GLA_DK, GLA_DV), prev_p,
            zp(bp, RWKV_PAIRS, LANE, LANE), True, lp, w_in_p[l], w_out_parts, gp, gq)
        st5 = state_s5[l].reshape(bs, S5_NS, 2)
        xs, h2, g2, r2, p2 = _mixer_layer(
            xs, bs, ts, cache_mem_k[l].reshape(bs, N_MEM, XA_WIDTH), cache_mem_v[l].reshape(bs, N_MEM, XA_WIDTH),
            st5[..., 0], st5[..., 1], state_gla[l], _split_shift(state_rwkv_shift[l]),
            _pack_rwkv_state(state_rwkv[l]), False, lp, w_in_p[l], w_out_parts, gp, gq)
        vals = (k_mem.reshape(bp, N_MEM, XA_HEADS, XA_HEAD_DIM), v_mem.reshape(bp, N_MEM, XA_HEADS, XA_HEAD_DIM),
                h1, h2, g1, g2, _unpack_rwkv_state(r1), _unpack_rwkv_state(r2), p1, p2)
        for o, val in zip(outs, vals):
            o.append(val)

    return (xp.reshape(bp, tp, d), xs.reshape(bs, ts, d)) + tuple(jnp.stack(o) for o in outs)
```

```python
import functools

import jax
import jax.numpy as jnp
from jax import lax
from jax.experimental import pallas as pl
from jax.experimental.pallas import tpu as pltpu

f32 = jnp.float32
bf16 = jnp.bfloat16
HI = lax.Precision.HIGHEST

D_MODEL = 4096
DEPTH = 4
N_MEM = 256
NORM_EPS = 1e-6

S5_WIDTH = 1024
S5_GROUP = 16
S5_GROUPS = 64
S5_STATE = 64
S5_NS = S5_GROUPS * S5_STATE
S5_BLOCKS = 8
S5_LANES = 512

GLA_WIDTH = 1024
GLA_HEADS = 4
GLA_DV = 256
GLA_DK = 128
GLA_QK = 512
GLA_GATE_RANK = 16
GLA_GATE_TAU = 16.0

RWKV_WIDTH = 1536
RWKV_HEAD = 64
RWKV_HEADS = 24
RWKV_PAIRS = 12
RWKV_GN_EPS = RWKV_HEAD * 1e-5
RWKV_SHIFT_COLS = 3 * RWKV_WIDTH + 128

XA_HEADS = 4
XA_HEAD_DIM = 128
XA_WIDTH = 512

LANE = 128
SUBLANE = 8
VMEM_LIMIT = 56 * 1024 * 1024

H_COLS = 12288
COL_RW_R, COL_RW_K, COL_RW_V, COL_RW_GATE = 0, 1536, 3072, 4608
COL_S5_U, COL_S5_GATE, COL_G_V, COL_G_GATE = 6144, 7168, 8192, 9216
COL_G_Q, COL_G_K, COL_XA_Q = 10240, 10752, 11264
COL_RW_WA, COL_G_LR = 11776, 11904


def _mm(a, b):
    return jnp.dot(a.astype(bf16), b.astype(bf16), preferred_element_type=f32)


def _mm_nt(a, b):
    return lax.dot_general(a.astype(bf16), b.astype(bf16), (((1,), (1,)), ((), ())), preferred_element_type=f32)


def _mm_tn(a, b):
    return lax.dot_general(a.astype(bf16), b.astype(bf16), (((0,), (0,)), ((), ())), preferred_element_type=f32)


def _mm_hi(a, b):
    return jnp.dot(a, b, precision=HI, preferred_element_type=f32)


def _split2(x):
    hi = x.astype(bf16)
    return hi, (x - hi.astype(f32)).astype(bf16)


def _mm_mask_l(mask01, x):
    hi, lo = _split2(x)
    m = mask01.astype(bf16)
    return jnp.dot(m, hi, preferred_element_type=f32) + jnp.dot(m, lo, preferred_element_type=f32)


def _mm_mask_r(x, mask01):
    hi, lo = _split2(x)
    m = mask01.astype(bf16)
    return jnp.dot(hi, m, preferred_element_type=f32) + jnp.dot(lo, m, preferred_element_type=f32)


def _mix_dtype(t):
    return bf16 if t % (2 * SUBLANE) == 0 else f32


def _params(*sem):
    return pltpu.CompilerParams(dimension_semantics=sem, vmem_limit_bytes=VMEM_LIMIT)


def _chunk_tril(n, c, strict=False):
    r = lax.broadcasted_iota(jnp.int32, (n, n), 0)
    q = lax.broadcasted_iota(jnp.int32, (n, n), 1)
    same = (r // c) == (q // c)
    return same & ((q < r) if strict else (q <= r))


def _norm_mm_kernel(x_ref, g_ref, w_ref, o_ref, xn_ref):
    @pl.when(pl.program_id(1) == 0)
    def _():
        x = x_ref[...]
        ms = jnp.mean(x * x, axis=-1, keepdims=True)
        xn_ref[...] = (x * lax.rsqrt(ms + NORM_EPS) * g_ref[...]).astype(bf16)

    o_ref[...] = jnp.dot(xn_ref[...], w_ref[...], preferred_element_type=f32)


def _norm_mm(x, g, w, tm=512, tn=512):
    m, d = x.shape
    n = w.shape[1]
    return pl.pallas_call(
        _norm_mm_kernel,
        grid=(m // tm, n // tn),
        in_specs=[pl.BlockSpec((tm, d), lambda i, j: (i, 0)),
                  pl.BlockSpec((1, d), lambda i, j: (0, 0)),
                  pl.BlockSpec((d, tn), lambda i, j: (0, j))],
        out_specs=pl.BlockSpec((tm, tn), lambda i, j: (i, j)),
        out_shape=jax.ShapeDtypeStruct((m, n), f32),
        scratch_shapes=[pltpu.VMEM((tm, d), bf16)],
        compiler_params=_params("parallel", "arbitrary"),
        name="norm_mm",
    )(x, g, w)


def _outproj_kernel(a1, a2, a3, a4, w1, w2, w3, w4, g_ref, x_ref, o_ref, acc_ref, *, nj, tn):
    j = pl.program_id(1)
    acc = _mm(a1[...], w1[...]) + _mm(a2[...], w2[...]) + _mm(a3[...], w3[...]) + _mm(a4[...], w4[...])
    acc_ref[j] = acc

    @pl.when(j == nj - 1)
    def _():
        ssq = jnp.sum(acc_ref[0] * acc_ref[0], axis=-1, keepdims=True)
        for k in range(1, nj):
            ssq += jnp.sum(acc_ref[k] * acc_ref[k], axis=-1, keepdims=True)
        inv = lax.rsqrt(ssq * (1.0 / D_MODEL) + NORM_EPS)
        for k in range(nj):
            sl = slice(k * tn, (k + 1) * tn)
            o_ref[:, sl] = x_ref[:, sl] + acc_ref[k] * inv * g_ref[:, sl]


def _outproj(parts, ws, g, x, tn=512):
    m, d = x.shape
    nj = d // tn
    tm = 512 if parts[0].dtype == bf16 else 256
    in_specs = [pl.BlockSpec((tm, p.shape[1]), lambda i, j: (i, 0)) for p in parts]
    in_specs += [pl.BlockSpec((w.shape[0], tn), lambda i, j: (0, j)) for w in ws]
    in_specs += [pl.BlockSpec((1, d), lambda i, j: (0, 0)),
                 pl.BlockSpec((tm, d), lambda i, j: (i, 0), pipeline_mode=pl.Buffered(1))]
    return pl.pallas_call(
        functools.partial(_outproj_kernel, nj=nj, tn=tn),
        grid=(m // tm, nj),
        in_specs=in_specs,
        out_specs=pl.BlockSpec((tm, d), lambda i, j: (i, 0)),
        out_shape=jax.ShapeDtypeStruct((m, d), f32),
        scratch_shapes=[pltpu.VMEM((nj, tm, tn), f32)],
        compiler_params=_params("parallel", "arbitrary"),
        name="outproj",
    )(*parts, *ws, g, x)


def _s5_kernel(u_ref, gate_ref, h0r_ref, h0i_ref, lre_ref, lim_ref, step_ref, bre_ref, bim_ref, cre_ref, cim_ref,
               d_ref, wglu_ref, bglu_ref, y_ref, hr_ref, hi_ref, sre, sim, car, cai, *, tt, chain):
    j = pl.program_id(1)
    nrt = tt // SUBLANE
    lr = lre_ref[...]
    li = lim_ref[...]
    st = jnp.exp(step_ref[...])
    kidx = (lax.broadcasted_iota(jnp.int32, (SUBLANE, S5_NS), 0) + 1).astype(f32)
    mag = jnp.exp(kidx * (lr * st))
    ang = kidx * (li * st)
    pw_re = mag * jnp.cos(ang)
    pw_im = mag * jnp.sin(ang)
    ab_re = pw_re[0:1]
    ab_im = pw_im[0:1]
    den = lr * lr + li * li
    f_re = ((ab_re - 1.0) * lr + ab_im * li) / den
    f_im = (ab_im * lr - (ab_re - 1.0) * li) / den

    u = u_ref[...]
    for blk in range(S5_BLOCKS):
        cols = slice(blk * S5_LANES, (blk + 1) * S5_LANES)
        fr = f_re[:, cols]
        fi = f_im[:, cols]
        b_re = bre_ref[blk]
        b_im = bim_ref[blk]
        ub = u[:, blk * LANE:(blk + 1) * LANE].astype(bf16)
        sre[:, cols] = _mm(ub, fr * b_re - fi * b_im)
        sim[:, cols] = _mm(ub, fr * b_im + fi * b_re)

    if chain:
        @pl.when(j == 0)
        def _():
            car[...] = jnp.broadcast_to(h0r_ref[0], (SUBLANE, S5_NS))
            cai[...] = jnp.broadcast_to(h0i_ref[0], (SUBLANE, S5_NS))

    rowi = lax.broadcasted_iota(jnp.int32, (SUBLANE, S5_LANES), 0)
    for ch in range(S5_NS // S5_LANES):
        cols = slice(ch * S5_LANES, (ch + 1) * S5_LANES)
        p_re = pw_re[:, cols]
        p_im = pw_im[:, cols]
        steps = [(d, p_re[d - 1:d], p_im[d - 1:d]) for d in (1, 2, 4)]

        def body(r, carry, cols=cols, p_re=p_re, p_im=p_im, steps=steps):
            rows = pl.ds(pl.multiple_of(r * SUBLANE, SUBLANE), SUBLANE)
            xr = sre[rows, cols]
            xi = sim[rows, cols]
            for d, ar, ai in steps:
                sr = jnp.where(rowi >= d, pltpu.roll(xr, d, 0), 0.0)
                si = jnp.where(rowi >= d, pltpu.roll(xi, d, 0), 0.0)
                xr, xi = xr + ar * sr - ai * si, xi + ar * si + ai * sr
            if chain:
                cr, ci = carry
            else:
                cr = jnp.broadcast_to(h0r_ref[pl.ds(r, 1), cols], (SUBLANE, S5_LANES))
                ci = jnp.broadcast_to(h0i_ref[pl.ds(r, 1), cols], (SUBLANE, S5_LANES))
            xr, xi = xr + p_re * cr - p_im * ci, xi + p_re * ci + p_im * cr
            sre[rows, cols] = xr
            sim[rows, cols] = xi
            if chain:
                return (jnp.broadcast_to(xr[SUBLANE - 1:SUBLANE], (SUBLANE, S5_LANES)),
                        jnp.broadcast_to(xi[SUBLANE - 1:SUBLANE], (SUBLANE, S5_LANES)))
            hr_ref[pl.ds(r, 1), cols] = xr[SUBLANE - 1:SUBLANE]
            hi_ref[pl.ds(r, 1), cols] = xi[SUBLANE - 1:SUBLANE]
            return carry

        if chain:
            cr, ci = lax.fori_loop(0, nrt, body, (car[:, cols], cai[:, cols]))
            car[:, cols] = cr
            cai[:, cols] = ci
        else:
            lax.fori_loop(0, nrt, body, 0)

    if chain:
        hr_ref[0] = car[0:1, :]
        hi_ref[0] = cai[0:1, :]

    ys = []
    for blk in range(S5_BLOCKS):
        cols = slice(blk * S5_LANES, (blk + 1) * S5_LANES)
        ys.append(_mm(sre[:, cols], cre_ref[blk]) - _mm(sim[:, cols], cim_ref[blk]))
    y = jnp.concatenate(ys, axis=1) + d_ref[...] * u
    z = jax.nn.gelu(y)
    gl = jax.nn.sigmoid(_mm(z, wglu_ref[...]) + bglu_ref[...])
    y_ref[...] = (z * gl * jax.nn.silu(gate_ref[...])).astype(y_ref.dtype)


def _s5(h, h0r, h0i, p, nseq, t, chain):
    m = nseq * t
    if chain:
        tt = min(t, 256)
        nj = t // tt
        grid = (nseq, nj)
        row = lambda b, j: b * nj + j
        h0r = h0r.reshape(nseq, 1, S5_NS)
        h0i = h0i.reshape(nseq, 1, S5_NS)
        h_spec = pl.BlockSpec((1, 1, S5_NS), lambda b, j: (b, 0, 0))
        h_shape = jax.ShapeDtypeStruct((nseq, 1, S5_NS), f32)
    else:
        assert t == SUBLANE
        tt = 64
        nj = m // tt
        grid = (1, nj)
        row = lambda b, j: j
        h_spec = pl.BlockSpec((tt // SUBLANE, S5_NS), lambda b, j: (j, 0))
        h_shape = jax.ShapeDtypeStruct((nseq, S5_NS), f32)
    cu, cg = COL_S5_U // S5_WIDTH, COL_S5_GATE // S5_WIDTH
    vec = lambda n: pl.BlockSpec((1, n), lambda b, j: (0, 0))
    full3 = lambda s: pl.BlockSpec(s, lambda b, j: (0, 0, 0))
    y, hr, hi = pl.pallas_call(
        functools.partial(_s5_kernel, tt=tt, chain=chain),
        grid=grid,
        in_specs=[pl.BlockSpec((tt, S5_WIDTH), lambda b, j: (row(b, j), cu)),
                  pl.BlockSpec((tt, S5_WIDTH), lambda b, j: (row(b, j), cg)),
                  h_spec, h_spec, vec(S5_NS), vec(S5_NS), vec(S5_NS),
                  full3((S5_BLOCKS, LANE, S5_LANES)), full3((S5_BLOCKS, LANE, S5_LANES)),
                  full3((S5_BLOCKS, S5_LANES, LANE)), full3((S5_BLOCKS, S5_LANES, LANE)),
                  vec(S5_WIDTH), pl.BlockSpec((S5_WIDTH, S5_WIDTH), lambda b, j: (0, 0)), vec(S5_WIDTH)],
        out_specs=[pl.BlockSpec((tt, S5_WIDTH), lambda b, j: (row(b, j), 0)), h_spec, h_spec],
        out_shape=[jax.ShapeDtypeStruct((m, S5_WIDTH), _mix_dtype(t)), h_shape, h_shape],
        scratch_shapes=[pltpu.VMEM((tt, S5_NS), f32), pltpu.VMEM((tt, S5_NS), f32),
                        pltpu.VMEM((SUBLANE, S5_NS), f32), pltpu.VMEM((SUBLANE, S5_NS), f32)],
        compiler_params=_params("parallel" if chain else "arbitrary", "arbitrary"),
        name="s5_chain" if chain else "s5_tiles",
    )(h, h, h0r, h0i, p["lam_re"], p["lam_im"], p["log_step"], p["b_re"], p["b_im"], p["c_re"], p["c_im"],
      p["d"], p["w_glu"], p["b_glu"])
    return y, hr.reshape(nseq, S5_NS), hi.reshape(nseq, S5_NS)


def _gla_kernel(q_ref, k_ref, v_ref, gate_ref, glr_ref, wg_ref, bg_ref, ng_ref, s0_ref, acc_ref, y_ref, s_ref,
                st_ref, b_ref, *, tt, c, nj):
    del acc_ref
    j = pl.program_id(1)

    @pl.when(j == 0)
    def _():
        for hd in range(GLA_HEADS):
            st_ref[hd] = s0_ref[0, hd].T

    la = jax.nn.log_sigmoid(_mm_hi(glr_ref[...], wg_ref[...]) + bg_ref[...]) * (1.0 / GLA_GATE_TAU)
    tri = jnp.where(_chunk_tril(tt, c), 1.0, 0.0).astype(f32)
    b_ref[...] = _mm_hi(tri, la)

    rowi = lax.broadcasted_iota(jnp.int32, (c, GLA_DK), 0)
    scale = GLA_DK ** -0.5

    def chunk(ci, carry):
        rows = pl.ds(pl.multiple_of(ci * c, c), c)
        for hd in range(GLA_HEADS):
            kl = slice(hd * GLA_DK, (hd + 1) * GLA_DK)
            vl = slice(hd * GLA_DV, (hd + 1) * GLA_DV)
            qh = q_ref[rows, kl] * scale
            kh = k_ref[rows, kl]
            vh = v_ref[rows, vl]
            bh = b_ref[rows, kl]
            o = jnp.zeros((c, GLA_DV), f32)
            for jj in range(c):
                dec = jnp.exp(jnp.where(rowi >= jj, bh - bh[jj:jj + 1], -jnp.inf))
                col = jnp.sum(qh * kh[jj:jj + 1] * dec, axis=-1, keepdims=True)
                o = o + col * vh[jj:jj + 1]
            s_t = st_ref[hd]
            o = o + _mm_nt(qh * jnp.exp(bh), s_t)
            b_last = bh[c - 1:c]
            st_ref[hd] = s_t * jnp.exp(b_last) + _mm_tn(vh, kh * jnp.exp(b_last - bh))
            on = o * lax.rsqrt(jnp.mean(o * o, axis=-1, keepdims=True) + NORM_EPS) * ng_ref[:, vl]
            y_ref[rows, vl] = (on * jax.nn.silu(gate_ref[rows, vl])).astype(y_ref.dtype)
        return carry

    lax.fori_loop(0, tt // c, chunk, 0)

    @pl.when(j == nj - 1)
    def _():
        for hd in range(GLA_HEADS):
            s_ref[0, hd] = st_ref[hd].T


def _gla(h, s0, l0, acc, l, p, nseq, t):
    m = nseq * t
    c = min(t, 16)
    tt = min(t, 128)
    nj = t // tt
    row = lambda b, j: b * nj + j
    vec = lambda n: pl.BlockSpec((1, n), lambda b, j: (0, 0))
    st_spec = lambda li: pl.BlockSpec((None, 1, GLA_HEADS, GLA_DK, GLA_DV), lambda b, j: (li, b, 0, 0, 0))
    return pl.pallas_call(
        functools.partial(_gla_kernel, tt=tt, c=c, nj=nj),
        grid=(nseq, nj),
        in_specs=[pl.BlockSpec((tt, GLA_QK), lambda b, j: (row(b, j), COL_G_Q // GLA_QK)),
                  pl.BlockSpec((tt, GLA_QK), lambda b, j: (row(b, j), COL_G_K // GLA_QK)),
                  pl.BlockSpec((tt, GLA_WIDTH), lambda b, j: (row(b, j), COL_G_V // GLA_WIDTH)),
                  pl.BlockSpec((tt, GLA_WIDTH), lambda b, j: (row(b, j), COL_G_GATE // GLA_WIDTH)),
                  pl.BlockSpec((tt, LANE), lambda b, j: (row(b, j), COL_G_LR // LANE)),
                  pl.BlockSpec((LANE, GLA_QK), lambda b, j: (0, 0)), vec(GLA_QK), vec(GLA_WIDTH), st_spec(l0),
                  pl.BlockSpec(memory_space=pl.ANY)],
        out_specs=[pl.BlockSpec((tt, GLA_WIDTH), lambda b, j: (row(b, j), 0)), st_spec(l)],
        out_shape=[jax.ShapeDtypeStruct((m, GLA_WIDTH), _mix_dtype(t)), jax.ShapeDtypeStruct(acc.shape, f32)],
        scratch_shapes=[pltpu.VMEM((GLA_HEADS, GLA_DV, GLA_DK), f32), pltpu.VMEM((tt, GLA_QK), f32)],
        input_output_aliases={9: 1},
        compiler_params=_params("parallel", "arbitrary"),
        name="gla",
    )(h, h, h, h, h, p["w_gate"], p["b_gate"], p["norm_g"], s0, acc)


def _seg_sum(x, e2):
    return jnp.concatenate([_mm_mask_r(x[:, i * LANE:(i + 1) * LANE], e2) for i in range(RWKV_PAIRS)], axis=1)


def _rwkv_kernel(r_ref, k_ref, v_ref, gate_ref, wa_ref, pr_ref, pk_ref, pv_ref, pwa_ref,
                 mur_ref, muk_ref, muv_ref, muwa_ref, w0_ref, w2_ref, a0_ref, a2_ref, kk_ref, ka_ref, rk_ref,
                 lnw_ref, lnb_ref, s0_ref, acc_ref, y_ref, s_ref,
                 st, c_r, c_k, c_v, c_wa, x_kkg, x_rg, x_bi, x_ki, x_bend, x_kend, x_gam, x_v, x_bon,
                 x_tk, x_rq, x_uv, x_yv, x_y, *, tt, c, nj, chain):
    del acc_ref
    j = pl.program_id(1)
    hd = RWKV_HEAD

    if chain:
        @pl.when(j == 0)
        def _():
            st[...] = s0_ref[0]
            c_r[...] = jnp.broadcast_to(pr_ref[0], c_r.shape)
            c_k[...] = jnp.broadcast_to(pk_ref[0], c_k.shape)
            c_v[...] = jnp.broadcast_to(pv_ref[0], c_v.shape)
            c_wa[...] = jnp.broadcast_to(pwa_ref[0], c_wa.shape)
    else:
        @pl.when(j == 0)
        def _():
            st[...] = jnp.zeros(st.shape, f32)

    def shift(x_ref, c_ref, p_ref, mu_ref):
        x = x_ref[...]
        rowid = lax.broadcasted_iota(jnp.int32, x.shape, 0)
        rolled = pltpu.roll(x, 1, 0)
        if chain:
            prev = jnp.where(rowid == 0, c_ref[0:1, :], rolled)
            c_ref[...] = jnp.broadcast_to(x[tt - 1:tt, :], c_ref.shape)
        else:
            prev = jnp.where(rowid % c == 0, p_ref[...], rolled)
        return x + (prev - x) * mu_ref[...]

    r = shift(r_ref, c_r, pr_ref, mur_ref)
    k = shift(k_ref, c_k, pk_ref, muk_ref)
    v = shift(v_ref, c_v, pv_ref, muv_ref)
    wa = shift(wa_ref, c_wa, pwa_ref, muwa_ref)

    w_log = -jax.nn.softplus(-(w0_ref[...] + _mm(jnp.tanh(wa), w2_ref[...]))) - 0.5
    lw = -jnp.exp(w_log)
    a = jax.nn.sigmoid(a0_ref[...] + _mm(wa, a2_ref[...]))

    lane = lax.broadcasted_iota(jnp.int32, (LANE, LANE), 1)
    sub = lax.broadcasted_iota(jnp.int32, (LANE, LANE), 0)
    bd = (lane // RWKV_HEAD) == (sub // RWKV_HEAD)
    e2 = jnp.where(bd, 1.0, 0.0).astype(f32)

    kk = k * kk_ref[...]
    kk = kk / jnp.maximum(jnp.sqrt(_seg_sum(kk * kk, e2)), 1e-12)
    k2 = k * (1.0 + (a - 1.0) * ka_ref[...])
    bv = kk * a
    ri = lax.broadcasted_iota(jnp.int32, (tt, tt), 0)
    qi = lax.broadcasted_iota(jnp.int32, (tt, tt), 1)
    same = (ri // c) == (qi // c)
    strict = same & (qi < ri)
    incl = same & (qi <= ri)
    eye = jnp.where(qi == ri, 1.0, 0.0).astype(f32)
    cl = _mm_mask_l(jnp.where(incl, 1.0, 0.0), lw)
    cll = _mm_mask_l(jnp.where(same, 1.0, 0.0), lw)
    ginv = jnp.exp(-cl)
    gend = jnp.exp(cll - cl)
    x_kkg[...] = kk * jnp.exp(cl - lw)
    x_rg[...] = r * jnp.exp(cl)
    x_bi[...] = bv * ginv
    x_ki[...] = k2 * ginv
    x_bend[...] = bv * gend
    x_kend[...] = k2 * gend
    x_gam[...] = jnp.exp(cll)
    x_v[...] = v
    x_bon[...] = _seg_sum(r * k2 * rk_ref[...], e2) * v

    head_a = lax.broadcasted_iota(jnp.int32, (tt, LANE), 1) < RWKV_HEAD

    def sel(xa, xb):
        return jnp.where(head_a, xa, xb)

    def tri_inv(am):
        n = -am
        t = eye + n
        width = 2
        while width < c:
            n = _mm(n, n)
            t = t + _mm(t, n)
            width *= 2
        return t

    for p in range(RWKV_PAIRS):
        ln = slice(p * LANE, (p + 1) * LANE)
        kkg = x_kkg[:, ln]
        rg = x_rg[:, ln]
        bi = x_bi[:, ln]
        ki = x_ki[:, ln]
        vp = x_v[:, ln]
        lhs = jnp.concatenate([jnp.where(head_a, kkg, 0.0), jnp.where(head_a, 0.0, kkg),
                               jnp.where(head_a, rg, 0.0), jnp.where(head_a, 0.0, rg)], axis=0)
        pb = _mm_nt(lhs, bi)
        pk = _mm_nt(lhs, ki)
        t_a = tri_inv(jnp.where(strict, pb[0:tt], 0.0))
        t_b = tri_inv(jnp.where(strict, pb[tt:2 * tt], 0.0))
        ar_a = jnp.where(incl, pb[2 * tt:3 * tt], 0.0)
        ar_b = jnp.where(incl, pb[3 * tt:4 * tt], 0.0)
        bm_a = jnp.where(strict, pk[0:tt], 0.0)
        bm_b = jnp.where(strict, pk[tt:2 * tt], 0.0)
        br_a = jnp.where(incl, pk[2 * tt:3 * tt], 0.0)
        br_b = jnp.where(incl, pk[3 * tt:4 * tt], 0.0)
        bmv = sel(_mm(bm_a, vp), _mm(bm_b, vp))
        uv = -sel(_mm(t_a, bmv), _mm(t_b, bmv))
        tk = sel(_mm(t_a, kkg), _mm(t_b, kkg))
        x_uv[:, ln] = uv
        x_tk[:, ln] = tk
        x_yv[:, ln] = sel(_mm(ar_a, uv), _mm(ar_b, uv)) + sel(_mm(br_a, vp), _mm(br_b, vp))
        x_rq[:, ln] = rg - sel(_mm(ar_a, tk), _mm(ar_b, tk))

    def chunk(ci, carry):
        rows = pl.ds(pl.multiple_of(ci * c, c), c)
        for p in range(RWKV_PAIRS):
            ln = slice(p * LANE, (p + 1) * LANE)
            if not chain:
                st[p, 0:hd, 0:hd] = s0_ref[ci, 2 * p]
                st[p, hd:2 * hd, hd:2 * hd] = s0_ref[ci, 2 * p + 1]
            s = st[p]
            zy = _mm_nt(jnp.concatenate([x_tk[rows, ln], x_rq[rows, ln]], axis=0), s)
            u = x_uv[rows, ln] - zy[0:c]
            x_y[rows, ln] = zy[c:2 * c] + x_yv[rows, ln]
            upd = _mm_tn(jnp.concatenate([u, x_v[rows, ln]], axis=0),
                         jnp.concatenate([x_bend[rows, ln], x_kend[rows, ln]], axis=0))
            s_new = s * x_gam[pl.ds(pl.multiple_of(ci * c, c), 1), ln] + jnp.where(bd, upd, 0.0)
            if chain:
                st[p] = s_new
            else:
                s_ref[ci, 2 * p] = s_new[0:hd, 0:hd]
                s_ref[ci, 2 * p + 1] = s_new[hd:2 * hd, hd:2 * hd]
        return carry

    lax.fori_loop(0, tt // c, chunk, 0)

    y = x_y[...]
    mean = _seg_sum(y, e2) * (1.0 / RWKV_HEAD)
    yc = y - mean
    var = _seg_sum(yc * yc, e2) * (1.0 / RWKV_HEAD)
    yn = yc * lax.rsqrt(var + RWKV_GN_EPS) * lnw_ref[...] + lnb_ref[...] + x_bon[...]
    y_ref[...] = (yn * jax.nn.silu(gate_ref[...])).astype(y_ref.dtype)

    if chain:
        @pl.when(j == nj - 1)
        def _():
            s_ref[0] = st[...]


def _rwkv(h, prev, s0, l0, acc, l, p, nseq, t, chain):
    m = nseq * t
    w = RWKV_WIDTH
    if chain:
        tt = min(t, 128)
        c = min(tt, 64)
        nj = t // tt
        grid = (nseq, nj)
        row = lambda b, j: b * nj + j
        prev = tuple(x[:, None, :] for x in prev)
        prow = lambda n: pl.BlockSpec((1, 1, n), lambda b, j: (b, 0, 0))
        st_spec = lambda li: pl.BlockSpec((None, 1, RWKV_PAIRS, LANE, LANE), lambda b, j: (li, b, 0, 0, 0))
    else:
        c = t
        tt = 64
        nj = m // tt
        grid = (1, nj)
        row = lambda b, j: j
        prev = tuple(jnp.repeat(x, t, axis=0) for x in prev)
        prow = lambda n: pl.BlockSpec((tt, n), lambda b, j: (j, 0))
        st_spec = lambda li: pl.BlockSpec((None, tt // c, RWKV_HEADS, RWKV_HEAD, RWKV_HEAD),
                                          lambda b, j: (li, j, 0, 0, 0))
    vec = lambda n: pl.BlockSpec((1, n), lambda b, j: (0, 0))
    tok = lambda col: pl.BlockSpec((tt, w), lambda b, j: (row(b, j), col // w))
    big = lambda: pltpu.VMEM((tt, w), f32)
    return pl.pallas_call(
        functools.partial(_rwkv_kernel, tt=tt, c=c, nj=nj, chain=chain),
        grid=grid,
        in_specs=[tok(COL_RW_R), tok(COL_RW_K), tok(COL_RW_V), tok(COL_RW_GATE),
                  pl.BlockSpec((tt, LANE), lambda b, j: (row(b, j), COL_RW_WA // LANE)),
                  prow(w), prow(w), prow(w), prow(LANE),
                  vec(w), vec(w), vec(w), vec(LANE),
                  vec(w), pl.BlockSpec((LANE, w), lambda b, j: (0, 0)), vec(w),
                  pl.BlockSpec((LANE, w), lambda b, j: (0, 0)), vec(w), vec(w), vec(w), vec(w), vec(w), st_spec(l0),
                  pl.BlockSpec(memory_space=pl.ANY)],
        out_specs=[pl.BlockSpec((tt, w), lambda b, j: (row(b, j), 0)), st_spec(l)],
        out_shape=[jax.ShapeDtypeStruct((m, w), _mix_dtype(t)), jax.ShapeDtypeStruct(acc.shape, f32)],
        input_output_aliases={23: 1},
        scratch_shapes=[pltpu.VMEM((RWKV_PAIRS, LANE, LANE), f32),
                        pltpu.VMEM((SUBLANE, w), f32), pltpu.VMEM((SUBLANE, w), f32),
                        pltpu.VMEM((SUBLANE, w), f32), pltpu.VMEM((SUBLANE, LANE), f32)] + [big() for _ in range(14)],
        compiler_params=_params("parallel" if chain else "arbitrary", "arbitrary"),
        name="rwkv_chain" if chain else "rwkv_tiles",
    )(h, h, h, h, h, *prev, p["mu_r"], p["mu_k"], p["mu_v"], p["mu_wa"], p["w0"], p["w2"], p["a0"], p["a2"],
      p["k_k"], p["k_a"], p["r_k"], p["ln_w"], p["ln_b"], s0, acc)


def _xattn_kernel(q_ref, k_ref, v_ref, o_ref):
    scale = XA_HEAD_DIM ** -0.5
    for hd in range(XA_HEADS):
        ln = slice(hd * XA_HEAD_DIM, (hd + 1) * XA_HEAD_DIM)
        s = _mm_nt(q_ref[:, ln] * scale, k_ref[0, :, ln])
        s = s - jnp.max(s, axis=-1, keepdims=True)
        e = jnp.exp(s)
        pr = e / jnp.sum(e, axis=-1, keepdims=True)
        o_ref[:, ln] = _mm(pr, v_ref[0, :, ln]).astype(o_ref.dtype)


def _xattn(h, mem_k, mem_v, nseq, t):
    m = nseq * t
    tq = min(t, 512)
    nj = t // tq
    mem_spec = pl.BlockSpec((1, N_MEM, XA_WIDTH), lambda b, j: (b, 0, 0))
    return pl.pallas_call(
        _xattn_kernel,
        grid=(nseq, nj),
        in_specs=[pl.BlockSpec((tq, XA_WIDTH), lambda b, j: (b * nj + j, COL_XA_Q // XA_WIDTH)), mem_spec, mem_spec],
        out_specs=pl.BlockSpec((tq, XA_WIDTH), lambda b, j: (b * nj + j, 0)),
        out_shape=jax.ShapeDtypeStruct((m, XA_WIDTH), _mix_dtype(t)),
        compiler_params=_params("parallel", "arbitrary"),
        name="xattn",
    )(h, mem_k, mem_v)


def _prep_w_in(w_in):
    o_rw = 5136
    seg = [(o_rw, o_rw + 1536), (o_rw + 1536, o_rw + 3072), (o_rw + 3072, o_rw + 4608), (9872, 11408),
           (0, 1024), (1024, 2048), (3072, 4096), (4096, 5120), (2048, 2560), (2560, 3072), (11408, 11920),
           (o_rw + 4608, o_rw + 4736), (5120, 5136)]
    parts = [w_in[:, :, a:b] for a, b in seg]
    used = sum(b - a for a, b in seg)
    parts.append(jnp.zeros(w_in.shape[:2] + (H_COLS - used,), w_in.dtype))
    return jnp.concatenate(parts, axis=2).astype(bf16)


def _block_diag_s5(b, c):
    gl = S5_GROUPS // S5_BLOCKS
    eye = jnp.eye(gl, dtype=b.dtype)
    bb = b.reshape(S5_BLOCKS, gl, S5_STATE, S5_GROUP)
    bbd = jnp.einsum("kgpn,gh->kgnhp", bb, eye).reshape(S5_BLOCKS, gl * S5_GROUP, gl * S5_STATE)
    cc = c.reshape(S5_BLOCKS, gl, S5_GROUP, S5_STATE)
    cbd = jnp.einsum("kgnp,gh->kgphn", cc, eye).reshape(S5_BLOCKS, gl * S5_STATE, gl * S5_GROUP)
    return bbd, cbd


def _layer_params(l, a):
    row = lambda x: x.reshape(1, -1)
    bre, cre = _block_diag_s5(a["s5_b_re"][l], a["s5_c_re"][l])
    bim, cim = _block_diag_s5(a["s5_b_im"][l], a["s5_c_im"][l])
    s5 = dict(lam_re=row(a["s5_lam_re"][l]), lam_im=row(a["s5_lam_im"][l]),
              log_step=row(jnp.broadcast_to(a["s5_log_step"][l][:, None], (S5_GROUPS, S5_STATE))),
              b_re=bre, b_im=bim, c_re=cre.astype(bf16), c_im=cim.astype(bf16),
              d=row(a["s5_d"][l]), w_glu=a["s5_w_glu"][l].astype(bf16), b_glu=row(a["s5_b_glu"][l]))
    wg = jnp.zeros((LANE, GLA_QK), f32).at[:GLA_GATE_RANK].set(a["gla_w_gate"][l])
    gla = dict(w_gate=wg, b_gate=row(a["gla_b_gate"][l]), norm_g=row(a["gla_norm_g"][l]))
    mu = a["rw_mu"][l]
    w = RWKV_WIDTH
    z64 = jnp.zeros((64, w), f32)
    rw = dict(mu_r=row(mu[0:w]), mu_k=row(mu[w:2 * w]), mu_v=row(mu[2 * w:3 * w]), mu_wa=row(mu[3 * w:]),
              w0=row(a["rw_w0"][l]), w2=jnp.concatenate([a["rw_w2"][l], z64], 0).astype(bf16),
              a0=row(a["rw_a0"][l]), a2=jnp.concatenate([z64, a["rw_a2"][l]], 0).astype(bf16),
              k_k=row(a["rw_k_k"][l]), k_a=row(a["rw_k_a"][l]), r_k=row(a["rw_r_k"][l]),
              ln_w=row(a["rw_ln_w"][l]), ln_b=row(a["rw_ln_b"][l]))
    return s5, gla, rw


def _unpack_rwkv_state(s):
    b = s.shape[0]
    s = s.reshape(b, RWKV_PAIRS, 2, RWKV_HEAD, 2, RWKV_HEAD)
    return jnp.stack([s[:, :, 0, :, 0, :], s[:, :, 1, :, 1, :]], axis=2).reshape(b, RWKV_HEADS, RWKV_HEAD, RWKV_HEAD)


def _split_shift(x):
    w = RWKV_WIDTH
    return tuple(x[:, a:b] for a, b in ((0, w), (w, 2 * w), (2 * w, 3 * w), (3 * w, 3 * w + LANE)))


def _mixer_layer(x, nseq, t, mem_k, mem_v, h0r, h0i, s_gla, s_rw, l0, acc_gla, acc_rw, l, prev, chain, lp,
                 w_in_l, w_out_parts, g_pre, g_post):
    s5p, glap, rwp = lp
    h = _norm_mm(x, g_pre, w_in_l)
    y_s5, hr, hi = _s5(h, h0r, h0i, s5p, nseq, t, chain)
    y_gla, s_gla_new = _gla(h, s_gla, l0, acc_gla, l, glap, nseq, t)
    y_rw, s_rw_new = _rwkv(h, prev, s_rw, l0, acc_rw, l, rwp, nseq, t, chain)
    y_xa = _xattn(h, mem_k, mem_v, nseq, t)
    x_new = _outproj((y_s5, y_gla, y_rw, y_xa), w_out_parts, g_post, x)
    last = h.reshape(nseq, t, H_COLS)[:, t - 1]
    w = RWKV_WIDTH
    shift = jnp.concatenate([last[:, COL_RW_R:COL_RW_R + w], last[:, COL_RW_K:COL_RW_K + w],
                             last[:, COL_RW_V:COL_RW_V + w], last[:, COL_RW_WA:COL_RW_WA + LANE]], axis=1)
    return x_new, jnp.stack([hr, hi], axis=-1).reshape(nseq, S5_GROUPS, S5_STATE, 2), s_gla_new, s_rw_new, shift


def kernel(x_prompt, x_sample, mem_prompt, cache_mem_k, cache_mem_v, state_s5, state_gla, state_rwkv, state_rwkv_shift, g_pre, g_post, w_in, w_out, s5_lam_re, s5_lam_im, s5_log_step, s5_b_re, s5_b_im, s5_c_re, s5_c_im, s5_d, s5_w_glu, s5_b_glu, gla_w_gate, gla_b_gate, gla_norm_g, rw_mu, rw_w0, rw_w2, rw_a0, rw_a2, rw_k_k, rw_k_a, rw_r_k, rw_ln_w, rw_ln_b, mem_g, w_mk, w_mv):
    raw = dict(s5_lam_re=s5_lam_re, s5_lam_im=s5_lam_im, s5_log_step=s5_log_step, s5_b_re=s5_b_re, s5_b_im=s5_b_im,
               s5_c_re=s5_c_re, s5_c_im=s5_c_im, s5_d=s5_d, s5_w_glu=s5_w_glu, s5_b_glu=s5_b_glu,
               gla_w_gate=gla_w_gate, gla_b_gate=gla_b_gate, gla_norm_g=gla_norm_g, rw_mu=rw_mu, rw_w0=rw_w0,
               rw_w2=rw_w2, rw_a0=rw_a0, rw_a2=rw_a2, rw_k_k=rw_k_k, rw_k_a=rw_k_a, rw_r_k=rw_r_k,
               rw_ln_w=rw_ln_w, rw_ln_b=rw_ln_b)
    bp, tp, d = x_prompt.shape
    bs, ts, _ = x_sample.shape
    depth = w_in.shape[0]
    w_in_p = _prep_w_in(w_in)
    w_out_b = w_out.astype(bf16)
    w_kv = jnp.concatenate([w_mk, w_mv], axis=2).astype(bf16)
    bounds = (0, S5_WIDTH, S5_WIDTH + GLA_WIDTH, S5_WIDTH + GLA_WIDTH + RWKV_WIDTH, D_MODEL)

    xp = x_prompt.reshape(bp * tp, d)
    xs = x_sample.reshape(bs * ts, d)
    mem2d = mem_prompt.reshape(bp * N_MEM, d)
    zp = lambda *s: jnp.zeros(s, f32)
    prev_p = _split_shift(zp(bp, RWKV_SHIFT_COLS))

    gla_p = zp(depth, bp, GLA_HEADS, GLA_DK, GLA_DV)
    gla_s = zp(depth, bs, GLA_HEADS, GLA_DK, GLA_DV)
    rw_p = zp(depth, bp, RWKV_PAIRS, LANE, LANE)
    rw_s = zp(depth, bs, RWKV_HEADS, RWKV_HEAD, RWKV_HEAD)
    zero_gla = zp(1, bp, GLA_HEADS, GLA_DK, GLA_DV)
    zero_rw = zp(1, bp, RWKV_PAIRS, LANE, LANE)

    outs = [[] for _ in range(6)]
    for l in range(depth):
        lp = _layer_params(l, raw)
        w_out_parts = tuple(w_out_b[l, bounds[i]:bounds[i + 1]] for i in range(4))
        gp, gq = g_pre[l].reshape(1, d), g_post[l].reshape(1, d)
        kv = _norm_mm(mem2d, mem_g[l].reshape(1, d), w_kv[l])
        k_mem = kv[:, :XA_WIDTH].reshape(bp, N_MEM, XA_WIDTH)
        v_mem = kv[:, XA_WIDTH:].reshape(bp, N_MEM, XA_WIDTH)
        xp, h1, gla_p, rw_p, p1 = _mixer_layer(
            xp, bp, tp, k_mem, v_mem, zp(bp, S5_NS), zp(bp, S5_NS), zero_gla, zero_rw, 0, gla_p, rw_p, l, prev_p,
            True, lp, w_in_p[l], w_out_parts, gp, gq)
        st5 = state_s5[l].reshape(bs, S5_NS, 2)
        xs, h2, gla_s, rw_s, p2 = _mixer_layer(
            xs, bs, ts, cache_mem_k[l].reshape(bs, N_MEM, XA_WIDTH), cache_mem_v[l].reshape(bs, N_MEM, XA_WIDTH),
            st5[..., 0], st5[..., 1], state_gla, state_rwkv, l, gla_s, rw_s, l, _split_shift(state_rwkv_shift[l]),
            False, lp, w_in_p[l], w_out_parts, gp, gq)
        vals = (k_mem.reshape(bp, N_MEM, XA_HEADS, XA_HEAD_DIM), v_mem.reshape(bp, N_MEM, XA_HEADS, XA_HEAD_DIM),
                h1, h2, p1, p2)
        for o, val in zip(outs, vals):
            o.append(val)

    mk, mv, s5_p, s5_s, sh_p, sh_s = (jnp.stack(o) for o in outs)
    rw_p_heads = _unpack_rwkv_state(rw_p.reshape(depth * bp, RWKV_PAIRS, LANE, LANE))
    rw_p_heads = rw_p_heads.reshape(depth, bp, RWKV_HEADS, RWKV_HEAD, RWKV_HEAD)
    return (xp.reshape(bp, tp, d), xs.reshape(bs, ts, d), mk, mv, s5_p, s5_s, gla_p, gla_s, rw_p_heads, rw_s,
            sh_p, sh_s)
```

```python
import functools

import jax
import jax.numpy as jnp
from jax import lax
from jax.experimental import pallas as pl
from jax.experimental.pallas import tpu as pltpu

f32 = jnp.float32
bf16 = jnp.bfloat16
HI = lax.Precision.HIGHEST

D_MODEL = 4096
DEPTH = 4
N_MEM = 256
NORM_EPS = 1e-6

S5_WIDTH = 1024
S5_GROUP = 16
S5_GROUPS = 64
S5_STATE = 64
S5_NS = S5_GROUPS * S5_STATE
S5_BLOCKS = 8
S5_LANES = 512

GLA_WIDTH = 1024
GLA_HEADS = 4
GLA_DV = 256
GLA_DK = 128
GLA_QK = 512
GLA_GATE_RANK = 16
GLA_GATE_TAU = 16.0

RWKV_WIDTH = 1536
RWKV_HEAD = 64
RWKV_HEADS = 24
RWKV_PAIRS = 12
RWKV_GROUP = 4
RWKV_GN_EPS = RWKV_HEAD * 1e-5
RWKV_SHIFT_COLS = 3 * RWKV_WIDTH + 128

XA_HEADS = 4
XA_HEAD_DIM = 128
XA_WIDTH = 512

LANE = 128
SUBLANE = 8
VMEM_LIMIT = 56 * 1024 * 1024

H_COLS = 12288
COL_RW_R, COL_RW_K, COL_RW_V, COL_RW_GATE = 0, 1536, 3072, 4608
COL_S5_U, COL_S5_GATE, COL_G_V, COL_G_GATE = 6144, 7168, 8192, 9216
COL_G_Q, COL_G_K, COL_XA_Q = 10240, 10752, 11264
COL_RW_WA, COL_G_LR = 11776, 11904


def _mm(a, b):
    return jnp.dot(a.astype(bf16), b.astype(bf16), preferred_element_type=f32)


def _mm_nt(a, b):
    return lax.dot_general(a.astype(bf16), b.astype(bf16), (((1,), (1,)), ((), ())), preferred_element_type=f32)


def _mm_tn(a, b):
    return lax.dot_general(a.astype(bf16), b.astype(bf16), (((0,), (0,)), ((), ())), preferred_element_type=f32)


def _mm_hi(a, b):
    return jnp.dot(a, b, precision=HI, preferred_element_type=f32)


def _split2(x):
    hi = x.astype(bf16)
    return hi, (x - hi.astype(f32)).astype(bf16)


def _mm_mask_l(mask01, x):
    hi, lo = _split2(x)
    m = mask01.astype(bf16)
    return jnp.dot(m, hi, preferred_element_type=f32) + jnp.dot(m, lo, preferred_element_type=f32)


def _mm_mask_r(x, mask01):
    hi, lo = _split2(x)
    m = mask01.astype(bf16)
    return jnp.dot(hi, m, preferred_element_type=f32) + jnp.dot(lo, m, preferred_element_type=f32)


def _mix_dtype(t):
    return bf16 if t % (2 * SUBLANE) == 0 else f32


def _params(*sem):
    return pltpu.CompilerParams(dimension_semantics=sem, vmem_limit_bytes=VMEM_LIMIT)


def _chunk_tril(n, c, strict=False):
    r = lax.broadcasted_iota(jnp.int32, (n, n), 0)
    q = lax.broadcasted_iota(jnp.int32, (n, n), 1)
    same = (r // c) == (q // c)
    return same & ((q < r) if strict else (q <= r))


def _norm_mm_kernel(x_ref, g_ref, w_ref, o_ref, xn_ref):
    @pl.when(pl.program_id(1) == 0)
    def _():
        x = x_ref[...]
        ms = jnp.mean(x * x, axis=-1, keepdims=True)
        xn_ref[...] = (x * lax.rsqrt(ms + NORM_EPS) * g_ref[...]).astype(bf16)

    o_ref[...] = jnp.dot(xn_ref[...], w_ref[...], preferred_element_type=f32)


def _norm_mm(x, g, w, tm=512, tn=512):
    m, d = x.shape
    n = w.shape[1]
    return pl.pallas_call(
        _norm_mm_kernel,
        grid=(m // tm, n // tn),
        in_specs=[pl.BlockSpec((tm, d), lambda i, j: (i, 0)),
                  pl.BlockSpec((1, d), lambda i, j: (0, 0)),
                  pl.BlockSpec((d, tn), lambda i, j: (0, j))],
        out_specs=pl.BlockSpec((tm, tn), lambda i, j: (i, j)),
        out_shape=jax.ShapeDtypeStruct((m, n), f32),
        scratch_shapes=[pltpu.VMEM((tm, d), bf16)],
        compiler_params=_params("parallel", "arbitrary"),
        name="norm_mm",
    )(x, g, w)


def _outproj_kernel(a1, a2, a3, a4, w1, w2, w3, w4, g_ref, x_ref, o_ref, acc_ref, *, nj, tn):
    j = pl.program_id(1)
    acc = _mm(a1[...], w1[...]) + _mm(a2[...], w2[...]) + _mm(a3[...], w3[...]) + _mm(a4[...], w4[...])
    acc_ref[j] = acc

    @pl.when(j == nj - 1)
    def _():
        ssq = jnp.sum(acc_ref[0] * acc_ref[0], axis=-1, keepdims=True)
        for k in range(1, nj):
            ssq += jnp.sum(acc_ref[k] * acc_ref[k], axis=-1, keepdims=True)
        inv = lax.rsqrt(ssq * (1.0 / D_MODEL) + NORM_EPS)
        for k in range(nj):
            sl = slice(k * tn, (k + 1) * tn)
            o_ref[:, sl] = x_ref[:, sl] + acc_ref[k] * inv * g_ref[:, sl]


def _outproj(parts, ws, g, x, tn=512):
    m, d = x.shape
    nj = d // tn
    tm = 512 if parts[0].dtype == bf16 else 256
    in_specs = [pl.BlockSpec((tm, p.shape[1]), lambda i, j: (i, 0)) for p in parts]
    in_specs += [pl.BlockSpec((w.shape[0], tn), lambda i, j: (0, j)) for w in ws]
    in_specs += [pl.BlockSpec((1, d), lambda i, j: (0, 0)),
                 pl.BlockSpec((tm, d), lambda i, j: (i, 0), pipeline_mode=pl.Buffered(1))]
    return pl.pallas_call(
        functools.partial(_outproj_kernel, nj=nj, tn=tn),
        grid=(m // tm, nj),
        in_specs=in_specs,
        out_specs=pl.BlockSpec((tm, d), lambda i, j: (i, 0)),
        out_shape=jax.ShapeDtypeStruct((m, d), f32),
        scratch_shapes=[pltpu.VMEM((nj, tm, tn), f32)],
        compiler_params=_params("parallel", "arbitrary"),
        name="outproj",
    )(*parts, *ws, g, x)


def _s5_kernel(u_ref, gate_ref, h0r_ref, h0i_ref, lre_ref, lim_ref, step_ref, bre_ref, bim_ref, cre_ref, cim_ref,
               d_ref, wglu_ref, bglu_ref, y_ref, hr_ref, hi_ref, sre, sim, car, cai, *, tt, chain):
    j = pl.program_id(1)
    nrt = tt // SUBLANE
    lr = lre_ref[...]
    li = lim_ref[...]
    st = jnp.exp(step_ref[...])
    kidx = (lax.broadcasted_iota(jnp.int32, (SUBLANE, S5_NS), 0) + 1).astype(f32)
    mag = jnp.exp(kidx * (lr * st))
    ang = kidx * (li * st)
    pw_re = mag * jnp.cos(ang)
    pw_im = mag * jnp.sin(ang)
    ab_re = pw_re[0:1]
    ab_im = pw_im[0:1]
    den = lr * lr + li * li
    f_re = ((ab_re - 1.0) * lr + ab_im * li) / den
    f_im = (ab_im * lr - (ab_re - 1.0) * li) / den

    u = u_ref[...]
    for blk in range(S5_BLOCKS):
        cols = slice(blk * S5_LANES, (blk + 1) * S5_LANES)
        fr = f_re[:, cols]
        fi = f_im[:, cols]
        b_re = bre_ref[blk]
        b_im = bim_ref[blk]
        ub = u[:, blk * LANE:(blk + 1) * LANE].astype(bf16)
        sre[:, cols] = _mm(ub, fr * b_re - fi * b_im)
        sim[:, cols] = _mm(ub, fr * b_im + fi * b_re)

    if chain:
        @pl.when(j == 0)
        def _():
            car[...] = jnp.broadcast_to(h0r_ref[0], (SUBLANE, S5_NS))
            cai[...] = jnp.broadcast_to(h0i_ref[0], (SUBLANE, S5_NS))

    rowi = lax.broadcasted_iota(jnp.int32, (SUBLANE, S5_LANES), 0)
    for ch in range(S5_NS // S5_LANES):
        cols = slice(ch * S5_LANES, (ch + 1) * S5_LANES)
        p_re = pw_re[:, cols]
        p_im = pw_im[:, cols]
        steps = [(d, p_re[d - 1:d], p_im[d - 1:d]) for d in (1, 2, 4)]

        def body(r, carry, cols=cols, p_re=p_re, p_im=p_im, steps=steps):
            rows = pl.ds(pl.multiple_of(r * SUBLANE, SUBLANE), SUBLANE)
            xr = sre[rows, cols]
            xi = sim[rows, cols]
            for d, ar, ai in steps:
                sr = jnp.where(rowi >= d, pltpu.roll(xr, d, 0), 0.0)
                si = jnp.where(rowi >= d, pltpu.roll(xi, d, 0), 0.0)
                xr, xi = xr + ar * sr - ai * si, xi + ar * si + ai * sr
            if chain:
                cr, ci = carry
            else:
                cr = jnp.broadcast_to(h0r_ref[pl.ds(r, 1), cols], (SUBLANE, S5_LANES))
                ci = jnp.broadcast_to(h0i_ref[pl.ds(r, 1), cols], (SUBLANE, S5_LANES))
            xr, xi = xr + p_re * cr - p_im * ci, xi + p_re * ci + p_im * cr
            sre[rows, cols] = xr
            sim[rows, cols] = xi
            if chain:
                return (jnp.broadcast_to(xr[SUBLANE - 1:SUBLANE], (SUBLANE, S5_LANES)),
                        jnp.broadcast_to(xi[SUBLANE - 1:SUBLANE], (SUBLANE, S5_LANES)))
            hr_ref[pl.ds(r, 1), cols] = xr[SUBLANE - 1:SUBLANE]
            hi_ref[pl.ds(r, 1), cols] = xi[SUBLANE - 1:SUBLANE]
            return carry

        if chain:
            cr, ci = lax.fori_loop(0, nrt, body, (car[:, cols], cai[:, cols]))
            car[:, cols] = cr
            cai[:, cols] = ci
        else:
            lax.fori_loop(0, nrt, body, 0)

    if chain:
        hr_ref[0] = car[0:1, :]
        hi_ref[0] = cai[0:1, :]

    ys = []
    for blk in range(S5_BLOCKS):
        cols = slice(blk * S5_LANES, (blk + 1) * S5_LANES)
        ys.append(_mm(sre[:, cols], cre_ref[blk]) - _mm(sim[:, cols], cim_ref[blk]))
    y = jnp.concatenate(ys, axis=1) + d_ref[...] * u
    z = jax.nn.gelu(y)
    gl = jax.nn.sigmoid(_mm(z, wglu_ref[...]) + bglu_ref[...])
    y_ref[...] = (z * gl * jax.nn.silu(gate_ref[...])).astype(y_ref.dtype)


def _s5(h, h0r, h0i, p, nseq, t, chain):
    m = nseq * t
    if chain:
        tt = min(t, 256)
        nj = t // tt
        grid = (nseq, nj)
        row = lambda b, j: b * nj + j
        h0r = h0r.reshape(nseq, 1, S5_NS)
        h0i = h0i.reshape(nseq, 1, S5_NS)
        h_spec = pl.BlockSpec((1, 1, S5_NS), lambda b, j: (b, 0, 0))
        h_shape = jax.ShapeDtypeStruct((nseq, 1, S5_NS), f32)
    else:
        assert t == SUBLANE
        tt = 64
        nj = m // tt
        grid = (1, nj)
        row = lambda b, j: j
        h_spec = pl.BlockSpec((tt // SUBLANE, S5_NS), lambda b, j: (j, 0))
        h_shape = jax.ShapeDtypeStruct((nseq, S5_NS), f32)
    cu, cg = COL_S5_U // S5_WIDTH, COL_S5_GATE // S5_WIDTH
    vec = lambda n: pl.BlockSpec((1, n), lambda b, j: (0, 0))
    full3 = lambda s: pl.BlockSpec(s, lambda b, j: (0, 0, 0))
    y, hr, hi = pl.pallas_call(
        functools.partial(_s5_kernel, tt=tt, chain=chain),
        grid=grid,
        in_specs=[pl.BlockSpec((tt, S5_WIDTH), lambda b, j: (row(b, j), cu)),
                  pl.BlockSpec((tt, S5_WIDTH), lambda b, j: (row(b, j), cg)),
                  h_spec, h_spec, vec(S5_NS), vec(S5_NS), vec(S5_NS),
                  full3((S5_BLOCKS, LANE, S5_LANES)), full3((S5_BLOCKS, LANE, S5_LANES)),
                  full3((S5_BLOCKS, S5_LANES, LANE)), full3((S5_BLOCKS, S5_LANES, LANE)),
                  vec(S5_WIDTH), pl.BlockSpec((S5_WIDTH, S5_WIDTH), lambda b, j: (0, 0)), vec(S5_WIDTH)],
        out_specs=[pl.BlockSpec((tt, S5_WIDTH), lambda b, j: (row(b, j), 0)), h_spec, h_spec],
        out_shape=[jax.ShapeDtypeStruct((m, S5_WIDTH), _mix_dtype(t)), h_shape, h_shape],
        scratch_shapes=[pltpu.VMEM((tt, S5_NS), f32), pltpu.VMEM((tt, S5_NS), f32),
                        pltpu.VMEM((SUBLANE, S5_NS), f32), pltpu.VMEM((SUBLANE, S5_NS), f32)],
        compiler_params=_params("parallel" if chain else "arbitrary", "arbitrary"),
        name="s5_chain" if chain else "s5_tiles",
    )(h, h, h0r, h0i, p["lam_re"], p["lam_im"], p["log_step"], p["b_re"], p["b_im"], p["c_re"], p["c_im"],
      p["d"], p["w_glu"], p["b_glu"])
    return y, hr.reshape(nseq, S5_NS), hi.reshape(nseq, S5_NS)


def _gla_kernel(q_ref, k_ref, v_ref, gate_ref, glr_ref, wg_ref, bg_ref, ng_ref, s0_ref, acc_ref, y_ref, s_ref,
                st_ref, b_ref, *, tt, c, nj):
    del acc_ref
    j = pl.program_id(1)

    @pl.when(j == 0)
    def _():
        for hd in range(GLA_HEADS):
            st_ref[hd] = s0_ref[0, hd].T

    la = jax.nn.log_sigmoid(_mm_hi(glr_ref[...], wg_ref[...]) + bg_ref[...]) * (1.0 / GLA_GATE_TAU)
    tri = jnp.where(_chunk_tril(tt, c), 1.0, 0.0).astype(f32)
    b_ref[...] = _mm_hi(tri, la)

    rowi = lax.broadcasted_iota(jnp.int32, (c, GLA_DK), 0)
    scale = GLA_DK ** -0.5

    def chunk(ci, carry):
        rows = pl.ds(pl.multiple_of(ci * c, c), c)
        for hd in range(GLA_HEADS):
            kl = slice(hd * GLA_DK, (hd + 1) * GLA_DK)
            vl = slice(hd * GLA_DV, (hd + 1) * GLA_DV)
            qh = q_ref[rows, kl] * scale
            kh = k_ref[rows, kl]
            vh = v_ref[rows, vl]
            bh = b_ref[rows, kl]
            o = jnp.zeros((c, GLA_DV), f32)
            for jj in range(c):
                dec = jnp.exp(jnp.where(rowi >= jj, bh - bh[jj:jj + 1], -jnp.inf))
                col = jnp.sum(qh * kh[jj:jj + 1] * dec, axis=-1, keepdims=True)
                o = o + col * vh[jj:jj + 1]
            s_t = st_ref[hd]
            o = o + _mm_nt(qh * jnp.exp(bh), s_t)
            b_last = bh[c - 1:c]
            st_ref[hd] = s_t * jnp.exp(b_last) + _mm_tn(vh, kh * jnp.exp(b_last - bh))
            on = o * lax.rsqrt(jnp.mean(o * o, axis=-1, keepdims=True) + NORM_EPS) * ng_ref[:, vl]
            y_ref[rows, vl] = (on * jax.nn.silu(gate_ref[rows, vl])).astype(y_ref.dtype)
        return carry

    lax.fori_loop(0, tt // c, chunk, 0)

    @pl.when(j == nj - 1)
    def _():
        for hd in range(GLA_HEADS):
            s_ref[0, hd] = st_ref[hd].T


def _stacked(acc, shape):
    if acc is None:
        return jnp.zeros((SUBLANE, LANE), f32), None
    assert acc.shape == shape
    return acc, 1


def _gla(h, s0, l0, acc, l, depth, p, nseq, t):
    m = nseq * t
    acc_shape = (depth, nseq, GLA_HEADS, GLA_DK, GLA_DV)
    acc, alias_to = _stacked(acc, acc_shape)
    c = min(t, 16)
    tt = min(t, 128)
    nj = t // tt
    row = lambda b, j: b * nj + j
    vec = lambda n: pl.BlockSpec((1, n), lambda b, j: (0, 0))
    st_spec = lambda li: pl.BlockSpec((None, 1, GLA_HEADS, GLA_DK, GLA_DV), lambda b, j: (li, b, 0, 0, 0))
    return pl.pallas_call(
        functools.partial(_gla_kernel, tt=tt, c=c, nj=nj),
        grid=(nseq, nj),
        in_specs=[pl.BlockSpec((tt, GLA_QK), lambda b, j: (row(b, j), COL_G_Q // GLA_QK)),
                  pl.BlockSpec((tt, GLA_QK), lambda b, j: (row(b, j), COL_G_K // GLA_QK)),
                  pl.BlockSpec((tt, GLA_WIDTH), lambda b, j: (row(b, j), COL_G_V // GLA_WIDTH)),
                  pl.BlockSpec((tt, GLA_WIDTH), lambda b, j: (row(b, j), COL_G_GATE // GLA_WIDTH)),
                  pl.BlockSpec((tt, LANE), lambda b, j: (row(b, j), COL_G_LR // LANE)),
                  pl.BlockSpec((LANE, GLA_QK), lambda b, j: (0, 0)), vec(GLA_QK), vec(GLA_WIDTH), st_spec(l0),
                  pl.BlockSpec(memory_space=pl.ANY)],
        out_specs=[pl.BlockSpec((tt, GLA_WIDTH), lambda b, j: (row(b, j), 0)), st_spec(l)],
        out_shape=[jax.ShapeDtypeStruct((m, GLA_WIDTH), _mix_dtype(t)), jax.ShapeDtypeStruct(acc_shape, f32)],
        scratch_shapes=[pltpu.VMEM((GLA_HEADS, GLA_DV, GLA_DK), f32), pltpu.VMEM((tt, GLA_QK), f32)],
        input_output_aliases={} if alias_to is None else {9: alias_to},
        compiler_params=_params("parallel", "arbitrary"),
        name="gla",
    )(h, h, h, h, h, p["w_gate"], p["b_gate"], p["norm_g"], s0, acc)


def _seg_sum(x, e2):
    return jnp.concatenate([_mm_mask_r(x[:, i * LANE:(i + 1) * LANE], e2) for i in range(RWKV_PAIRS)], axis=1)


def _rwkv_kernel(r_ref, k_ref, v_ref, gate_ref, wa_ref, pr_ref, pk_ref, pv_ref, pwa_ref,
                 mur_ref, muk_ref, muv_ref, muwa_ref, w0_ref, w2_ref, a0_ref, a2_ref, kk_ref, ka_ref, rk_ref,
                 lnw_ref, lnb_ref, s0_ref, acc_ref, y_ref, s_ref,
                 st, c_r, c_k, c_v, c_wa, x_kkg, x_rg, x_bi, x_ki, x_bend, x_kend, x_gam, x_v, x_bon,
                 x_tk, x_rq, x_uv, x_yv, x_y, *, tt, c, nj, chain):
    del acc_ref
    j = pl.program_id(1)
    hd = RWKV_HEAD

    if chain:
        @pl.when(j == 0)
        def _():
            st[...] = s0_ref[0]
            c_r[...] = jnp.broadcast_to(pr_ref[0], c_r.shape)
            c_k[...] = jnp.broadcast_to(pk_ref[0], c_k.shape)
            c_v[...] = jnp.broadcast_to(pv_ref[0], c_v.shape)
            c_wa[...] = jnp.broadcast_to(pwa_ref[0], c_wa.shape)
    else:
        @pl.when(j == 0)
        def _():
            st[...] = jnp.zeros(st.shape, f32)

    def shift(x_ref, c_ref, p_ref, mu_ref):
        x = x_ref[...]
        rowid = lax.broadcasted_iota(jnp.int32, x.shape, 0)
        rolled = pltpu.roll(x, 1, 0)
        if chain:
            prev = jnp.where(rowid == 0, c_ref[0:1, :], rolled)
            c_ref[...] = jnp.broadcast_to(x[tt - 1:tt, :], c_ref.shape)
        else:
            prev = jnp.where(rowid % c == 0, p_ref[...], rolled)
        return x + (prev - x) * mu_ref[...]

    r = shift(r_ref, c_r, pr_ref, mur_ref)
    k = shift(k_ref, c_k, pk_ref, muk_ref)
    v = shift(v_ref, c_v, pv_ref, muv_ref)
    wa = shift(wa_ref, c_wa, pwa_ref, muwa_ref)

    w_log = -jax.nn.softplus(-(w0_ref[...] + _mm(jnp.tanh(wa), w2_ref[...]))) - 0.5
    lw = -jnp.exp(w_log)
    a = jax.nn.sigmoid(a0_ref[...] + _mm(wa, a2_ref[...]))

    lane = lax.broadcasted_iota(jnp.int32, (LANE, LANE), 1)
    sub = lax.broadcasted_iota(jnp.int32, (LANE, LANE), 0)
    bd = (lane // RWKV_HEAD) == (sub // RWKV_HEAD)
    e2 = jnp.where(bd, 1.0, 0.0).astype(f32)

    kk = k * kk_ref[...]
    kk = kk / jnp.maximum(jnp.sqrt(_seg_sum(kk * kk, e2)), 1e-12)
    k2 = k * (1.0 + (a - 1.0) * ka_ref[...])
    bv = kk * a
    ri = lax.broadcasted_iota(jnp.int32, (tt, tt), 0)
    qi = lax.broadcasted_iota(jnp.int32, (tt, tt), 1)
    same = (ri // c) == (qi // c)
    strict = same & (qi < ri)
    incl = same & (qi <= ri)
    eye = jnp.where(qi == ri, 1.0, 0.0).astype(f32)
    cl = _mm_mask_l(jnp.where(incl, 1.0, 0.0), lw)
    cll = _mm_mask_l(jnp.where(same, 1.0, 0.0), lw)
    ginv = jnp.exp(-cl)
    gend = jnp.exp(cll - cl)
    x_kkg[...] = kk * jnp.exp(cl - lw)
    x_rg[...] = r * jnp.exp(cl)
    x_bi[...] = bv * ginv
    x_ki[...] = k2 * ginv
    x_bend[...] = bv * gend
    x_kend[...] = k2 * gend
    x_gam[...] = jnp.exp(cll)
    x_v[...] = v
    x_bon[...] = _seg_sum(r * k2 * rk_ref[...], e2) * v

    head_a = lax.broadcasted_iota(jnp.int32, (tt, LANE), 1) < RWKV_HEAD

    def sel(xa, xb):
        return jnp.where(head_a, xa, xb)

    def tri_inv(ams):
        ns = [-am for am in ams]
        ts = [eye + n for n in ns]
        if c <= 2:
            return ts
        ns = [_mm(n, n) for n in ns]
        width = 4
        while width < c:
            both = [_mm(jnp.concatenate([t, n], axis=0), n) for t, n in zip(ts, ns)]
            ts = [t + b[0:tt] for t, b in zip(ts, both)]
            ns = [b[tt:2 * tt] for b in both]
            width *= 2
        last = [_mm(t, n) for t, n in zip(ts, ns)]
        return [t + m for t, m in zip(ts, last)]

    def lanes(p):
        return slice(p * LANE, (p + 1) * LANE)

    for g0 in range(0, RWKV_PAIRS, RWKV_GROUP):
        grp = list(range(g0, g0 + RWKV_GROUP))
        kkg = [x_kkg[:, lanes(p)] for p in grp]
        rg = [x_rg[:, lanes(p)] for p in grp]
        vp = [x_v[:, lanes(p)] for p in grp]
        lhs = [jnp.concatenate([jnp.where(head_a, a, 0.0), jnp.where(head_a, 0.0, a),
                                jnp.where(head_a, b, 0.0), jnp.where(head_a, 0.0, b)], axis=0)
               for a, b in zip(kkg, rg)]
        if tt % LANE == 0:
            pbk = [_mm_nt(x, jnp.concatenate([x_bi[:, lanes(p)], x_ki[:, lanes(p)]], axis=0))
                   for x, p in zip(lhs, grp)]
            pb = [x[:, 0:tt] for x in pbk]
            pk = [x[:, tt:2 * tt] for x in pbk]
        else:
            pb = [_mm_nt(x, x_bi[:, lanes(p)]) for x, p in zip(lhs, grp)]
            pk = [_mm_nt(x, x_ki[:, lanes(p)]) for x, p in zip(lhs, grp)]
        tinv = tri_inv([jnp.where(strict, x[hh * tt:(hh + 1) * tt], 0.0) for x in pb for hh in (0, 1)])
        ar = [jnp.where(incl, x[hh * tt:(hh + 1) * tt], 0.0) for x in pb for hh in (2, 3)]
        bmr = [jnp.concatenate([jnp.where(strict, x[hh * tt:(hh + 1) * tt], 0.0),
                                jnp.where(incl, x[(hh + 2) * tt:(hh + 3) * tt], 0.0)], axis=0)
               for x in pk for hh in (0, 1)]
        mv = [_mm(x, vp[i // 2]) for i, x in enumerate(bmr)]
        bmv = [sel(mv[2 * i][0:tt], mv[2 * i + 1][0:tt]) for i in range(RWKV_GROUP)]
        brv = [sel(mv[2 * i][tt:2 * tt], mv[2 * i + 1][tt:2 * tt]) for i in range(RWKV_GROUP)]
        rhs_t = [jnp.concatenate([a, b], axis=1) for a, b in zip(bmv, kkg)]
        mt = [_mm(x, rhs_t[i // 2]) for i, x in enumerate(tinv)]
        uv = [-sel(mt[2 * i][:, 0:LANE], mt[2 * i + 1][:, 0:LANE]) for i in range(RWKV_GROUP)]
        tk = [sel(mt[2 * i][:, LANE:2 * LANE], mt[2 * i + 1][:, LANE:2 * LANE]) for i in range(RWKV_GROUP)]
        rhs_r = [jnp.concatenate([a, b], axis=1) for a, b in zip(uv, tk)]
        mr = [_mm(x, rhs_r[i // 2]) for i, x in enumerate(ar)]
        for i, p in enumerate(grp):
            x_uv[:, lanes(p)] = uv[i]
            x_tk[:, lanes(p)] = tk[i]
            x_yv[:, lanes(p)] = sel(mr[2 * i][:, 0:LANE], mr[2 * i + 1][:, 0:LANE]) + brv[i]
            x_rq[:, lanes(p)] = rg[i] - sel(mr[2 * i][:, LANE:2 * LANE], mr[2 * i + 1][:, LANE:2 * LANE])

    def chunk(ci, carry):
        row0 = pl.multiple_of(ci * c, c)
        rows = pl.ds(row0, c)
        pairs = range(RWKV_PAIRS)
        if not chain:
            for p in pairs:
                st[p, 0:hd, 0:hd] = s0_ref[ci, 2 * p]
                st[p, hd:2 * hd, hd:2 * hd] = s0_ref[ci, 2 * p + 1]
        s = [st[p] for p in pairs]
        zy = [_mm_nt(jnp.concatenate([x_tk[rows, lanes(p)], x_rq[rows, lanes(p)]], axis=0), s[p]) for p in pairs]
        u = [x_uv[rows, lanes(p)] - zy[p][0:c] for p in pairs]
        for p in pairs:
            x_y[rows, lanes(p)] = zy[p][c:2 * c] + x_yv[rows, lanes(p)]
        upd = [_mm_tn(jnp.concatenate([u[p], x_v[rows, lanes(p)]], axis=0),
                      jnp.concatenate([x_bend[rows, lanes(p)], x_kend[rows, lanes(p)]], axis=0)) for p in pairs]
        for p in pairs:
            s_new = s[p] * x_gam[pl.ds(row0, 1), lanes(p)] + jnp.where(bd, upd[p], 0.0)
            if chain:
                st[p] = s_new
            else:
                s_ref[ci, 2 * p] = s_new[0:hd, 0:hd]
                s_ref[ci, 2 * p + 1] = s_new[hd:2 * hd, hd:2 * hd]
        return carry

    lax.fori_loop(0, tt // c, chunk, 0)

    y = x_y[...]
    mean = _seg_sum(y, e2) * (1.0 / RWKV_HEAD)
    yc = y - mean
    var = _seg_sum(yc * yc, e2) * (1.0 / RWKV_HEAD)
    yn = yc * lax.rsqrt(var + RWKV_GN_EPS) * lnw_ref[...] + lnb_ref[...] + x_bon[...]
    y_ref[...] = (yn * jax.nn.silu(gate_ref[...])).astype(y_ref.dtype)

    if chain:
        @pl.when(j == nj - 1)
        def _():
            s_ref[0] = st[...]


def _rwkv(h, prev, s0, l0, acc, l, depth, p, nseq, t, chain):
    m = nseq * t
    w = RWKV_WIDTH
    acc_shape = (depth, nseq) + ((RWKV_PAIRS, LANE, LANE) if chain else (RWKV_HEADS, RWKV_HEAD, RWKV_HEAD))
    acc, alias_to = _stacked(acc, acc_shape)
    if chain:
        tt = min(t, 128)
        c = min(tt, 64)
        nj = t // tt
        grid = (nseq, nj)
        row = lambda b, j: b * nj + j
        prev = tuple(x[:, None, :] for x in prev)
        prow = lambda n: pl.BlockSpec((1, 1, n), lambda b, j: (b, 0, 0))
        st_spec = lambda li: pl.BlockSpec((None, 1, RWKV_PAIRS, LANE, LANE), lambda b, j: (li, b, 0, 0, 0))
    else:
        c = t
        tt = 64
        nj = m // tt
        grid = (1, nj)
        row = lambda b, j: j
        prev = tuple(jnp.repeat(x, t, axis=0) for x in prev)
        prow = lambda n: pl.BlockSpec((tt, n), lambda b, j: (j, 0))
        st_spec = lambda li: pl.BlockSpec((None, tt // c, RWKV_HEADS, RWKV_HEAD, RWKV_HEAD),
                                          lambda b, j: (li, j, 0, 0, 0))
    vec = lambda n: pl.BlockSpec((1, n), lambda b, j: (0, 0))
    tok = lambda col: pl.BlockSpec((tt, w), lambda b, j: (row(b, j), col // w))
    big = lambda: pltpu.VMEM((tt, w), f32)
    return pl.pallas_call(
        functools.partial(_rwkv_kernel, tt=tt, c=c, nj=nj, chain=chain),
        grid=grid,
        in_specs=[tok(COL_RW_R), tok(COL_RW_K), tok(COL_RW_V), tok(COL_RW_GATE),
                  pl.BlockSpec((tt, LANE), lambda b, j: (row(b, j), COL_RW_WA // LANE)),
                  prow(w), prow(w), prow(w), prow(LANE),
                  vec(w), vec(w), vec(w), vec(LANE),
                  vec(w), pl.BlockSpec((LANE, w), lambda b, j: (0, 0)), vec(w),
                  pl.BlockSpec((LANE, w), lambda b, j: (0, 0)), vec(w), vec(w), vec(w), vec(w), vec(w), st_spec(l0),
                  pl.BlockSpec(memory_space=pl.ANY)],
        out_specs=[pl.BlockSpec((tt, w), lambda b, j: (row(b, j), 0)), st_spec(l)],
        out_shape=[jax.ShapeDtypeStruct((m, w), _mix_dtype(t)), jax.ShapeDtypeStruct(acc_shape, f32)],
        input_output_aliases={} if alias_to is None else {23: alias_to},
        scratch_shapes=[pltpu.VMEM((RWKV_PAIRS, LANE, LANE), f32),
                        pltpu.VMEM((SUBLANE, w), f32), pltpu.VMEM((SUBLANE, w), f32),
                        pltpu.VMEM((SUBLANE, w), f32), pltpu.VMEM((SUBLANE, LANE), f32)] + [big() for _ in range(14)],
        compiler_params=_params("parallel" if chain else "arbitrary", "arbitrary"),
        name="rwkv_chain" if chain else "rwkv_tiles",
    )(h, h, h, h, h, *prev, p["mu_r"], p["mu_k"], p["mu_v"], p["mu_wa"], p["w0"], p["w2"], p["a0"], p["a2"],
      p["k_k"], p["k_a"], p["r_k"], p["ln_w"], p["ln_b"], s0, acc)


def _xattn_kernel(q_ref, k_ref, v_ref, o_ref):
    scale = XA_HEAD_DIM ** -0.5
    for hd in range(XA_HEADS):
        ln = slice(hd * XA_HEAD_DIM, (hd + 1) * XA_HEAD_DIM)
        s = _mm_nt(q_ref[:, ln] * scale, k_ref[0, :, ln])
        s = s - jnp.max(s, axis=-1, keepdims=True)
        e = jnp.exp(s)
        pr = e / jnp.sum(e, axis=-1, keepdims=True)
        o_ref[:, ln] = _mm(pr, v_ref[0, :, ln]).astype(o_ref.dtype)


def _xattn(h, mem_k, mem_v, nseq, t):
    m = nseq * t
    tq = min(t, 512)
    nj = t // tq
    mem_spec = pl.BlockSpec((1, N_MEM, XA_WIDTH), lambda b, j: (b, 0, 0))
    return pl.pallas_call(
        _xattn_kernel,
        grid=(nseq, nj),
        in_specs=[pl.BlockSpec((tq, XA_WIDTH), lambda b, j: (b * nj + j, COL_XA_Q // XA_WIDTH)), mem_spec, mem_spec],
        out_specs=pl.BlockSpec((tq, XA_WIDTH), lambda b, j: (b * nj + j, 0)),
        out_shape=jax.ShapeDtypeStruct((m, XA_WIDTH), _mix_dtype(t)),
        compiler_params=_params("parallel", "arbitrary"),
        name="xattn",
    )(h, mem_k, mem_v)


_RW0 = 5136
W_IN_SEGMENTS = ((_RW0, _RW0 + 1536), (_RW0 + 1536, _RW0 + 3072), (_RW0 + 3072, _RW0 + 4608), (9872, 11408),
                 (0, 1024), (1024, 2048), (3072, 4096), (4096, 5120), (2048, 2560), (2560, 3072), (11408, 11920),
                 (_RW0 + 4608, _RW0 + 4736), (5120, 5136))


def _reorder_cast_kernel(w_ref, o_ref):
    dst = 0
    for a, b in W_IN_SEGMENTS:
        o_ref[0, :, dst:dst + (b - a)] = w_ref[0, :, a:b].astype(bf16)
        dst += b - a
    o_ref[0, :, dst:] = jnp.zeros((o_ref.shape[1], H_COLS - dst), bf16)


def _prep_w_in(w_in, tr=128):
    depth, d, n = w_in.shape
    return pl.pallas_call(
        _reorder_cast_kernel,
        grid=(depth, d // tr),
        in_specs=[pl.BlockSpec((1, tr, n), lambda l, i: (l, i, 0))],
        out_specs=pl.BlockSpec((1, tr, H_COLS), lambda l, i: (l, i, 0)),
        out_shape=jax.ShapeDtypeStruct((depth, d, H_COLS), bf16),
        compiler_params=_params("parallel", "parallel"),
        name="w_in_layout",
    )(w_in)


def _block_diag_s5(b, c):
    gl = S5_GROUPS // S5_BLOCKS
    eye = jnp.eye(gl, dtype=b.dtype)
    bb = b.reshape(S5_BLOCKS, gl, S5_STATE, S5_GROUP)
    bbd = jnp.einsum("kgpn,gh->kgnhp", bb, eye).reshape(S5_BLOCKS, gl * S5_GROUP, gl * S5_STATE)
    cc = c.reshape(S5_BLOCKS, gl, S5_GROUP, S5_STATE)
    cbd = jnp.einsum("kgnp,gh->kgphn", cc, eye).reshape(S5_BLOCKS, gl * S5_STATE, gl * S5_GROUP)
    return bbd, cbd


def _layer_params(l, a):
    row = lambda x: x.reshape(1, -1)
    bre, cre = _block_diag_s5(a["s5_b_re"][l], a["s5_c_re"][l])
    bim, cim = _block_diag_s5(a["s5_b_im"][l], a["s5_c_im"][l])
    s5 = dict(lam_re=row(a["s5_lam_re"][l]), lam_im=row(a["s5_lam_im"][l]),
              log_step=row(jnp.broadcast_to(a["s5_log_step"][l][:, None], (S5_GROUPS, S5_STATE))),
              b_re=bre, b_im=bim, c_re=cre.astype(bf16), c_im=cim.astype(bf16),
              d=row(a["s5_d"][l]), w_glu=a["s5_w_glu"][l].astype(bf16), b_glu=row(a["s5_b_glu"][l]))
    wg = jnp.zeros((LANE, GLA_QK), f32).at[:GLA_GATE_RANK].set(a["gla_w_gate"][l])
    gla = dict(w_gate=wg, b_gate=row(a["gla_b_gate"][l]), norm_g=row(a["gla_norm_g"][l]))
    mu = a["rw_mu"][l]
    w = RWKV_WIDTH
    z64 = jnp.zeros((64, w), f32)
    rw = dict(mu_r=row(mu[0:w]), mu_k=row(mu[w:2 * w]), mu_v=row(mu[2 * w:3 * w]), mu_wa=row(mu[3 * w:]),
              w0=row(a["rw_w0"][l]), w2=jnp.concatenate([a["rw_w2"][l], z64], 0).astype(bf16),
              a0=row(a["rw_a0"][l]), a2=jnp.concatenate([z64, a["rw_a2"][l]], 0).astype(bf16),
              k_k=row(a["rw_k_k"][l]), k_a=row(a["rw_k_a"][l]), r_k=row(a["rw_r_k"][l]),
              ln_w=row(a["rw_ln_w"][l]), ln_b=row(a["rw_ln_b"][l]))
    return s5, gla, rw


def _unpack_rwkv_state(s):
    b = s.shape[0]
    s = s.reshape(b, RWKV_PAIRS, 2, RWKV_HEAD, 2, RWKV_HEAD)
    return jnp.stack([s[:, :, 0, :, 0, :], s[:, :, 1, :, 1, :]], axis=2).reshape(b, RWKV_HEADS, RWKV_HEAD, RWKV_HEAD)


def _split_shift(x):
    w = RWKV_WIDTH
    return tuple(x[:, a:b] for a, b in ((0, w), (w, 2 * w), (2 * w, 3 * w), (3 * w, 3 * w + LANE)))


def _mixer_layer(x, nseq, t, mem_k, mem_v, h0r, h0i, s_gla, s_rw, l0, acc_gla, acc_rw, l, depth, prev, chain, lp,
                 w_in_l, w_out_parts, g_pre, g_post):
    s5p, glap, rwp = lp
    h = _norm_mm(x, g_pre, w_in_l)
    y_s5, hr, hi = _s5(h, h0r, h0i, s5p, nseq, t, chain)
    y_gla, s_gla_new = _gla(h, s_gla, l0, acc_gla, l, depth, glap, nseq, t)
    y_rw, s_rw_new = _rwkv(h, prev, s_rw, l0, acc_rw, l, depth, rwp, nseq, t, chain)
    y_xa = _xattn(h, mem_k, mem_v, nseq, t)
    x_new = _outproj((y_s5, y_gla, y_rw, y_xa), w_out_parts, g_post, x)
    last = h.reshape(nseq, t, H_COLS)[:, t - 1]
    w = RWKV_WIDTH
    shift = jnp.concatenate([last[:, COL_RW_R:COL_RW_R + w], last[:, COL_RW_K:COL_RW_K + w],
                             last[:, COL_RW_V:COL_RW_V + w], last[:, COL_RW_WA:COL_RW_WA + LANE]], axis=1)
    return x_new, jnp.stack([hr, hi], axis=-1).reshape(nseq, S5_GROUPS, S5_STATE, 2), s_gla_new, s_rw_new, shift


def kernel(x_prompt, x_sample, mem_prompt, cache_mem_k, cache_mem_v, state_s5, state_gla, state_rwkv, state_rwkv_shift, g_pre, g_post, w_in, w_out, s5_lam_re, s5_lam_im, s5_log_step, s5_b_re, s5_b_im, s5_c_re, s5_c_im, s5_d, s5_w_glu, s5_b_glu, gla_w_gate, gla_b_gate, gla_norm_g, rw_mu, rw_w0, rw_w2, rw_a0, rw_a2, rw_k_k, rw_k_a, rw_r_k, rw_ln_w, rw_ln_b, mem_g, w_mk, w_mv):
    raw = dict(s5_lam_re=s5_lam_re, s5_lam_im=s5_lam_im, s5_log_step=s5_log_step, s5_b_re=s5_b_re, s5_b_im=s5_b_im,
               s5_c_re=s5_c_re, s5_c_im=s5_c_im, s5_d=s5_d, s5_w_glu=s5_w_glu, s5_b_glu=s5_b_glu,
               gla_w_gate=gla_w_gate, gla_b_gate=gla_b_gate, gla_norm_g=gla_norm_g, rw_mu=rw_mu, rw_w0=rw_w0,
               rw_w2=rw_w2, rw_a0=rw_a0, rw_a2=rw_a2, rw_k_k=rw_k_k, rw_k_a=rw_k_a, rw_r_k=rw_r_k,
               rw_ln_w=rw_ln_w, rw_ln_b=rw_ln_b)
    bp, tp, d = x_prompt.shape
    bs, ts, _ = x_sample.shape
    depth = w_in.shape[0]
    w_in_p = _prep_w_in(w_in)
    w_out_b = w_out.astype(bf16)
    w_kv = jnp.concatenate([w_mk, w_mv], axis=2).astype(bf16)
    bounds = (0, S5_WIDTH, S5_WIDTH + GLA_WIDTH, S5_WIDTH + GLA_WIDTH + RWKV_WIDTH, D_MODEL)

    xp = x_prompt.reshape(bp * tp, d)
    xs = x_sample.reshape(bs * ts, d)
    mem2d = mem_prompt.reshape(bp * N_MEM, d)
    zp = lambda *s: jnp.zeros(s, f32)
    prev_p = _split_shift(zp(bp, RWKV_SHIFT_COLS))

    gla_p = gla_s = rw_p = rw_s = None
    zero_gla = zp(1, bp, GLA_HEADS, GLA_DK, GLA_DV)
    zero_rw = zp(1, bp, RWKV_PAIRS, LANE, LANE)

    outs = [[] for _ in range(6)]
    for l in range(depth):
        lp = _layer_params(l, raw)
        w_out_parts = tuple(w_out_b[l, bounds[i]:bounds[i + 1]] for i in range(4))
        gp, gq = g_pre[l].reshape(1, d), g_post[l].reshape(1, d)
        kv = _norm_mm(mem2d, mem_g[l].reshape(1, d), w_kv[l])
        k_mem = kv[:, :XA_WIDTH].reshape(bp, N_MEM, XA_WIDTH)
        v_mem = kv[:, XA_WIDTH:].reshape(bp, N_MEM, XA_WIDTH)
        xp, h1, gla_p, rw_p, p1 = _mixer_layer(
            xp, bp, tp, k_mem, v_mem, zp(bp, S5_NS), zp(bp, S5_NS), zero_gla, zero_rw, 0, gla_p, rw_p, l, depth,
            prev_p, True, lp, w_in_p[l], w_out_parts, gp, gq)
        st5 = state_s5[l].reshape(bs, S5_NS, 2)
        xs, h2, gla_s, rw_s, p2 = _mixer_layer(
            xs, bs, ts, cache_mem_k[l].reshape(bs, N_MEM, XA_WIDTH), cache_mem_v[l].reshape(bs, N_MEM, XA_WIDTH),
            st5[..., 0], st5[..., 1], state_gla, state_rwkv, l, gla_s, rw_s, l, depth,
            _split_shift(state_rwkv_shift[l]), False, lp, w_in_p[l], w_out_parts, gp, gq)
        vals = (k_mem.reshape(bp, N_MEM, XA_HEADS, XA_HEAD_DIM), v_mem.reshape(bp, N_MEM, XA_HEADS, XA_HEAD_DIM),
                h1, h2, p1, p2)
        for o, val in zip(outs, vals):
            o.append(val)

    mk, mv, s5_p, s5_s, sh_p, sh_s = (jnp.stack(o) for o in outs)
    rw_p_heads = _unpack_rwkv_state(rw_p.reshape(depth * bp, RWKV_PAIRS, LANE, LANE))
    rw_p_heads = rw_p_heads.reshape(depth, bp, RWKV_HEADS, RWKV_HEAD, RWKV_HEAD)
    return (xp.reshape(bp, tp, d), xs.reshape(bs, ts, d), mk, mv, s5_p, s5_s, gla_p, gla_s, rw_p_heads, rw_s,
            sh_p, sh_s)
```

```python
import functools

import jax
import jax.numpy as jnp
from jax import lax
from jax.experimental import pallas as pl
from jax.experimental.pallas import tpu as pltpu

f32 = jnp.float32
bf16 = jnp.bfloat16
HI = lax.Precision.HIGHEST

D_MODEL = 4096
DEPTH = 4
N_MEM = 256
NORM_EPS = 1e-6

S5_WIDTH = 1024
S5_GROUP = 16
S5_GROUPS = 64
S5_STATE = 64
S5_NS = S5_GROUPS * S5_STATE
S5_BLOCKS = 8
S5_LANES = 512

GLA_WIDTH = 1024
GLA_HEADS = 4
GLA_DV = 256
GLA_DK = 128
GLA_QK = 512
GLA_GATE_RANK = 16
GLA_GATE_TAU = 16.0

RWKV_WIDTH = 1536
RWKV_HEAD = 64
RWKV_HEADS = 24
RWKV_PAIRS = 12
RWKV_GROUP = 4
RWKV_GN_EPS = RWKV_HEAD * 1e-5
RWKV_SHIFT_COLS = 3 * RWKV_WIDTH + 128

XA_HEADS = 4
XA_HEAD_DIM = 128
XA_WIDTH = 512

LANE = 128
SUBLANE = 8
VMEM_LIMIT = 56 * 1024 * 1024

H_COLS = 12288
COL_RW_R, COL_RW_K, COL_RW_V, COL_RW_GATE = 0, 1536, 3072, 4608
COL_S5_U, COL_S5_GATE, COL_G_V, COL_G_GATE = 6144, 7168, 8192, 9216
COL_G_Q, COL_G_K, COL_XA_Q = 10240, 10752, 11264
COL_RW_WA, COL_G_LR = 11776, 11904


def _mm(a, b):
    return jnp.dot(a.astype(bf16), b.astype(bf16), preferred_element_type=f32)


def _mm_nt(a, b):
    return lax.dot_general(a.astype(bf16), b.astype(bf16), (((1,), (1,)), ((), ())), preferred_element_type=f32)


def _mm_tn(a, b):
    return lax.dot_general(a.astype(bf16), b.astype(bf16), (((0,), (0,)), ((), ())), preferred_element_type=f32)


def _mm_hi(a, b):
    return jnp.dot(a, b, precision=HI, preferred_element_type=f32)


def _split2(x):
    hi = x.astype(bf16)
    return hi, (x - hi.astype(f32)).astype(bf16)


def _mm_mask_l(mask01, x):
    hi, lo = _split2(x)
    m = mask01.astype(bf16)
    return jnp.dot(m, hi, preferred_element_type=f32) + jnp.dot(m, lo, preferred_element_type=f32)


def _mm_mask_r(x, mask01):
    hi, lo = _split2(x)
    m = mask01.astype(bf16)
    return jnp.dot(hi, m, preferred_element_type=f32) + jnp.dot(lo, m, preferred_element_type=f32)


def _mix_dtype(t):
    return bf16 if t % (2 * SUBLANE) == 0 else f32


def _params(*sem):
    return pltpu.CompilerParams(dimension_semantics=sem, vmem_limit_bytes=VMEM_LIMIT)


def _chunk_tril(n, c, strict=False):
    r = lax.broadcasted_iota(jnp.int32, (n, n), 0)
    q = lax.broadcasted_iota(jnp.int32, (n, n), 1)
    same = (r // c) == (q // c)
    return same & ((q < r) if strict else (q <= r))


def _norm_mm_kernel(x_ref, g_ref, w_ref, o_ref, xn_ref):
    @pl.when(pl.program_id(1) == 0)
    def _():
        x = x_ref[...]
        ms = jnp.mean(x * x, axis=-1, keepdims=True)
        xn_ref[...] = (x * lax.rsqrt(ms + NORM_EPS) * g_ref[...]).astype(bf16)

    o_ref[...] = lax.dot_general(xn_ref[...], w_ref[...], (((1,), (1,)), ((), ())), preferred_element_type=f32)


def _norm_mm(x, g, wt, l, tm=512, tn=512):
    m, d = x.shape
    n = wt.shape[1]
    return pl.pallas_call(
        _norm_mm_kernel,
        grid=(m // tm, n // tn),
        in_specs=[pl.BlockSpec((tm, d), lambda i, j: (i, 0)),
                  pl.BlockSpec((1, d), lambda i, j: (0, 0)),
                  pl.BlockSpec((None, tn, d), lambda i, j: (l, j, 0))],
        out_specs=pl.BlockSpec((tm, tn), lambda i, j: (i, j)),
        out_shape=jax.ShapeDtypeStruct((m, n), f32),
        scratch_shapes=[pltpu.VMEM((tm, d), bf16)],
        compiler_params=_params("parallel", "arbitrary"),
        name="norm_mm",
    )(x, g, wt)


def _outproj_kernel(a1, a2, a3, a4, w_ref, g_ref, x_ref, o_ref, acc_ref, *, nj, tn):
    j = pl.program_id(1)
    acc = None
    k0 = 0
    for a in (a1, a2, a3, a4):
        part = _mm(a[...], w_ref[k0:k0 + a.shape[1], :])
        acc = part if acc is None else acc + part
        k0 += a.shape[1]
    acc_ref[j] = acc

    @pl.when(j == nj - 1)
    def _():
        ssq = jnp.sum(acc_ref[0] * acc_ref[0], axis=-1, keepdims=True)
        for k in range(1, nj):
            ssq += jnp.sum(acc_ref[k] * acc_ref[k], axis=-1, keepdims=True)
        inv = lax.rsqrt(ssq * (1.0 / D_MODEL) + NORM_EPS)
        for k in range(nj):
            sl = slice(k * tn, (k + 1) * tn)
            o_ref[:, sl] = x_ref[:, sl] + acc_ref[k] * inv * g_ref[:, sl]


def _outproj(parts, w, l, g, x, tn=512):
    m, d = x.shape
    nj = d // tn
    tm = 512 if parts[0].dtype == bf16 else 256
    in_specs = [pl.BlockSpec((tm, p.shape[1]), lambda i, j: (i, 0)) for p in parts]
    in_specs += [pl.BlockSpec((None, d, tn), lambda i, j: (l, 0, j))]
    in_specs += [pl.BlockSpec((1, d), lambda i, j: (0, 0)),
                 pl.BlockSpec((tm, d), lambda i, j: (i, 0), pipeline_mode=pl.Buffered(1))]
    return pl.pallas_call(
        functools.partial(_outproj_kernel, nj=nj, tn=tn),
        grid=(m // tm, nj),
        in_specs=in_specs,
        out_specs=pl.BlockSpec((tm, d), lambda i, j: (i, 0)),
        out_shape=jax.ShapeDtypeStruct((m, d), f32),
        scratch_shapes=[pltpu.VMEM((nj, tm, tn), f32)],
        compiler_params=_params("parallel", "arbitrary"),
        name="outproj",
    )(*parts, w, g, x)


def _s5_kernel(u_ref, gate_ref, h0r_ref, h0i_ref, lre_ref, lim_ref, step_ref, bre_ref, bim_ref, cre_ref, cim_ref,
               d_ref, wglu_ref, bglu_ref, y_ref, hr_ref, hi_ref, sre, sim, car, cai, *, tt, chain):
    j = pl.program_id(1)
    nrt = tt // SUBLANE
    lr = lre_ref[...]
    li = lim_ref[...]
    st = jnp.exp(step_ref[...])
    kidx = (lax.broadcasted_iota(jnp.int32, (SUBLANE, S5_NS), 0) + 1).astype(f32)
    mag = jnp.exp(kidx * (lr * st))
    ang = kidx * (li * st)
    pw_re = mag * jnp.cos(ang)
    pw_im = mag * jnp.sin(ang)
    ab_re = pw_re[0:1]
    ab_im = pw_im[0:1]
    den = lr * lr + li * li
    f_re = ((ab_re - 1.0) * lr + ab_im * li) / den
    f_im = (ab_im * lr - (ab_re - 1.0) * li) / den

    u = u_ref[...]
    for blk in range(S5_BLOCKS):
        cols = slice(blk * S5_LANES, (blk + 1) * S5_LANES)
        fr = f_re[:, cols]
        fi = f_im[:, cols]
        b_re = bre_ref[blk]
        b_im = bim_ref[blk]
        ub = u[:, blk * LANE:(blk + 1) * LANE].astype(bf16)
        sre[:, cols] = _mm(ub, fr * b_re - fi * b_im)
        sim[:, cols] = _mm(ub, fr * b_im + fi * b_re)

    if chain:
        @pl.when(j == 0)
        def _():
            car[...] = jnp.broadcast_to(h0r_ref[0], (SUBLANE, S5_NS))
            cai[...] = jnp.broadcast_to(h0i_ref[0], (SUBLANE, S5_NS))

    rowi = lax.broadcasted_iota(jnp.int32, (SUBLANE, S5_LANES), 0)
    for ch in range(S5_NS // S5_LANES):
        cols = slice(ch * S5_LANES, (ch + 1) * S5_LANES)
        p_re = pw_re[:, cols]
        p_im = pw_im[:, cols]
        steps = [(d, p_re[d - 1:d], p_im[d - 1:d]) for d in (1, 2, 4)]

        def body(r, carry, cols=cols, p_re=p_re, p_im=p_im, steps=steps):
            rows = pl.ds(pl.multiple_of(r * SUBLANE, SUBLANE), SUBLANE)
            xr = sre[rows, cols]
            xi = sim[rows, cols]
            for d, ar, ai in steps:
                sr = jnp.where(rowi >= d, pltpu.roll(xr, d, 0), 0.0)
                si = jnp.where(rowi >= d, pltpu.roll(xi, d, 0), 0.0)
                xr, xi = xr + ar * sr - ai * si, xi + ar * si + ai * sr
            if chain:
                cr, ci = carry
            else:
                cr = jnp.broadcast_to(h0r_ref[pl.ds(r, 1), cols], (SUBLANE, S5_LANES))
                ci = jnp.broadcast_to(h0i_ref[pl.ds(r, 1), cols], (SUBLANE, S5_LANES))
            xr, xi = xr + p_re * cr - p_im * ci, xi + p_re * ci + p_im * cr
            sre[rows, cols] = xr
            sim[rows, cols] = xi
            if chain:
                return (jnp.broadcast_to(xr[SUBLANE - 1:SUBLANE], (SUBLANE, S5_LANES)),
                        jnp.broadcast_to(xi[SUBLANE - 1:SUBLANE], (SUBLANE, S5_LANES)))
            hr_ref[pl.ds(r, 1), cols] = xr[SUBLANE - 1:SUBLANE]
            hi_ref[pl.ds(r, 1), cols] = xi[SUBLANE - 1:SUBLANE]
            return carry

        if chain:
            cr, ci = lax.fori_loop(0, nrt, body, (car[:, cols], cai[:, cols]))
            car[:, cols] = cr
            cai[:, cols] = ci
        else:
            lax.fori_loop(0, nrt, body, 0)

    if chain:
        hr_ref[0] = car[0:1, :]
        hi_ref[0] = cai[0:1, :]

    ys = []
    for blk in range(S5_BLOCKS):
        cols = slice(blk * S5_LANES, (blk + 1) * S5_LANES)
        ys.append(_mm(sre[:, cols], cre_ref[blk]) - _mm(sim[:, cols], cim_ref[blk]))
    y = jnp.concatenate(ys, axis=1) + d_ref[...] * u
    z = jax.nn.gelu(y)
    gl = jax.nn.sigmoid(_mm(z, wglu_ref[...]) + bglu_ref[...])
    y_ref[...] = (z * gl * jax.nn.silu(gate_ref[...])).astype(y_ref.dtype)


def _s5(h, h0r, h0i, p, nseq, t, chain):
    m = nseq * t
    if chain:
        tt = min(t, 256)
        nj = t // tt
        grid = (nseq, nj)
        row = lambda b, j: b * nj + j
        h0r = h0r.reshape(nseq, 1, S5_NS)
        h0i = h0i.reshape(nseq, 1, S5_NS)
        h_spec = pl.BlockSpec((1, 1, S5_NS), lambda b, j: (b, 0, 0))
        h_shape = jax.ShapeDtypeStruct((nseq, 1, S5_NS), f32)
    else:
        assert t == SUBLANE
        tt = 64
        nj = m // tt
        grid = (1, nj)
        row = lambda b, j: j
        h_spec = pl.BlockSpec((tt // SUBLANE, S5_NS), lambda b, j: (j, 0))
        h_shape = jax.ShapeDtypeStruct((nseq, S5_NS), f32)
    cu, cg = COL_S5_U // S5_WIDTH, COL_S5_GATE // S5_WIDTH
    vec = lambda n: pl.BlockSpec((1, n), lambda b, j: (0, 0))
    full3 = lambda s: pl.BlockSpec(s, lambda b, j: (0, 0, 0))
    y, hr, hi = pl.pallas_call(
        functools.partial(_s5_kernel, tt=tt, chain=chain),
        grid=grid,
        in_specs=[pl.BlockSpec((tt, S5_WIDTH), lambda b, j: (row(b, j), cu)),
                  pl.BlockSpec((tt, S5_WIDTH), lambda b, j: (row(b, j), cg)),
                  h_spec, h_spec, vec(S5_NS), vec(S5_NS), vec(S5_NS),
                  full3((S5_BLOCKS, LANE, S5_LANES)), full3((S5_BLOCKS, LANE, S5_LANES)),
                  full3((S5_BLOCKS, S5_LANES, LANE)), full3((S5_BLOCKS, S5_LANES, LANE)),
                  vec(S5_WIDTH), pl.BlockSpec((S5_WIDTH, S5_WIDTH), lambda b, j: (0, 0)), vec(S5_WIDTH)],
        out_specs=[pl.BlockSpec((tt, S5_WIDTH), lambda b, j: (row(b, j), 0)), h_spec, h_spec],
        out_shape=[jax.ShapeDtypeStruct((m, S5_WIDTH), _mix_dtype(t)), h_shape, h_shape],
        scratch_shapes=[pltpu.VMEM((tt, S5_NS), f32), pltpu.VMEM((tt, S5_NS), f32),
                        pltpu.VMEM((SUBLANE, S5_NS), f32), pltpu.VMEM((SUBLANE, S5_NS), f32)],
        compiler_params=_params("parallel" if chain else "arbitrary", "arbitrary"),
        name="s5_chain" if chain else "s5_tiles",
    )(h, h, h0r, h0i, p["lam_re"], p["lam_im"], p["log_step"], p["b_re"], p["b_im"], p["c_re"], p["c_im"],
      p["d"], p["w_glu"], p["b_glu"])
    return y, hr.reshape(nseq, S5_NS), hi.reshape(nseq, S5_NS)


def _gla_kernel(q_ref, k_ref, v_ref, gate_ref, glr_ref, wg_ref, bg_ref, ng_ref, s0_ref, acc_ref, y_ref, s_ref,
                st_ref, b_ref, *, tt, c, nj):
    del acc_ref
    j = pl.program_id(1)

    @pl.when(j == 0)
    def _():
        for hd in range(GLA_HEADS):
            st_ref[hd] = s0_ref[0, hd].T

    la = jax.nn.log_sigmoid(_mm_hi(glr_ref[...], wg_ref[...]) + bg_ref[...]) * (1.0 / GLA_GATE_TAU)
    tri = jnp.where(_chunk_tril(tt, c), 1.0, 0.0).astype(f32)
    b_ref[...] = _mm_hi(tri, la)

    rowi = lax.broadcasted_iota(jnp.int32, (c, GLA_DK), 0)
    scale = GLA_DK ** -0.5

    def chunk(ci, carry):
        rows = pl.ds(pl.multiple_of(ci * c, c), c)
        for hd in range(GLA_HEADS):
            kl = slice(hd * GLA_DK, (hd + 1) * GLA_DK)
            vl = slice(hd * GLA_DV, (hd + 1) * GLA_DV)
            qh = q_ref[rows, kl] * scale
            kh = k_ref[rows, kl]
            vh = v_ref[rows, vl]
            bh = b_ref[rows, kl]
            o = jnp.zeros((c, GLA_DV), f32)
            for jj in range(c):
                dec = jnp.exp(jnp.where(rowi >= jj, bh - bh[jj:jj + 1], -jnp.inf))
                col = jnp.sum(qh * kh[jj:jj + 1] * dec, axis=-1, keepdims=True)
                o = o + col * vh[jj:jj + 1]
            s_t = st_ref[hd]
            o = o + _mm_nt(qh * jnp.exp(bh), s_t)
            b_last = bh[c - 1:c]
            st_ref[hd] = s_t * jnp.exp(b_last) + _mm_tn(vh, kh * jnp.exp(b_last - bh))
            on = o * lax.rsqrt(jnp.mean(o * o, axis=-1, keepdims=True) + NORM_EPS) * ng_ref[:, vl]
            y_ref[rows, vl] = (on * jax.nn.silu(gate_ref[rows, vl])).astype(y_ref.dtype)
        return carry

    lax.fori_loop(0, tt // c, chunk, 0)

    @pl.when(j == nj - 1)
    def _():
        for hd in range(GLA_HEADS):
            s_ref[0, hd] = st_ref[hd].T


def _stacked(acc, shape):
    if acc is None:
        return jnp.zeros((SUBLANE, LANE), f32), None
    assert acc.shape == shape
    return acc, 1


def _gla(h, s0, l0, acc, l, depth, p, nseq, t):
    m = nseq * t
    acc_shape = (depth, nseq, GLA_HEADS, GLA_DK, GLA_DV)
    acc, alias_to = _stacked(acc, acc_shape)
    c = min(t, 16)
    tt = min(t, 128)
    nj = t // tt
    row = lambda b, j: b * nj + j
    vec = lambda n: pl.BlockSpec((1, n), lambda b, j: (0, 0))
    st_spec = lambda li: pl.BlockSpec((None, 1, GLA_HEADS, GLA_DK, GLA_DV), lambda b, j: (li, b, 0, 0, 0))
    return pl.pallas_call(
        functools.partial(_gla_kernel, tt=tt, c=c, nj=nj),
        grid=(nseq, nj),
        in_specs=[pl.BlockSpec((tt, GLA_QK), lambda b, j: (row(b, j), COL_G_Q // GLA_QK)),
                  pl.BlockSpec((tt, GLA_QK), lambda b, j: (row(b, j), COL_G_K // GLA_QK)),
                  pl.BlockSpec((tt, GLA_WIDTH), lambda b, j: (row(b, j), COL_G_V // GLA_WIDTH)),
                  pl.BlockSpec((tt, GLA_WIDTH), lambda b, j: (row(b, j), COL_G_GATE // GLA_WIDTH)),
                  pl.BlockSpec((tt, LANE), lambda b, j: (row(b, j), COL_G_LR // LANE)),
                  pl.BlockSpec((LANE, GLA_QK), lambda b, j: (0, 0)), vec(GLA_QK), vec(GLA_WIDTH), st_spec(l0),
                  pl.BlockSpec(memory_space=pl.ANY)],
        out_specs=[pl.BlockSpec((tt, GLA_WIDTH), lambda b, j: (row(b, j), 0)), st_spec(l)],
        out_shape=[jax.ShapeDtypeStruct((m, GLA_WIDTH), _mix_dtype(t)), jax.ShapeDtypeStruct(acc_shape, f32)],
        scratch_shapes=[pltpu.VMEM((GLA_HEADS, GLA_DV, GLA_DK), f32), pltpu.VMEM((tt, GLA_QK), f32)],
        input_output_aliases={} if alias_to is None else {9: alias_to},
        compiler_params=_params("parallel", "arbitrary"),
        name="gla",
    )(h, h, h, h, h, p["w_gate"], p["b_gate"], p["norm_g"], s0, acc)


def _seg_sum(x, e2):
    return jnp.concatenate([_mm_mask_r(x[:, i * LANE:(i + 1) * LANE], e2) for i in range(RWKV_PAIRS)], axis=1)


def _rwkv_kernel(r_ref, k_ref, v_ref, gate_ref, wa_ref, pr_ref, pk_ref, pv_ref, pwa_ref,
                 mur_ref, muk_ref, muv_ref, muwa_ref, w0_ref, w2_ref, a0_ref, a2_ref, kk_ref, ka_ref, rk_ref,
                 lnw_ref, lnb_ref, s0_ref, acc_ref, y_ref, s_ref,
                 st, c_r, c_k, c_v, c_wa, x_kkg, x_rg, x_bi, x_ki, x_bend, x_kend, x_gam, x_v, x_bon,
                 x_tk, x_rq, x_uv, x_yv, x_y, *, tt, c, nj, chain):
    del acc_ref
    j = pl.program_id(1)
    hd = RWKV_HEAD

    if chain:
        @pl.when(j == 0)
        def _():
            st[...] = s0_ref[0]
            c_r[...] = jnp.broadcast_to(pr_ref[0], c_r.shape)
            c_k[...] = jnp.broadcast_to(pk_ref[0], c_k.shape)
            c_v[...] = jnp.broadcast_to(pv_ref[0], c_v.shape)
            c_wa[...] = jnp.broadcast_to(pwa_ref[0], c_wa.shape)
    else:
        @pl.when(j == 0)
        def _():
            st[...] = jnp.zeros(st.shape, f32)

    def shift(x_ref, c_ref, p_ref, mu_ref):
        x = x_ref[...]
        rowid = lax.broadcasted_iota(jnp.int32, x.shape, 0)
        rolled = pltpu.roll(x, 1, 0)
        if chain:
            prev = jnp.where(rowid == 0, c_ref[0:1, :], rolled)
            c_ref[...] = jnp.broadcast_to(x[tt - 1:tt, :], c_ref.shape)
        else:
            prev = jnp.where(rowid % c == 0, p_ref[...], rolled)
        return x + (prev - x) * mu_ref[...]

    r = shift(r_ref, c_r, pr_ref, mur_ref)
    k = shift(k_ref, c_k, pk_ref, muk_ref)
    v = shift(v_ref, c_v, pv_ref, muv_ref)
    wa = shift(wa_ref, c_wa, pwa_ref, muwa_ref)

    w_log = -jax.nn.softplus(-(w0_ref[...] + _mm(jnp.tanh(wa), w2_ref[...]))) - 0.5
    lw = -jnp.exp(w_log)
    a = jax.nn.sigmoid(a0_ref[...] + _mm(wa, a2_ref[...]))

    lane = lax.broadcasted_iota(jnp.int32, (LANE, LANE), 1)
    sub = lax.broadcasted_iota(jnp.int32, (LANE, LANE), 0)
    bd = (lane // RWKV_HEAD) == (sub // RWKV_HEAD)
    e2 = jnp.where(bd, 1.0, 0.0).astype(f32)

    kk = k * kk_ref[...]
    kk = kk / jnp.maximum(jnp.sqrt(_seg_sum(kk * kk, e2)), 1e-12)
    k2 = k * (1.0 + (a - 1.0) * ka_ref[...])
    bv = kk * a
    ri = lax.broadcasted_iota(jnp.int32, (tt, tt), 0)
    qi = lax.broadcasted_iota(jnp.int32, (tt, tt), 1)
    same = (ri // c) == (qi // c)
    strict = same & (qi < ri)
    incl = same & (qi <= ri)
    eye = jnp.where(qi == ri, 1.0, 0.0).astype(f32)
    cl = _mm_mask_l(jnp.where(incl, 1.0, 0.0), lw)
    cll = _mm_mask_l(jnp.where(same, 1.0, 0.0), lw)
    ginv = jnp.exp(-cl)
    gend = jnp.exp(cll - cl)
    x_kkg[...] = kk * jnp.exp(cl - lw)
    x_rg[...] = r * jnp.exp(cl)
    x_bi[...] = bv * ginv
    x_ki[...] = k2 * ginv
    x_bend[...] = bv * gend
    x_kend[...] = k2 * gend
    x_gam[...] = jnp.exp(cll)
    x_v[...] = v
    x_bon[...] = _seg_sum(r * k2 * rk_ref[...], e2) * v

    head_a = lax.broadcasted_iota(jnp.int32, (tt, LANE), 1) < RWKV_HEAD

    def sel(xa, xb):
        return jnp.where(head_a, xa, xb)

    def tri_inv(ams):
        ns = [-am for am in ams]
        ts = [eye + n for n in ns]
        if c <= 2:
            return ts
        ns = [_mm(n, n) for n in ns]
        width = 4
        while width < c:
            both = [_mm(jnp.concatenate([t, n], axis=0), n) for t, n in zip(ts, ns)]
            ts = [t + b[0:tt] for t, b in zip(ts, both)]
            ns = [b[tt:2 * tt] for b in both]
            width *= 2
        last = [_mm(t, n) for t, n in zip(ts, ns)]
        return [t + m for t, m in zip(ts, last)]

    def lanes(p):
        return slice(p * LANE, (p + 1) * LANE)

    for g0 in range(0, RWKV_PAIRS, RWKV_GROUP):
        grp = list(range(g0, g0 + RWKV_GROUP))
        kkg = [x_kkg[:, lanes(p)] for p in grp]
        rg = [x_rg[:, lanes(p)] for p in grp]
        vp = [x_v[:, lanes(p)] for p in grp]
        lhs = [jnp.concatenate([jnp.where(head_a, a, 0.0), jnp.where(head_a, 0.0, a),
                                jnp.where(head_a, b, 0.0), jnp.where(head_a, 0.0, b)], axis=0)
               for a, b in zip(kkg, rg)]
        if tt % LANE == 0:
            pbk = [_mm_nt(x, jnp.concatenate([x_bi[:, lanes(p)], x_ki[:, lanes(p)]], axis=0))
                   for x, p in zip(lhs, grp)]
            pb = [x[:, 0:tt] for x in pbk]
            pk = [x[:, tt:2 * tt] for x in pbk]
        else:
            pb = [_mm_nt(x, x_bi[:, lanes(p)]) for x, p in zip(lhs, grp)]
            pk = [_mm_nt(x, x_ki[:, lanes(p)]) for x, p in zip(lhs, grp)]
        tinv = tri_inv([jnp.where(strict, x[hh * tt:(hh + 1) * tt], 0.0) for x in pb for hh in (0, 1)])
        ar = [jnp.where(incl, x[hh * tt:(hh + 1) * tt], 0.0) for x in pb for hh in (2, 3)]
        bmr = [jnp.concatenate([jnp.where(strict, x[hh * tt:(hh + 1) * tt], 0.0),
                                jnp.where(incl, x[(hh + 2) * tt:(hh + 3) * tt], 0.0)], axis=0)
               for x in pk for hh in (0, 1)]
        mv = [_mm(x, vp[i // 2]) for i, x in enumerate(bmr)]
        bmv = [sel(mv[2 * i][0:tt], mv[2 * i + 1][0:tt]) for i in range(RWKV_GROUP)]
        brv = [sel(mv[2 * i][tt:2 * tt], mv[2 * i + 1][tt:2 * tt]) for i in range(RWKV_GROUP)]
        rhs_t = [jnp.concatenate([a, b], axis=1) for a, b in zip(bmv, kkg)]
        mt = [_mm(x, rhs_t[i // 2]) for i, x in enumerate(tinv)]
        uv = [-sel(mt[2 * i][:, 0:LANE], mt[2 * i + 1][:, 0:LANE]) for i in range(RWKV_GROUP)]
        tk = [sel(mt[2 * i][:, LANE:2 * LANE], mt[2 * i + 1][:, LANE:2 * LANE]) for i in range(RWKV_GROUP)]
        rhs_r = [jnp.concatenate([a, b], axis=1) for a, b in zip(uv, tk)]
        mr = [_mm(x, rhs_r[i // 2]) for i, x in enumerate(ar)]
        for i, p in enumerate(grp):
            x_uv[:, lanes(p)] = uv[i]
            x_tk[:, lanes(p)] = tk[i]
            x_yv[:, lanes(p)] = sel(mr[2 * i][:, 0:LANE], mr[2 * i + 1][:, 0:LANE]) + brv[i]
            x_rq[:, lanes(p)] = rg[i] - sel(mr[2 * i][:, LANE:2 * LANE], mr[2 * i + 1][:, LANE:2 * LANE])

    def chunk(ci, carry):
        row0 = pl.multiple_of(ci * c, c)
        rows = pl.ds(row0, c)
        pairs = range(RWKV_PAIRS)
        if not chain:
            for p in pairs:
                st[p, 0:hd, 0:hd] = s0_ref[ci, 2 * p]
                st[p, hd:2 * hd, hd:2 * hd] = s0_ref[ci, 2 * p + 1]
        s = [st[p] for p in pairs]
        zy = [_mm_nt(jnp.concatenate([x_tk[rows, lanes(p)], x_rq[rows, lanes(p)]], axis=0), s[p]) for p in pairs]
        u = [x_uv[rows, lanes(p)] - zy[p][0:c] for p in pairs]
        for p in pairs:
            x_y[rows, lanes(p)] = zy[p][c:2 * c] + x_yv[rows, lanes(p)]
        upd = [_mm_tn(jnp.concatenate([u[p], x_v[rows, lanes(p)]], axis=0),
                      jnp.concatenate([x_bend[rows, lanes(p)], x_kend[rows, lanes(p)]], axis=0)) for p in pairs]
        for p in pairs:
            s_new = s[p] * x_gam[pl.ds(row0, 1), lanes(p)] + jnp.where(bd, upd[p], 0.0)
            if chain:
                st[p] = s_new
            else:
                s_ref[ci, 2 * p] = s_new[0:hd, 0:hd]
                s_ref[ci, 2 * p + 1] = s_new[hd:2 * hd, hd:2 * hd]
        return carry

    lax.fori_loop(0, tt // c, chunk, 0)

    y = x_y[...]
    mean = _seg_sum(y, e2) * (1.0 / RWKV_HEAD)
    yc = y - mean
    var = _seg_sum(yc * yc, e2) * (1.0 / RWKV_HEAD)
    yn = yc * lax.rsqrt(var + RWKV_GN_EPS) * lnw_ref[...] + lnb_ref[...] + x_bon[...]
    y_ref[...] = (yn * jax.nn.silu(gate_ref[...])).astype(y_ref.dtype)

    if chain:
        @pl.when(j == nj - 1)
        def _():
            s_ref[0] = st[...]


def _rwkv(h, prev, s0, l0, acc, l, depth, p, nseq, t, chain):
    m = nseq * t
    w = RWKV_WIDTH
    acc_shape = (depth, nseq) + ((RWKV_PAIRS, LANE, LANE) if chain else (RWKV_HEADS, RWKV_HEAD, RWKV_HEAD))
    acc, alias_to = _stacked(acc, acc_shape)
    if chain:
        tt = min(t, 128)
        c = min(tt, 64)
        nj = t // tt
        grid = (nseq, nj)
        row = lambda b, j: b * nj + j
        prev = tuple(x[:, None, :] for x in prev)
        prow = lambda n: pl.BlockSpec((1, 1, n), lambda b, j: (b, 0, 0))
        st_spec = lambda li: pl.BlockSpec((None, 1, RWKV_PAIRS, LANE, LANE), lambda b, j: (li, b, 0, 0, 0))
    else:
        c = t
        tt = 64
        nj = m // tt
        grid = (1, nj)
        row = lambda b, j: j
        prev = tuple(jnp.repeat(x, t, axis=0) for x in prev)
        prow = lambda n: pl.BlockSpec((tt, n), lambda b, j: (j, 0))
        st_spec = lambda li: pl.BlockSpec((None, tt // c, RWKV_HEADS, RWKV_HEAD, RWKV_HEAD),
                                          lambda b, j: (li, j, 0, 0, 0))
    vec = lambda n: pl.BlockSpec((1, n), lambda b, j: (0, 0))
    tok = lambda col: pl.BlockSpec((tt, w), lambda b, j: (row(b, j), col // w))
    big = lambda: pltpu.VMEM((tt, w), f32)
    return pl.pallas_call(
        functools.partial(_rwkv_kernel, tt=tt, c=c, nj=nj, chain=chain),
        grid=grid,
        in_specs=[tok(COL_RW_R), tok(COL_RW_K), tok(COL_RW_V), tok(COL_RW_GATE),
                  pl.BlockSpec((tt, LANE), lambda b, j: (row(b, j), COL_RW_WA // LANE)),
                  prow(w), prow(w), prow(w), prow(LANE),
                  vec(w), vec(w), vec(w), vec(LANE),
                  vec(w), pl.BlockSpec((LANE, w), lambda b, j: (0, 0)), vec(w),
                  pl.BlockSpec((LANE, w), lambda b, j: (0, 0)), vec(w), vec(w), vec(w), vec(w), vec(w), st_spec(l0),
                  pl.BlockSpec(memory_space=pl.ANY)],
        out_specs=[pl.BlockSpec((tt, w), lambda b, j: (row(b, j), 0)), st_spec(l)],
        out_shape=[jax.ShapeDtypeStruct((m, w), _mix_dtype(t)), jax.ShapeDtypeStruct(acc_shape, f32)],
        input_output_aliases={} if alias_to is None else {23: alias_to},
        scratch_shapes=[pltpu.VMEM((RWKV_PAIRS, LANE, LANE), f32),
                        pltpu.VMEM((SUBLANE, w), f32), pltpu.VMEM((SUBLANE, w), f32),
                        pltpu.VMEM((SUBLANE, w), f32), pltpu.VMEM((SUBLANE, LANE), f32)] + [big() for _ in range(14)],
        compiler_params=_params("parallel" if chain else "arbitrary", "arbitrary"),
        name="rwkv_chain" if chain else "rwkv_tiles",
    )(h, h, h, h, h, *prev, p["mu_r"], p["mu_k"], p["mu_v"], p["mu_wa"], p["w0"], p["w2"], p["a0"], p["a2"],
      p["k_k"], p["k_a"], p["r_k"], p["ln_w"], p["ln_b"], s0, acc)


def _xattn_kernel(q_ref, k_ref, v_ref, o_ref):
    scale = XA_HEAD_DIM ** -0.5
    heads = range(XA_HEADS)
    ln = [slice(hd * XA_HEAD_DIM, (hd + 1) * XA_HEAD_DIM) for hd in heads]
    rows = [pl.ds(hd, N_MEM, stride=XA_HEADS) for hd in heads]
    s = [_mm_nt(q_ref[:, ln[hd]] * scale, k_ref[0, rows[hd], :]) for hd in heads]
    s = [x - jnp.max(x, axis=-1, keepdims=True) for x in s]
    e = [jnp.exp(x) for x in s]
    pr = [x / jnp.sum(x, axis=-1, keepdims=True) for x in e]
    o = [_mm(pr[hd], v_ref[0, rows[hd], :]) for hd in heads]
    for hd in heads:
        o_ref[:, ln[hd]] = o[hd].astype(o_ref.dtype)


def _xattn(h, mem_k, mem_v, nseq, t):
    m = nseq * t
    tq = min(t, 512)
    nj = t // tq
    mem_spec = pl.BlockSpec((1, N_MEM * XA_HEADS, XA_HEAD_DIM), lambda b, j: (b, 0, 0))
    return pl.pallas_call(
        _xattn_kernel,
        grid=(nseq, nj),
        in_specs=[pl.BlockSpec((tq, XA_WIDTH), lambda b, j: (b * nj + j, COL_XA_Q // XA_WIDTH)), mem_spec, mem_spec],
        out_specs=pl.BlockSpec((tq, XA_WIDTH), lambda b, j: (b * nj + j, 0)),
        out_shape=jax.ShapeDtypeStruct((m, XA_WIDTH), _mix_dtype(t)),
        compiler_params=_params("parallel", "arbitrary"),
        name="xattn",
    )(h, mem_k, mem_v)


_RW0 = 5136
W_IN_SEGMENTS = ((_RW0, _RW0 + 1536), (_RW0 + 1536, _RW0 + 3072), (_RW0 + 3072, _RW0 + 4608), (9872, 11408),
                 (0, 1024), (1024, 2048), (3072, 4096), (4096, 5120), (2048, 2560), (2560, 3072), (11408, 11920),
                 (_RW0 + 4608, _RW0 + 4736), (5120, 5136))


def _reorder_cast_kernel(w_ref, o_ref):
    dst = 0
    for a, b in W_IN_SEGMENTS:
        o_ref[dst:dst + (b - a), :] = w_ref[a:b, :].astype(bf16)
        dst += b - a
    o_ref[dst:, :] = jnp.zeros((H_COLS - dst, o_ref.shape[1]), bf16)


def _prep_w_in(w_in, tk=256):
    depth, d, n = w_in.shape
    return pl.pallas_call(
        _reorder_cast_kernel,
        grid=(depth, d // tk),
        in_specs=[pl.BlockSpec((None, n, tk), lambda l, i: (l, 0, i))],
        out_specs=pl.BlockSpec((None, H_COLS, tk), lambda l, i: (l, 0, i)),
        out_shape=jax.ShapeDtypeStruct((depth, H_COLS, d), bf16),
        compiler_params=_params("parallel", "parallel"),
        name="w_in_layout",
    )(jnp.swapaxes(w_in, 1, 2))


def _block_diag_s5(b, c):
    gl = S5_GROUPS // S5_BLOCKS
    eye = jnp.eye(gl, dtype=b.dtype)
    bb = b.reshape(S5_BLOCKS, gl, S5_STATE, S5_GROUP)
    bbd = jnp.einsum("kgpn,gh->kgnhp", bb, eye).reshape(S5_BLOCKS, gl * S5_GROUP, gl * S5_STATE)
    cc = c.reshape(S5_BLOCKS, gl, S5_GROUP, S5_STATE)
    cbd = jnp.einsum("kgnp,gh->kgphn", cc, eye).reshape(S5_BLOCKS, gl * S5_STATE, gl * S5_GROUP)
    return bbd, cbd


def _layer_params(l, a):
    row = lambda x: x.reshape(1, -1)
    bre, cre = _block_diag_s5(a["s5_b_re"][l], a["s5_c_re"][l])
    bim, cim = _block_diag_s5(a["s5_b_im"][l], a["s5_c_im"][l])
    s5 = dict(lam_re=row(a["s5_lam_re"][l]), lam_im=row(a["s5_lam_im"][l]),
              log_step=row(jnp.broadcast_to(a["s5_log_step"][l][:, None], (S5_GROUPS, S5_STATE))),
              b_re=bre, b_im=bim, c_re=cre.astype(bf16), c_im=cim.astype(bf16),
              d=row(a["s5_d"][l]), w_glu=a["s5_w_glu"][l].astype(bf16), b_glu=row(a["s5_b_glu"][l]))
    wg = jnp.zeros((LANE, GLA_QK), f32).at[:GLA_GATE_RANK].set(a["gla_w_gate"][l])
    gla = dict(w_gate=wg, b_gate=row(a["gla_b_gate"][l]), norm_g=row(a["gla_norm_g"][l]))
    mu = a["rw_mu"][l]
    w = RWKV_WIDTH
    z64 = jnp.zeros((64, w), f32)
    rw = dict(mu_r=row(mu[0:w]), mu_k=row(mu[w:2 * w]), mu_v=row(mu[2 * w:3 * w]), mu_wa=row(mu[3 * w:]),
              w0=row(a["rw_w0"][l]), w2=jnp.concatenate([a["rw_w2"][l], z64], 0).astype(bf16),
              a0=row(a["rw_a0"][l]), a2=jnp.concatenate([z64, a["rw_a2"][l]], 0).astype(bf16),
              k_k=row(a["rw_k_k"][l]), k_a=row(a["rw_k_a"][l]), r_k=row(a["rw_r_k"][l]),
              ln_w=row(a["rw_ln_w"][l]), ln_b=row(a["rw_ln_b"][l]))
    return s5, gla, rw


def _unpack_rwkv_state(s):
    b = s.shape[0]
    s = s.reshape(b, RWKV_PAIRS, 2, RWKV_HEAD, 2, RWKV_HEAD)
    return jnp.stack([s[:, :, 0, :, 0, :], s[:, :, 1, :, 1, :]], axis=2).reshape(b, RWKV_HEADS, RWKV_HEAD, RWKV_HEAD)


def _split_shift(x):
    w = RWKV_WIDTH
    return tuple(x[:, a:b] for a, b in ((0, w), (w, 2 * w), (2 * w, 3 * w), (3 * w, 3 * w + LANE)))


def _mixer_layer(x, nseq, t, mem_k, mem_v, h0r, h0i, s_gla, s_rw, l0, acc_gla, acc_rw, l, depth, prev, chain, lp,
                 w_in_t, w_out_b, g_pre, g_post):
    s5p, glap, rwp = lp
    h = _norm_mm(x, g_pre, w_in_t, l)
    y_s5, hr, hi = _s5(h, h0r, h0i, s5p, nseq, t, chain)
    y_gla, s_gla_new = _gla(h, s_gla, l0, acc_gla, l, depth, glap, nseq, t)
    y_rw, s_rw_new = _rwkv(h, prev, s_rw, l0, acc_rw, l, depth, rwp, nseq, t, chain)
    y_xa = _xattn(h, mem_k, mem_v, nseq, t)
    x_new = _outproj((y_s5, y_gla, y_rw, y_xa), w_out_b, l, g_post, x)
    last = h.reshape(nseq, t, H_COLS)[:, t - 1]
    w = RWKV_WIDTH
    shift = jnp.concatenate([last[:, COL_RW_R:COL_RW_R + w], last[:, COL_RW_K:COL_RW_K + w],
                             last[:, COL_RW_V:COL_RW_V + w], last[:, COL_RW_WA:COL_RW_WA + LANE]], axis=1)
    return x_new, jnp.stack([hr, hi], axis=-1).reshape(nseq, S5_GROUPS, S5_STATE, 2), s_gla_new, s_rw_new, shift


def kernel(x_prompt, x_sample, mem_prompt, cache_mem_k, cache_mem_v, state_s5, state_gla, state_rwkv, state_rwkv_shift, g_pre, g_post, w_in, w_out, s5_lam_re, s5_lam_im, s5_log_step, s5_b_re, s5_b_im, s5_c_re, s5_c_im, s5_d, s5_w_glu, s5_b_glu, gla_w_gate, gla_b_gate, gla_norm_g, rw_mu, rw_w0, rw_w2, rw_a0, rw_a2, rw_k_k, rw_k_a, rw_r_k, rw_ln_w, rw_ln_b, mem_g, w_mk, w_mv):
    raw = dict(s5_lam_re=s5_lam_re, s5_lam_im=s5_lam_im, s5_log_step=s5_log_step, s5_b_re=s5_b_re, s5_b_im=s5_b_im,
               s5_c_re=s5_c_re, s5_c_im=s5_c_im, s5_d=s5_d, s5_w_glu=s5_w_glu, s5_b_glu=s5_b_glu,
               gla_w_gate=gla_w_gate, gla_b_gate=gla_b_gate, gla_norm_g=gla_norm_g, rw_mu=rw_mu, rw_w0=rw_w0,
               rw_w2=rw_w2, rw_a0=rw_a0, rw_a2=rw_a2, rw_k_k=rw_k_k, rw_k_a=rw_k_a, rw_r_k=rw_r_k,
               rw_ln_w=rw_ln_w, rw_ln_b=rw_ln_b)
    bp, tp, d = x_prompt.shape
    bs, ts, _ = x_sample.shape
    depth = w_in.shape[0]
    w_in_t = _prep_w_in(w_in)
    w_out_b = w_out.astype(bf16)
    w_kv_t = jnp.swapaxes(jnp.concatenate([w_mk, w_mv], axis=2), 1, 2).astype(bf16)
    mem_rows = N_MEM * XA_HEADS

    xp = x_prompt.reshape(bp * tp, d)
    xs = x_sample.reshape(bs * ts, d)
    mem2d = mem_prompt.reshape(bp * N_MEM, d)
    zp = lambda *s: jnp.zeros(s, f32)
    prev_p = _split_shift(zp(bp, RWKV_SHIFT_COLS))

    gla_p = gla_s = rw_p = rw_s = None
    zero_gla = zp(1, bp, GLA_HEADS, GLA_DK, GLA_DV)
    zero_rw = zp(1, bp, RWKV_PAIRS, LANE, LANE)

    outs = [[] for _ in range(6)]
    for l in range(depth):
        lp = _layer_params(l, raw)
        gp, gq = g_pre[l].reshape(1, d), g_post[l].reshape(1, d)
        kv = _norm_mm(mem2d, mem_g[l].reshape(1, d), w_kv_t, l)
        k_mem = kv[:, :XA_WIDTH].reshape(bp, N_MEM, XA_WIDTH)
        v_mem = kv[:, XA_WIDTH:].reshape(bp, N_MEM, XA_WIDTH)
        xp, h1, gla_p, rw_p, p1 = _mixer_layer(
            xp, bp, tp, k_mem.reshape(bp, mem_rows, XA_HEAD_DIM), v_mem.reshape(bp, mem_rows, XA_HEAD_DIM),
            zp(bp, S5_NS), zp(bp, S5_NS), zero_gla, zero_rw, 0, gla_p, rw_p, l, depth,
            prev_p, True, lp, w_in_t, w_out_b, gp, gq)
        st5 = state_s5[l].reshape(bs, S5_NS, 2)
        xs, h2, gla_s, rw_s, p2 = _mixer_layer(
            xs, bs, ts, cache_mem_k[l].reshape(bs, mem_rows, XA_HEAD_DIM),
            cache_mem_v[l].reshape(bs, mem_rows, XA_HEAD_DIM),
            st5[..., 0], st5[..., 1], state_gla, state_rwkv, l, gla_s, rw_s, l, depth,
            _split_shift(state_rwkv_shift[l]), False, lp, w_in_t, w_out_b, gp, gq)
        vals = (k_mem.reshape(bp, N_MEM, XA_HEADS, XA_HEAD_DIM), v_mem.reshape(bp, N_MEM, XA_HEADS, XA_HEAD_DIM),
                h1, h2, p1, p2)
        for o, val in zip(outs, vals):
            o.append(val)

    mk, mv, s5_p, s5_s, sh_p, sh_s = (jnp.stack(o) for o in outs)
    rw_p_heads = _unpack_rwkv_state(rw_p.reshape(depth * bp, RWKV_PAIRS, LANE, LANE))
    rw_p_heads = rw_p_heads.reshape(depth, bp, RWKV_HEADS, RWKV_HEAD, RWKV_HEAD)
    return (xp.reshape(bp, tp, d), xs.reshape(bs, ts, d), mk, mv, s5_p, s5_s, gla_p, gla_s, rw_p_heads, rw_s,
            sh_p, sh_s)
```

```python
import functools
import math

import jax
import jax.numpy as jnp
from jax import lax
from jax.experimental import pallas as pl
from jax.experimental.pallas import tpu as pltpu

f32 = jnp.float32
bf16 = jnp.bfloat16

D_MODEL = 4096
DEPTH = 4
N_MEM = 256
NORM_EPS = 1e-6

S5_WIDTH = 1024
S5_GROUP = 16
S5_GROUPS = 64
S5_STATE = 64
S5_NS = S5_GROUPS * S5_STATE
S5_BLOCKS = 8
S5_LANES = 512

GLA_WIDTH = 1024
GLA_HEADS = 4
GLA_DV = 256
GLA_DK = 128
GLA_QK = 512
GLA_GATE_RANK = 16
GLA_GATE_TAU = 16.0

RWKV_WIDTH = 1536
RWKV_HEAD = 64
RWKV_HEADS = 24
RWKV_PAIRS = 12
RWKV_GROUP = 12
RWKV_GN_EPS = RWKV_HEAD * 1e-5
RWKV_SHIFT_COLS = 3 * RWKV_WIDTH + 128

XA_HEADS = 4
XA_HEAD_DIM = 128
XA_WIDTH = 512

LANE = 128
SUBLANE = 8
VMEM_LIMIT = 56 * 1024 * 1024

H_COLS = 12288
COL_RW_R, COL_RW_K, COL_RW_V, COL_RW_GATE = 0, 1536, 3072, 4608
COL_S5_U, COL_S5_GATE, COL_G_V, COL_G_GATE = 6144, 7168, 8192, 9216
COL_G_Q, COL_G_K, COL_XA_Q = 10240, 10752, 11264
COL_RW_WA, COL_G_LR = 11776, 11904


def _mm(a, b):
    return jnp.dot(a.astype(bf16), b.astype(bf16), preferred_element_type=f32)


def _mm_nt(a, b):
    return lax.dot_general(a.astype(bf16), b.astype(bf16), (((1,), (1,)), ((), ())), preferred_element_type=f32)


def _mm_tn(a, b):
    return lax.dot_general(a.astype(bf16), b.astype(bf16), (((0,), (0,)), ((), ())), preferred_element_type=f32)


def _split2(x):
    hi = x.astype(bf16)
    return hi, (x - hi.astype(f32)).astype(bf16)


def _mm_mask_l(mask01, x):
    hi, lo = _split2(x)
    m = mask01.astype(bf16)
    return jnp.dot(m, hi, preferred_element_type=f32) + jnp.dot(m, lo, preferred_element_type=f32)


def _mix_dtype(t):
    return bf16 if t % (2 * SUBLANE) == 0 else f32


def _params(*sem):
    return pltpu.CompilerParams(dimension_semantics=sem, vmem_limit_bytes=VMEM_LIMIT)


def _chunk_tril(n, c, strict=False):
    r = lax.broadcasted_iota(jnp.int32, (n, n), 0)
    q = lax.broadcasted_iota(jnp.int32, (n, n), 1)
    same = (r // c) == (q // c)
    return same & ((q < r) if strict else (q <= r))


def _norm_mm_kernel(x_ref, g_ref, w_ref, o_ref, xn_ref):
    @pl.when(pl.program_id(1) == 0)
    def _():
        rows = 256
        for r0 in range(0, x_ref.shape[0], rows):
            x = x_ref[r0:r0 + rows, :]
            ms = jnp.mean(x * x, axis=-1, keepdims=True)
            xn_ref[r0:r0 + rows, :] = (x * lax.rsqrt(ms + NORM_EPS) * g_ref[...]).astype(bf16)

    o_ref[...] = lax.dot_general(xn_ref[...], w_ref[...], (((1,), (1,)), ((), ())), preferred_element_type=f32)


def _norm_mm(x, g, wt, l):
    m, d = x.shape
    n = wt.shape[1]
    tm = min(m, 1024)
    tn = 768 if n % 768 == 0 else 512
    xmode = dict(pipeline_mode=pl.Buffered(1))
    return pl.pallas_call(
        _norm_mm_kernel,
        grid=(m // tm, n // tn),
        in_specs=[pl.BlockSpec((tm, d), lambda i, j: (i, 0), **xmode),
                  pl.BlockSpec((1, d), lambda i, j: (0, 0)),
                  pl.BlockSpec((None, tn, d), lambda i, j: (l, j, 0))],
        out_specs=pl.BlockSpec((tm, tn), lambda i, j: (i, j)),
        out_shape=jax.ShapeDtypeStruct((m, n), f32),
        scratch_shapes=[pltpu.VMEM((tm, d), bf16)],
        compiler_params=_params("parallel", "arbitrary"),
        name="norm_mm",
    )(x, g, wt)


def _outproj_kernel(a1, a2, a3, a4, w_ref, g_ref, x_ref, o_ref, acc_ref, *, nj, tn):
    j = pl.program_id(1)
    acc = None
    k0 = 0
    for a in (a1, a2, a3, a4):
        part = _mm(a[...], w_ref[k0:k0 + a.shape[1], :])
        acc = part if acc is None else acc + part
        k0 += a.shape[1]
    acc_ref[j] = acc

    @pl.when(j == nj - 1)
    def _():
        ssq = jnp.sum(acc_ref[0] * acc_ref[0], axis=-1, keepdims=True)
        for k in range(1, nj):
            ssq += jnp.sum(acc_ref[k] * acc_ref[k], axis=-1, keepdims=True)
        inv = lax.rsqrt(ssq * (1.0 / D_MODEL) + NORM_EPS)
        for k in range(nj):
            sl = slice(k * tn, (k + 1) * tn)
            o_ref[:, sl] = x_ref[:, sl] + acc_ref[k] * inv * g_ref[:, sl]


def _outproj(parts, w, l, g, x, tn=512):
    m, d = x.shape
    nj = d // tn
    tm = 512 if parts[0].dtype == bf16 else 256
    in_specs = [pl.BlockSpec((tm, p.shape[1]), lambda i, j: (i, 0)) for p in parts]
    in_specs += [pl.BlockSpec((None, d, tn), lambda i, j: (l, 0, j))]
    in_specs += [pl.BlockSpec((1, d), lambda i, j: (0, 0)),
                 pl.BlockSpec((tm, d), lambda i, j: (i, 0), pipeline_mode=pl.Buffered(1))]
    return pl.pallas_call(
        functools.partial(_outproj_kernel, nj=nj, tn=tn),
        grid=(m // tm, nj),
        in_specs=in_specs,
        out_specs=pl.BlockSpec((tm, d), lambda i, j: (i, 0)),
        out_shape=jax.ShapeDtypeStruct((m, d), f32),
        scratch_shapes=[pltpu.VMEM((nj, tm, tn), f32)],
        compiler_params=_params("parallel", "arbitrary"),
        name="outproj",
    )(*parts, w, g, x)


def _s5_kernel(u_ref, gate_ref, h0r_ref, h0i_ref, lre_ref, lim_ref, step_ref, bre_ref, bim_ref, cre_ref, cim_ref,
               d_ref, wglu_ref, bglu_ref, y_ref, hr_ref, hi_ref, sre, sim, car, cai, *, tt, chain):
    j = pl.program_id(1)
    nrt = tt // SUBLANE
    lr = lre_ref[...]
    li = lim_ref[...]
    st = jnp.exp(step_ref[...])
    kidx = (lax.broadcasted_iota(jnp.int32, (SUBLANE, S5_NS), 0) + 1).astype(f32)
    mag = jnp.exp(kidx * (lr * st))
    ang = kidx * (li * st)
    pw_re = mag * jnp.cos(ang)
    pw_im = mag * jnp.sin(ang)
    ab_re = pw_re[0:1]
    ab_im = pw_im[0:1]
    den = lr * lr + li * li
    f_re = ((ab_re - 1.0) * lr + ab_im * li) / den
    f_im = (ab_im * lr - (ab_re - 1.0) * li) / den

    u = u_ref[...]
    for blk in range(S5_BLOCKS):
        cols = slice(blk * S5_LANES, (blk + 1) * S5_LANES)
        fr = f_re[:, cols]
        fi = f_im[:, cols]
        b_re = bre_ref[blk]
        b_im = bim_ref[blk]
        ub = u[:, blk * LANE:(blk + 1) * LANE].astype(bf16)
        sre[:, cols] = _mm(ub, fr * b_re - fi * b_im)
        sim[:, cols] = _mm(ub, fr * b_im + fi * b_re)

    if chain:
        @pl.when(j == 0)
        def _():
            car[...] = jnp.broadcast_to(h0r_ref[0], (SUBLANE, S5_NS))
            cai[...] = jnp.broadcast_to(h0i_ref[0], (SUBLANE, S5_NS))

    rowi = lax.broadcasted_iota(jnp.int32, (SUBLANE, S5_LANES), 0)
    for ch in range(S5_NS // S5_LANES):
        cols = slice(ch * S5_LANES, (ch + 1) * S5_LANES)
        p_re = pw_re[:, cols]
        p_im = pw_im[:, cols]
        steps = [(d, jnp.where(rowi >= d, p_re[d - 1:d], 0.0), jnp.where(rowi >= d, p_im[d - 1:d], 0.0))
                 for d in (1, 2, 4)]

        def body(r, carry, cols=cols, p_re=p_re, p_im=p_im, steps=steps):
            rows = pl.ds(pl.multiple_of(r * SUBLANE, SUBLANE), SUBLANE)
            xr = sre[rows, cols]
            xi = sim[rows, cols]
            for d, ar, ai in steps:
                sr = pltpu.roll(xr, d, 0)
                si = pltpu.roll(xi, d, 0)
                xr, xi = xr + ar * sr - ai * si, xi + ar * si + ai * sr
            if chain:
                cr, ci = carry
            else:
                cr = jnp.broadcast_to(h0r_ref[pl.ds(r, 1), cols], (SUBLANE, S5_LANES))
                ci = jnp.broadcast_to(h0i_ref[pl.ds(r, 1), cols], (SUBLANE, S5_LANES))
            xr, xi = xr + p_re * cr - p_im * ci, xi + p_re * ci + p_im * cr
            sre[rows, cols] = xr
            sim[rows, cols] = xi
            if chain:
                return (jnp.broadcast_to(xr[SUBLANE - 1:SUBLANE], (SUBLANE, S5_LANES)),
                        jnp.broadcast_to(xi[SUBLANE - 1:SUBLANE], (SUBLANE, S5_LANES)))
            hr_ref[pl.ds(r, 1), cols] = xr[SUBLANE - 1:SUBLANE]
            hi_ref[pl.ds(r, 1), cols] = xi[SUBLANE - 1:SUBLANE]
            return carry

        if chain:
            cr, ci = lax.fori_loop(0, nrt, body, (car[:, cols], cai[:, cols]))
            car[:, cols] = cr
            cai[:, cols] = ci
        else:
            lax.fori_loop(0, nrt, body, 0)

    if chain:
        hr_ref[0] = car[0:1, :]
        hi_ref[0] = cai[0:1, :]

    ys = []
    for blk in range(S5_BLOCKS):
        cols = slice(blk * S5_LANES, (blk + 1) * S5_LANES)
        ys.append(_mm(sre[:, cols], cre_ref[blk]) - _mm(sim[:, cols], cim_ref[blk]))
    y = jnp.concatenate(ys, axis=1) + d_ref[...] * u
    z = jax.nn.gelu(y)
    gl = jax.nn.sigmoid(_mm(z, wglu_ref[...]) + bglu_ref[...])
    y_ref[...] = (z * gl * jax.nn.silu(gate_ref[...])).astype(y_ref.dtype)


def _s5(h, h0r, h0i, p, nseq, t, chain):
    m = nseq * t
    if chain:
        tt = min(t, 256)
        nj = t // tt
        grid = (nseq, nj)
        row = lambda b, j: b * nj + j
        h0r = h0r.reshape(nseq, 1, S5_NS)
        h0i = h0i.reshape(nseq, 1, S5_NS)
        h_spec = pl.BlockSpec((1, 1, S5_NS), lambda b, j: (b, 0, 0))
        h_shape = jax.ShapeDtypeStruct((nseq, 1, S5_NS), f32)
    else:
        assert t == SUBLANE
        tt = 64
        nj = m // tt
        grid = (1, nj)
        row = lambda b, j: j
        h_spec = pl.BlockSpec((tt // SUBLANE, S5_NS), lambda b, j: (j, 0))
        h_shape = jax.ShapeDtypeStruct((nseq, S5_NS), f32)
    cu, cg = COL_S5_U // S5_WIDTH, COL_S5_GATE // S5_WIDTH
    vec = lambda n: pl.BlockSpec((1, n), lambda b, j: (0, 0))
    full3 = lambda s: pl.BlockSpec(s, lambda b, j: (0, 0, 0))
    y, hr, hi = pl.pallas_call(
        functools.partial(_s5_kernel, tt=tt, chain=chain),
        grid=grid,
        in_specs=[pl.BlockSpec((tt, S5_WIDTH), lambda b, j: (row(b, j), cu)),
                  pl.BlockSpec((tt, S5_WIDTH), lambda b, j: (row(b, j), cg)),
                  h_spec, h_spec, vec(S5_NS), vec(S5_NS), vec(S5_NS),
                  full3((S5_BLOCKS, LANE, S5_LANES)), full3((S5_BLOCKS, LANE, S5_LANES)),
                  full3((S5_BLOCKS, S5_LANES, LANE)), full3((S5_BLOCKS, S5_LANES, LANE)),
                  vec(S5_WIDTH), pl.BlockSpec((S5_WIDTH, S5_WIDTH), lambda b, j: (0, 0)), vec(S5_WIDTH)],
        out_specs=[pl.BlockSpec((tt, S5_WIDTH), lambda b, j: (row(b, j), 0)), h_spec, h_spec],
        out_shape=[jax.ShapeDtypeStruct((m, S5_WIDTH), _mix_dtype(t)), h_shape, h_shape],
        scratch_shapes=[pltpu.VMEM((tt, S5_NS), f32), pltpu.VMEM((tt, S5_NS), f32),
                        pltpu.VMEM((SUBLANE, S5_NS), f32), pltpu.VMEM((SUBLANE, S5_NS), f32)],
        compiler_params=_params("parallel" if chain else "arbitrary", "arbitrary"),
        name="s5_chain" if chain else "s5_tiles",
    )(h, h, h0r, h0i, p["lam_re"], p["lam_im"], p["log_step"], p["b_re"], p["b_im"], p["c_re"], p["c_im"],
      p["d"], p["w_glu"], p["b_glu"])
    return y, hr.reshape(nseq, S5_NS), hi.reshape(nseq, S5_NS)


def _gla_kernel(q_ref, k_ref, v_ref, gate_ref, glr_ref, wg_ref, bg_ref, ng_ref, s0_ref, acc_ref, y_ref, s_ref,
                st_ref, b_ref, *, tt, c, nj):
    del acc_ref
    j = pl.program_id(1)

    @pl.when(j == 0)
    def _():
        for hd in range(GLA_HEADS):
            st_ref[hd] = s0_ref[0, hd].T

    la = jax.nn.log_sigmoid(_mm(glr_ref[...], wg_ref[...]) + bg_ref[...]) * (1.0 / GLA_GATE_TAU)
    b_ref[...] = _mm_mask_l(jnp.where(_chunk_tril(tt, c), 1.0, 0.0), la)

    rowi = lax.broadcasted_iota(jnp.int32, (c, GLA_DK), 0)
    lanei = lax.broadcasted_iota(jnp.int32, (c, GLA_DK), 1)
    ones = jnp.ones((GLA_DK, GLA_DK), bf16)
    vzero = jnp.zeros((GLA_DK - c, GLA_DV), f32)
    scale = GLA_DK ** -0.5
    heads = range(GLA_HEADS)
    kl = [slice(hd * GLA_DK, (hd + 1) * GLA_DK) for hd in heads]
    vl = [slice(hd * GLA_DV, (hd + 1) * GLA_DV) for hd in heads]

    def chunk(ci, carry):
        rows = pl.ds(pl.multiple_of(ci * c, c), c)
        qh = [q_ref[rows, kl[hd]] * scale for hd in heads]
        kh = [k_ref[rows, kl[hd]] for hd in heads]
        vh = [v_ref[rows, vl[hd]] for hd in heads]
        bh = [b_ref[rows, kl[hd]] for hd in heads]
        prod = [jnp.concatenate(
            [qh[hd] * kh[hd][jj:jj + 1] * jnp.exp(jnp.where(rowi >= jj, bh[hd] - bh[hd][jj:jj + 1], -jnp.inf))
             for jj in range(c)], axis=0) for hd in heads]
        sums = [_mm(prod[hd], ones) for hd in heads]
        att = []
        for hd in heads:
            a = jnp.zeros((c, GLA_DK), f32)
            for jj in range(c):
                a = a + jnp.where(lanei == jj, sums[hd][jj * c:(jj + 1) * c], 0.0)
            att.append(a)
        s_t = [st_ref[hd] for hd in heads]
        o_in = [_mm(att[hd], jnp.concatenate([vh[hd], vzero], axis=0)) for hd in heads]
        o_st = [_mm_nt(qh[hd] * jnp.exp(bh[hd]), s_t[hd]) for hd in heads]
        b_last = [bh[hd][c - 1:c] for hd in heads]
        upd = [_mm_tn(vh[hd], kh[hd] * jnp.exp(b_last[hd] - bh[hd])) for hd in heads]
        for hd in heads:
            st_ref[hd] = s_t[hd] * jnp.exp(b_last[hd]) + upd[hd]
            o = o_in[hd] + o_st[hd]
            on = o * lax.rsqrt(jnp.mean(o * o, axis=-1, keepdims=True) + NORM_EPS) * ng_ref[:, vl[hd]]
            y_ref[rows, vl[hd]] = (on * jax.nn.silu(gate_ref[rows, vl[hd]])).astype(y_ref.dtype)
        return carry

    lax.fori_loop(0, tt // c, chunk, 0, unroll=math.gcd(tt // c, 4))

    @pl.when(j == nj - 1)
    def _():
        for hd in range(GLA_HEADS):
            s_ref[0, hd] = st_ref[hd].T


def _stacked(acc, shape):
    if acc is None:
        return jnp.zeros((SUBLANE, LANE), f32), None
    assert acc.shape == shape
    return acc, 1


def _gla(h, s0, l0, acc, l, depth, p, nseq, t):
    m = nseq * t
    acc_shape = (depth, nseq, GLA_HEADS, GLA_DK, GLA_DV)
    acc, alias_to = _stacked(acc, acc_shape)
    c = min(t, 16)
    tt = min(t, 256)
    nj = t // tt
    row = lambda b, j: b * nj + j
    vec = lambda n: pl.BlockSpec((1, n), lambda b, j: (0, 0))
    st_spec = lambda li: pl.BlockSpec((None, 1, GLA_HEADS, GLA_DK, GLA_DV), lambda b, j: (li, b, 0, 0, 0))
    return pl.pallas_call(
        functools.partial(_gla_kernel, tt=tt, c=c, nj=nj),
        grid=(nseq, nj),
        in_specs=[pl.BlockSpec((tt, GLA_QK), lambda b, j: (row(b, j), COL_G_Q // GLA_QK)),
                  pl.BlockSpec((tt, GLA_QK), lambda b, j: (row(b, j), COL_G_K // GLA_QK)),
                  pl.BlockSpec((tt, GLA_WIDTH), lambda b, j: (row(b, j), COL_G_V // GLA_WIDTH)),
                  pl.BlockSpec((tt, GLA_WIDTH), lambda b, j: (row(b, j), COL_G_GATE // GLA_WIDTH)),
                  pl.BlockSpec((tt, LANE), lambda b, j: (row(b, j), COL_G_LR // LANE)),
                  pl.BlockSpec((LANE, GLA_QK), lambda b, j: (0, 0)), vec(GLA_QK), vec(GLA_WIDTH), st_spec(l0),
                  pl.BlockSpec(memory_space=pl.ANY)],
        out_specs=[pl.BlockSpec((tt, GLA_WIDTH), lambda b, j: (row(b, j), 0)), st_spec(l)],
        out_shape=[jax.ShapeDtypeStruct((m, GLA_WIDTH), _mix_dtype(t)), jax.ShapeDtypeStruct(acc_shape, f32)],
        scratch_shapes=[pltpu.VMEM((GLA_HEADS, GLA_DV, GLA_DK), f32), pltpu.VMEM((tt, GLA_QK), f32)],
        input_output_aliases={} if alias_to is None else {9: alias_to},
        compiler_params=_params("parallel", "arbitrary"),
        name="gla",
    )(h, h, h, h, h, p["w_gate"], p["b_gate"], p["norm_g"], s0, acc)


def _seg_sum(x, e2):
    n = x.shape[0]
    stacked = jnp.concatenate([x[:, i * LANE:(i + 1) * LANE] for i in range(RWKV_PAIRS)], axis=0)
    s = _mm(stacked, e2)
    return jnp.concatenate([s[i * n:(i + 1) * n] for i in range(RWKV_PAIRS)], axis=1)


def _rwkv_kernel(r_ref, k_ref, v_ref, gate_ref, wa_ref, pr_ref, pk_ref, pv_ref, pwa_ref,
                 mur_ref, muk_ref, muv_ref, muwa_ref, w0_ref, w2_ref, a0_ref, a2_ref, kk_ref, ka_ref, rk_ref,
                 lnw_ref, lnb_ref, s0_ref, acc_ref, y_ref, s_ref,
                 st, c_r, c_k, c_v, c_wa, x_kkg, x_rg, x_bi, x_ki, x_bend, x_kend, x_gam, x_v, x_bon,
                 x_tk, x_rq, x_uv, x_yv, x_y, *, tt, c, nj, chain):
    del acc_ref
    j = pl.program_id(1)
    hd = RWKV_HEAD

    if chain:
        @pl.when(j == 0)
        def _():
            st[...] = s0_ref[0]
            c_r[...] = jnp.broadcast_to(pr_ref[0], c_r.shape)
            c_k[...] = jnp.broadcast_to(pk_ref[0], c_k.shape)
            c_v[...] = jnp.broadcast_to(pv_ref[0], c_v.shape)
            c_wa[...] = jnp.broadcast_to(pwa_ref[0], c_wa.shape)
    else:
        @pl.when(j == 0)
        def _():
            st[...] = jnp.zeros(st.shape, f32)

    def shift(x_ref, c_ref, p_ref, mu_ref):
        x = x_ref[...]
        rowid = lax.broadcasted_iota(jnp.int32, x.shape, 0)
        rolled = pltpu.roll(x, 1, 0)
        if chain:
            prev = jnp.where(rowid == 0, c_ref[0:1, :], rolled)
            c_ref[...] = jnp.broadcast_to(x[tt - 1:tt, :], c_ref.shape)
        else:
            prev = jnp.where(rowid % c == 0, p_ref[...], rolled)
        return x + (prev - x) * mu_ref[...]

    r = shift(r_ref, c_r, pr_ref, mur_ref)
    k = shift(k_ref, c_k, pk_ref, muk_ref)
    v = shift(v_ref, c_v, pv_ref, muv_ref)
    wa = shift(wa_ref, c_wa, pwa_ref, muwa_ref)

    w_log = -jax.nn.softplus(-(w0_ref[...] + _mm(jnp.tanh(wa), w2_ref[...]))) - 0.5
    lw = -jnp.exp(w_log)
    a = jax.nn.sigmoid(a0_ref[...] + _mm(wa, a2_ref[...]))

    lane = lax.broadcasted_iota(jnp.int32, (LANE, LANE), 1)
    sub = lax.broadcasted_iota(jnp.int32, (LANE, LANE), 0)
    bd = (lane // RWKV_HEAD) == (sub // RWKV_HEAD)
    e2 = jnp.where(bd, 1.0, 0.0).astype(f32)

    kk = k * kk_ref[...]
    kk = kk / jnp.maximum(jnp.sqrt(_seg_sum(kk * kk, e2)), 1e-12)
    k2 = k * (1.0 + (a - 1.0) * ka_ref[...])
    bv = kk * a
    ri = lax.broadcasted_iota(jnp.int32, (tt, tt), 0)
    qi = lax.broadcasted_iota(jnp.int32, (tt, tt), 1)
    same = (ri // c) == (qi // c)
    strict = same & (qi < ri)
    incl = same & (qi <= ri)
    eye = jnp.where(qi == ri, 1.0, 0.0).astype(f32)
    cl = _mm_mask_l(jnp.where(incl, 1.0, 0.0), lw)
    if c == tt:
        cll = jnp.broadcast_to(cl[tt - 1:tt, :], cl.shape)
    else:
        cll = _mm_mask_l(jnp.where(same, 1.0, 0.0), lw)
    ginv = jnp.exp(-cl)
    gend = jnp.exp(cll - cl)
    x_kkg[...] = kk * jnp.exp(cl - lw)
    x_rg[...] = r * jnp.exp(cl)
    x_bi[...] = bv * ginv
    x_ki[...] = k2 * ginv
    x_bend[...] = bv * gend
    x_kend[...] = k2 * gend
    x_gam[...] = jnp.exp(cll)
    x_v[...] = v
    x_bon[...] = _seg_sum(r * k2 * rk_ref[...], e2) * v

    head_a = lax.broadcasted_iota(jnp.int32, (tt, LANE), 1) < RWKV_HEAD

    def sel(xa, xb):
        return jnp.where(head_a, xa, xb)

    zero_tt = jnp.zeros((tt, tt), f32)

    def tri_inv(ams):
        def bdiag(x):
            return jnp.concatenate([jnp.concatenate([x[:, 0:tt], zero_tt], axis=1),
                                    jnp.concatenate([zero_tt, x[:, tt:2 * tt]], axis=1)], axis=0)

        if tt % LANE:
            ns = [-am for am in ams]
            ts = [eye + n for n in ns]
            if c > 2:
                ns = [_mm(n, n) for n in ns]
                width = 4
                while width < c:
                    both = [_mm(jnp.concatenate([t, n], axis=0), n) for t, n in zip(ts, ns)]
                    ts = [t + b[0:tt] for t, b in zip(ts, both)]
                    ns = [b[tt:2 * tt] for b in both]
                    width *= 2
                ts = [t + _mm(t, n) for t, n in zip(ts, ns)]
            return ts
        ns = [jnp.concatenate([-ams[i], -ams[i + 1]], axis=1) for i in range(0, len(ams), 2)]
        eye2 = jnp.concatenate([eye, eye], axis=1)
        ts = [eye2 + n for n in ns]
        if c > 2:
            ns = [_mm(n, bdiag(n)) for n in ns]
            width = 4
            while width < c:
                both = [_mm(jnp.concatenate([t, n], axis=0), bdiag(n)) for t, n in zip(ts, ns)]
                ts = [t + b[0:tt] for t, b in zip(ts, both)]
                ns = [b[tt:2 * tt] for b in both]
                width *= 2
            last = [_mm(t, bdiag(n)) for t, n in zip(ts, ns)]
            ts = [t + m for t, m in zip(ts, last)]
        return [t[:, hh * tt:(hh + 1) * tt] for t in ts for hh in (0, 1)]

    def lanes(p):
        return slice(p * LANE, (p + 1) * LANE)

    for g0 in range(0, RWKV_PAIRS, RWKV_GROUP):
        grp = list(range(g0, g0 + RWKV_GROUP))
        kkg = [x_kkg[:, lanes(p)] for p in grp]
        rg = [x_rg[:, lanes(p)] for p in grp]
        vp = [x_v[:, lanes(p)] for p in grp]
        lhs = [jnp.concatenate([jnp.where(head_a, a, 0.0), jnp.where(head_a, 0.0, a),
                                jnp.where(head_a, b, 0.0), jnp.where(head_a, 0.0, b)], axis=0)
               for a, b in zip(kkg, rg)]
        if tt % LANE == 0:
            pbk = [_mm_nt(x, jnp.concatenate([x_bi[:, lanes(p)], x_ki[:, lanes(p)]], axis=0))
                   for x, p in zip(lhs, grp)]
            pb = [x[:, 0:tt] for x in pbk]
            pk = [x[:, tt:2 * tt] for x in pbk]
        else:
            pb = [_mm_nt(x, x_bi[:, lanes(p)]) for x, p in zip(lhs, grp)]
            pk = [_mm_nt(x, x_ki[:, lanes(p)]) for x, p in zip(lhs, grp)]
        tinv = tri_inv([jnp.where(strict, x[hh * tt:(hh + 1) * tt], 0.0) for x in pb for hh in (0, 1)])
        ar = [jnp.where(incl, x[hh * tt:(hh + 1) * tt], 0.0) for x in pb for hh in (2, 3)]
        bmr = [jnp.concatenate([jnp.where(strict, x[hh * tt:(hh + 1) * tt], 0.0),
                                jnp.where(incl, x[(hh + 2) * tt:(hh + 3) * tt], 0.0)], axis=0)
               for x in pk for hh in (0, 1)]
        mv = [_mm(x, vp[i // 2]) for i, x in enumerate(bmr)]
        bmv = [sel(mv[2 * i][0:tt], mv[2 * i + 1][0:tt]) for i in range(RWKV_GROUP)]
        brv = [sel(mv[2 * i][tt:2 * tt], mv[2 * i + 1][tt:2 * tt]) for i in range(RWKV_GROUP)]
        rhs_t = [jnp.concatenate([a, b], axis=1) for a, b in zip(bmv, kkg)]
        mt = [_mm(x, rhs_t[i // 2]) for i, x in enumerate(tinv)]
        uv = [-sel(mt[2 * i][:, 0:LANE], mt[2 * i + 1][:, 0:LANE]) for i in range(RWKV_GROUP)]
        tk = [sel(mt[2 * i][:, LANE:2 * LANE], mt[2 * i + 1][:, LANE:2 * LANE]) for i in range(RWKV_GROUP)]
        rhs_r = [jnp.concatenate([a, b], axis=1) for a, b in zip(uv, tk)]
        mr = [_mm(x, rhs_r[i // 2]) for i, x in enumerate(ar)]
        for i, p in enumerate(grp):
            x_uv[:, lanes(p)] = uv[i]
            x_tk[:, lanes(p)] = tk[i]
            x_yv[:, lanes(p)] = sel(mr[2 * i][:, 0:LANE], mr[2 * i + 1][:, 0:LANE]) + brv[i]
            x_rq[:, lanes(p)] = rg[i] - sel(mr[2 * i][:, LANE:2 * LANE], mr[2 * i + 1][:, LANE:2 * LANE])

    def chunk(ci, carry):
        row0 = pl.multiple_of(ci * c, c)
        rows = pl.ds(row0, c)
        pairs = range(RWKV_PAIRS)
        if not chain:
            for p in pairs:
                st[p, 0:hd, 0:hd] = s0_ref[ci, 2 * p]
                st[p, hd:2 * hd, hd:2 * hd] = s0_ref[ci, 2 * p + 1]
        s = [st[p] for p in pairs]
        zy = [_mm_nt(jnp.concatenate([x_tk[rows, lanes(p)], x_rq[rows, lanes(p)]], axis=0), s[p]) for p in pairs]
        u = [x_uv[rows, lanes(p)] - zy[p][0:c] for p in pairs]
        for p in pairs:
            x_y[rows, lanes(p)] = zy[p][c:2 * c] + x_yv[rows, lanes(p)]
        upd = [_mm_tn(jnp.concatenate([u[p], x_v[rows, lanes(p)]], axis=0),
                      jnp.concatenate([x_bend[rows, lanes(p)], x_kend[rows, lanes(p)]], axis=0)) for p in pairs]
        for p in pairs:
            s_new = s[p] * x_gam[pl.ds(row0, 1), lanes(p)] + jnp.where(bd, upd[p], 0.0)
            if chain:
                st[p] = s_new
            else:
                s_ref[ci, 2 * p] = s_new[0:hd, 0:hd]
                s_ref[ci, 2 * p + 1] = s_new[hd:2 * hd, hd:2 * hd]
        return carry

    lax.fori_loop(0, tt // c, chunk, 0)

    y = x_y[...]
    mean = _seg_sum(y, e2) * (1.0 / RWKV_HEAD)
    yc = y - mean
    var = _seg_sum(yc * yc, e2) * (1.0 / RWKV_HEAD)
    yn = yc * lax.rsqrt(var + RWKV_GN_EPS) * lnw_ref[...] + lnb_ref[...] + x_bon[...]
    y_ref[...] = (yn * jax.nn.silu(gate_ref[...])).astype(y_ref.dtype)

    if chain:
        @pl.when(j == nj - 1)
        def _():
            s_ref[0] = st[...]


def _rwkv(h, prev, s0, l0, acc, l, depth, p, nseq, t, chain):
    m = nseq * t
    w = RWKV_WIDTH
    acc_shape = (depth, nseq) + ((RWKV_PAIRS, LANE, LANE) if chain else (RWKV_HEADS, RWKV_HEAD, RWKV_HEAD))
    acc, alias_to = _stacked(acc, acc_shape)
    if chain:
        tt = min(t, 128)
        c = tt
        nj = t // tt
        grid = (nseq, nj)
        row = lambda b, j: b * nj + j
        prev = tuple(x[:, None, :] for x in prev)
        prow = lambda n: pl.BlockSpec((1, 1, n), lambda b, j: (b, 0, 0))
        st_spec = lambda li: pl.BlockSpec((None, 1, RWKV_PAIRS, LANE, LANE), lambda b, j: (li, b, 0, 0, 0))
    else:
        c = t
        tt = 64
        nj = m // tt
        grid = (1, nj)
        row = lambda b, j: j
        prev = tuple(jnp.repeat(x, t, axis=0) for x in prev)
        prow = lambda n: pl.BlockSpec((tt, n), lambda b, j: (j, 0))
        st_spec = lambda li: pl.BlockSpec((None, tt // c, RWKV_HEADS, RWKV_HEAD, RWKV_HEAD),
                                          lambda b, j: (li, j, 0, 0, 0))
    vec = lambda n: pl.BlockSpec((1, n), lambda b, j: (0, 0))
    tok = lambda col: pl.BlockSpec((tt, w), lambda b, j: (row(b, j), col // w))
    big = lambda: pltpu.VMEM((tt, w), f32)
    return pl.pallas_call(
        functools.partial(_rwkv_kernel, tt=tt, c=c, nj=nj, chain=chain),
        grid=grid,
        in_specs=[tok(COL_RW_R), tok(COL_RW_K), tok(COL_RW_V), tok(COL_RW_GATE),
                  pl.BlockSpec((tt, LANE), lambda b, j: (row(b, j), COL_RW_WA // LANE)),
                  prow(w), prow(w), prow(w), prow(LANE),
                  vec(w), vec(w), vec(w), vec(LANE),
                  vec(w), pl.BlockSpec((LANE, w), lambda b, j: (0, 0)), vec(w),
                  pl.BlockSpec((LANE, w), lambda b, j: (0, 0)), vec(w), vec(w), vec(w), vec(w), vec(w), st_spec(l0),
                  pl.BlockSpec(memory_space=pl.ANY)],
        out_specs=[pl.BlockSpec((tt, w), lambda b, j: (row(b, j), 0)), st_spec(l)],
        out_shape=[jax.ShapeDtypeStruct((m, w), _mix_dtype(t)), jax.ShapeDtypeStruct(acc_shape, f32)],
        input_output_aliases={} if alias_to is None else {23: alias_to},
        scratch_shapes=[pltpu.VMEM((RWKV_PAIRS, LANE, LANE), f32),
                        pltpu.VMEM((SUBLANE, w), f32), pltpu.VMEM((SUBLANE, w), f32),
                        pltpu.VMEM((SUBLANE, w), f32), pltpu.VMEM((SUBLANE, LANE), f32)] + [big() for _ in range(14)],
        compiler_params=_params("parallel" if chain else "arbitrary", "arbitrary"),
        name="rwkv_chain" if chain else "rwkv_tiles",
    )(h, h, h, h, h, *prev, p["mu_r"], p["mu_k"], p["mu_v"], p["mu_wa"], p["w0"], p["w2"], p["a0"], p["a2"],
      p["k_k"], p["k_a"], p["r_k"], p["ln_w"], p["ln_b"], s0, acc)


def _xattn_kernel(q_ref, k_ref, v_ref, o_ref):
    scale = XA_HEAD_DIM ** -0.5
    heads = range(XA_HEADS)
    ln = [slice(hd * XA_HEAD_DIM, (hd + 1) * XA_HEAD_DIM) for hd in heads]
    rows = [pl.ds(hd, N_MEM, stride=XA_HEADS) for hd in heads]
    s = [_mm_nt(q_ref[:, ln[hd]] * scale, k_ref[0, rows[hd], :]) for hd in heads]
    s = [x - jnp.max(x, axis=-1, keepdims=True) for x in s]
    e = [jnp.exp(x) for x in s]
    pr = [x / jnp.sum(x, axis=-1, keepdims=True) for x in e]
    o = [_mm(pr[hd], v_ref[0, rows[hd], :]) for hd in heads]
    for hd in heads:
        o_ref[:, ln[hd]] = o[hd].astype(o_ref.dtype)


def _xattn(h, mem_k, mem_v, nseq, t):
    m = nseq * t
    tq = min(t, 512)
    nj = t // tq
    mem_spec = pl.BlockSpec((1, N_MEM * XA_HEADS, XA_HEAD_DIM), lambda b, j: (b, 0, 0))
    return pl.pallas_call(
        _xattn_kernel,
        grid=(nseq, nj),
        in_specs=[pl.BlockSpec((tq, XA_WIDTH), lambda b, j: (b * nj + j, COL_XA_Q // XA_WIDTH)), mem_spec, mem_spec],
        out_specs=pl.BlockSpec((tq, XA_WIDTH), lambda b, j: (b * nj + j, 0)),
        out_shape=jax.ShapeDtypeStruct((m, XA_WIDTH), _mix_dtype(t)),
        compiler_params=_params("parallel", "arbitrary"),
        name="xattn",
    )(h, mem_k, mem_v)


_RW0 = 5136
W_IN_SEGMENTS = ((_RW0, _RW0 + 1536), (_RW0 + 1536, _RW0 + 3072), (_RW0 + 3072, _RW0 + 4608), (9872, 11408),
                 (0, 1024), (1024, 2048), (3072, 4096), (4096, 5120), (2048, 2560), (2560, 3072), (11408, 11920),
                 (_RW0 + 4608, _RW0 + 4736), (5120, 5136))


def _reorder_cast_kernel(w_ref, o_ref):
    dst = 0
    for a, b in W_IN_SEGMENTS:
        o_ref[dst:dst + (b - a), :] = w_ref[a:b, :].astype(bf16)
        dst += b - a
    o_ref[dst:, :] = jnp.zeros((H_COLS - dst, o_ref.shape[1]), bf16)


def _prep_w_in(w_in, tk=256):
    depth, d, n = w_in.shape
    return pl.pallas_call(
        _reorder_cast_kernel,
        grid=(depth, d // tk),
        in_specs=[pl.BlockSpec((None, n, tk), lambda l, i: (l, 0, i))],
        out_specs=pl.BlockSpec((None, H_COLS, tk), lambda l, i: (l, 0, i)),
        out_shape=jax.ShapeDtypeStruct((depth, H_COLS, d), bf16),
        compiler_params=_params("parallel", "parallel"),
        name="w_in_layout",
    )(jnp.swapaxes(w_in, 1, 2))


def _block_diag_s5(b, c):
    gl = S5_GROUPS // S5_BLOCKS
    eye = jnp.eye(gl, dtype=b.dtype)
    bb = b.reshape(S5_BLOCKS, gl, S5_STATE, S5_GROUP)
    bbd = jnp.einsum("kgpn,gh->kgnhp", bb, eye).reshape(S5_BLOCKS, gl * S5_GROUP, gl * S5_STATE)
    cc = c.reshape(S5_BLOCKS, gl, S5_GROUP, S5_STATE)
    cbd = jnp.einsum("kgnp,gh->kgphn", cc, eye).reshape(S5_BLOCKS, gl * S5_STATE, gl * S5_GROUP)
    return bbd, cbd


def _layer_params(l, a):
    row = lambda x: x.reshape(1, -1)
    bre, cre = _block_diag_s5(a["s5_b_re"][l], a["s5_c_re"][l])
    bim, cim = _block_diag_s5(a["s5_b_im"][l], a["s5_c_im"][l])
    s5 = dict(lam_re=row(a["s5_lam_re"][l]), lam_im=row(a["s5_lam_im"][l]),
              log_step=row(jnp.broadcast_to(a["s5_log_step"][l][:, None], (S5_GROUPS, S5_STATE))),
              b_re=bre, b_im=bim, c_re=cre.astype(bf16), c_im=cim.astype(bf16),
              d=row(a["s5_d"][l]), w_glu=a["s5_w_glu"][l].astype(bf16), b_glu=row(a["s5_b_glu"][l]))
    wg = jnp.zeros((LANE, GLA_QK), f32).at[:GLA_GATE_RANK].set(a["gla_w_gate"][l])
    gla = dict(w_gate=wg, b_gate=row(a["gla_b_gate"][l]), norm_g=row(a["gla_norm_g"][l]))
    mu = a["rw_mu"][l]
    w = RWKV_WIDTH
    z64 = jnp.zeros((64, w), f32)
    rw = dict(mu_r=row(mu[0:w]), mu_k=row(mu[w:2 * w]), mu_v=row(mu[2 * w:3 * w]), mu_wa=row(mu[3 * w:]),
              w0=row(a["rw_w0"][l]), w2=jnp.concatenate([a["rw_w2"][l], z64], 0).astype(bf16),
              a0=row(a["rw_a0"][l]), a2=jnp.concatenate([z64, a["rw_a2"][l]], 0).astype(bf16),
              k_k=row(a["rw_k_k"][l]), k_a=row(a["rw_k_a"][l]), r_k=row(a["rw_r_k"][l]),
              ln_w=row(a["rw_ln_w"][l]), ln_b=row(a["rw_ln_b"][l]))
    return s5, gla, rw


def _unpack_rwkv_state(s):
    b = s.shape[0]
    s = s.reshape(b, RWKV_PAIRS, 2, RWKV_HEAD, 2, RWKV_HEAD)
    return jnp.stack([s[:, :, 0, :, 0, :], s[:, :, 1, :, 1, :]], axis=2).reshape(b, RWKV_HEADS, RWKV_HEAD, RWKV_HEAD)


def _split_shift(x):
    w = RWKV_WIDTH
    return tuple(x[:, a:b] for a, b in ((0, w), (w, 2 * w), (2 * w, 3 * w), (3 * w, 3 * w + LANE)))


def _mixer_layer(x, nseq, t, mem_k, mem_v, h0r, h0i, s_gla, s_rw, l0, acc_gla, acc_rw, l, depth, prev, chain, lp,
                 w_in_t, w_out_b, g_pre, g_post):
    s5p, glap, rwp = lp
    h = _norm_mm(x, g_pre, w_in_t, l)
    y_s5, hr, hi = _s5(h, h0r, h0i, s5p, nseq, t, chain)
    y_gla, s_gla_new = _gla(h, s_gla, l0, acc_gla, l, depth, glap, nseq, t)
    y_rw, s_rw_new = _rwkv(h, prev, s_rw, l0, acc_rw, l, depth, rwp, nseq, t, chain)
    y_xa = _xattn(h, mem_k, mem_v, nseq, t)
    x_new = _outproj((y_s5, y_gla, y_rw, y_xa), w_out_b, l, g_post, x)
    last = h.reshape(nseq, t, H_COLS)[:, t - 1]
    w = RWKV_WIDTH
    shift = jnp.concatenate([last[:, COL_RW_R:COL_RW_R + w], last[:, COL_RW_K:COL_RW_K + w],
                             last[:, COL_RW_V:COL_RW_V + w], last[:, COL_RW_WA:COL_RW_WA + LANE]], axis=1)
    return x_new, jnp.stack([hr, hi], axis=-1).reshape(nseq, S5_GROUPS, S5_STATE, 2), s_gla_new, s_rw_new, shift


def kernel(x_prompt, x_sample, mem_prompt, cache_mem_k, cache_mem_v, state_s5, state_gla, state_rwkv, state_rwkv_shift, g_pre, g_post, w_in, w_out, s5_lam_re, s5_lam_im, s5_log_step, s5_b_re, s5_b_im, s5_c_re, s5_c_im, s5_d, s5_w_glu, s5_b_glu, gla_w_gate, gla_b_gate, gla_norm_g, rw_mu, rw_w0, rw_w2, rw_a0, rw_a2, rw_k_k, rw_k_a, rw_r_k, rw_ln_w, rw_ln_b, mem_g, w_mk, w_mv):
    raw = dict(s5_lam_re=s5_lam_re, s5_lam_im=s5_lam_im, s5_log_step=s5_log_step, s5_b_re=s5_b_re, s5_b_im=s5_b_im,
               s5_c_re=s5_c_re, s5_c_im=s5_c_im, s5_d=s5_d, s5_w_glu=s5_w_glu, s5_b_glu=s5_b_glu,
               gla_w_gate=gla_w_gate, gla_b_gate=gla_b_gate, gla_norm_g=gla_norm_g, rw_mu=rw_mu, rw_w0=rw_w0,
               rw_w2=rw_w2, rw_a0=rw_a0, rw_a2=rw_a2, rw_k_k=rw_k_k, rw_k_a=rw_k_a, rw_r_k=rw_r_k,
               rw_ln_w=rw_ln_w, rw_ln_b=rw_ln_b)
    bp, tp, d = x_prompt.shape
    bs, ts, _ = x_sample.shape
    depth = w_in.shape[0]
    w_in_t = _prep_w_in(w_in)
    w_out_b = w_out.astype(bf16)
    w_kv_t = jnp.swapaxes(jnp.concatenate([w_mk, w_mv], axis=2), 1, 2).astype(bf16)
    mem_rows = N_MEM * XA_HEADS

    xp = x_prompt.reshape(bp * tp, d)
    xs = x_sample.reshape(bs * ts, d)
    mem2d = mem_prompt.reshape(bp * N_MEM, d)
    zp = lambda *s: jnp.zeros(s, f32)
    prev_p = _split_shift(zp(bp, RWKV_SHIFT_COLS))

    gla_p = gla_s = rw_p = rw_s = None
    zero_gla = zp(1, bp, GLA_HEADS, GLA_DK, GLA_DV)
    zero_rw = zp(1, bp, RWKV_PAIRS, LANE, LANE)

    outs = [[] for _ in range(6)]
    for l in range(depth):
        lp = _layer_params(l, raw)
        gp, gq = g_pre[l].reshape(1, d), g_post[l].reshape(1, d)
        kv = _norm_mm(mem2d, mem_g[l].reshape(1, d), w_kv_t, l)
        k_mem = kv[:, :XA_WIDTH].reshape(bp, N_MEM, XA_WIDTH)
        v_mem = kv[:, XA_WIDTH:].reshape(bp, N_MEM, XA_WIDTH)
        xp, h1, gla_p, rw_p, p1 = _mixer_layer(
            xp, bp, tp, k_mem.reshape(bp, mem_rows, XA_HEAD_DIM), v_mem.reshape(bp, mem_rows, XA_HEAD_DIM),
            zp(bp, S5_NS), zp(bp, S5_NS), zero_gla, zero_rw, 0, gla_p, rw_p, l, depth,
            prev_p, True, lp, w_in_t, w_out_b, gp, gq)
        st5 = state_s5[l].reshape(bs, S5_NS, 2)
        xs, h2, gla_s, rw_s, p2 = _mixer_layer(
            xs, bs, ts, cache_mem_k[l].reshape(bs, mem_rows, XA_HEAD_DIM),
            cache_mem_v[l].reshape(bs, mem_rows, XA_HEAD_DIM),
            st5[..., 0], st5[..., 1], state_gla, state_rwkv, l, gla_s, rw_s, l, depth,
            _split_shift(state_rwkv_shift[l]), False, lp, w_in_t, w_out_b, gp, gq)
        vals = (k_mem.reshape(bp, N_MEM, XA_HEADS, XA_HEAD_DIM), v_mem.reshape(bp, N_MEM, XA_HEADS, XA_HEAD_DIM),
                h1, h2, p1, p2)
        for o, val in zip(outs, vals):
            o.append(val)

    mk, mv, s5_p, s5_s, sh_p, sh_s = (jnp.stack(o) for o in outs)
    rw_p_heads = _unpack_rwkv_state(rw_p.reshape(depth * bp, RWKV_PAIRS, LANE, LANE))
    rw_p_heads = rw_p_heads.reshape(depth, bp, RWKV_HEADS, RWKV_HEAD, RWKV_HEAD)
    return (xp.reshape(bp, tp, d), xs.reshape(bs, ts, d), mk, mv, s5_p, s5_s, gla_p, gla_s, rw_p_heads, rw_s,
            sh_p, sh_s)
```

```python
import functools
import math

import jax
import jax.numpy as jnp
from jax import lax
from jax.experimental import pallas as pl
from jax.experimental.pallas import tpu as pltpu

f32 = jnp.float32
bf16 = jnp.bfloat16

D_MODEL = 4096
DEPTH = 4
N_MEM = 256
NORM_EPS = 1e-6

S5_WIDTH = 1024
S5_GROUP = 16
S5_GROUPS = 64
S5_STATE = 64
S5_NS = S5_GROUPS * S5_STATE
S5_BLOCKS = 8
S5_LANES = 512

GLA_WIDTH = 1024
GLA_HEADS = 4
GLA_DV = 256
GLA_DK = 128
GLA_QK = 512
GLA_GATE_RANK = 16
GLA_GATE_TAU = 16.0

RWKV_WIDTH = 1536
RWKV_HEAD = 64
RWKV_HEADS = 24
RWKV_PAIRS = 12
RWKV_GROUP = 12
RWKV_GN_EPS = RWKV_HEAD * 1e-5
RWKV_SHIFT_COLS = 3 * RWKV_WIDTH + 128

XA_HEADS = 4
XA_HEAD_DIM = 128
XA_WIDTH = 512

LANE = 128
SUBLANE = 8
VMEM_LIMIT = 56 * 1024 * 1024

H_COLS = 12288
COL_RW_R, COL_RW_K, COL_RW_V, COL_RW_GATE = 0, 1536, 3072, 4608
COL_S5_U, COL_S5_GATE, COL_G_V, COL_G_GATE = 6144, 7168, 8192, 9216
COL_G_Q, COL_G_K, COL_XA_Q = 10240, 10752, 11264
COL_RW_WA, COL_G_LR = 11776, 11904


def _mm(a, b):
    return jnp.dot(a.astype(bf16), b.astype(bf16), preferred_element_type=f32)


def _mm_nt(a, b):
    return lax.dot_general(a.astype(bf16), b.astype(bf16), (((1,), (1,)), ((), ())), preferred_element_type=f32)


def _mm_tn(a, b):
    return lax.dot_general(a.astype(bf16), b.astype(bf16), (((0,), (0,)), ((), ())), preferred_element_type=f32)


def _split2(x):
    hi = x.astype(bf16)
    return hi, (x - hi.astype(f32)).astype(bf16)


def _mm_mask_l(mask01, x):
    hi, lo = _split2(x)
    m = mask01.astype(bf16)
    return jnp.dot(m, hi, preferred_element_type=f32) + jnp.dot(m, lo, preferred_element_type=f32)


def _mix_dtype(t):
    return bf16 if t % (2 * SUBLANE) == 0 else f32


def _params(*sem):
    return pltpu.CompilerParams(dimension_semantics=sem, vmem_limit_bytes=VMEM_LIMIT)


def _chunk_tril(n, c, strict=False):
    r = lax.broadcasted_iota(jnp.int32, (n, n), 0)
    q = lax.broadcasted_iota(jnp.int32, (n, n), 1)
    same = (r // c) == (q // c)
    return same & ((q < r) if strict else (q <= r))


def _norm_mm_kernel(x_ref, g_ref, w_ref, o_ref, xn_ref):
    @pl.when(pl.program_id(1) == 0)
    def _():
        rows = 256
        for r0 in range(0, x_ref.shape[0], rows):
            x = x_ref[r0:r0 + rows, :]
            ms = jnp.mean(x * x, axis=-1, keepdims=True)
            xn_ref[r0:r0 + rows, :] = (x * lax.rsqrt(ms + NORM_EPS) * g_ref[...]).astype(bf16)

    o_ref[...] = lax.dot_general(xn_ref[...], w_ref[...], (((1,), (1,)), ((), ())), preferred_element_type=f32)


def _norm_mm(x, g, wt, l):
    m, d = x.shape
    n = wt.shape[1]
    tm = min(m, 1024)
    tn = 768 if n % 768 == 0 else 512
    xmode = dict(pipeline_mode=pl.Buffered(1))
    return pl.pallas_call(
        _norm_mm_kernel,
        grid=(m // tm, n // tn),
        in_specs=[pl.BlockSpec((tm, d), lambda i, j: (i, 0), **xmode),
                  pl.BlockSpec((1, d), lambda i, j: (0, 0)),
                  pl.BlockSpec((None, tn, d), lambda i, j: (l, j, 0))],
        out_specs=pl.BlockSpec((tm, tn), lambda i, j: (i, j)),
        out_shape=jax.ShapeDtypeStruct((m, n), f32),
        scratch_shapes=[pltpu.VMEM((tm, d), bf16)],
        compiler_params=_params("parallel", "arbitrary"),
        name="norm_mm",
    )(x, g, wt)


def _proj_kernel(xn_ref, w_ref, o_ref):
    o_ref[...] = lax.dot_general(xn_ref[...], w_ref[...], (((1,), (1,)), ((), ())), preferred_element_type=f32)


def _proj(xn, wt, l):
    m, d = xn.shape
    n = wt.shape[1]
    tm = min(m, 1024)
    tn = 768 if n % 768 == 0 else 512
    return pl.pallas_call(
        _proj_kernel,
        grid=(m // tm, n // tn),
        in_specs=[pl.BlockSpec((tm, d), lambda i, j: (i, 0)),
                  pl.BlockSpec((None, tn, d), lambda i, j: (l, j, 0))],
        out_specs=pl.BlockSpec((tm, tn), lambda i, j: (i, j)),
        out_shape=jax.ShapeDtypeStruct((m, n), f32),
        compiler_params=_params("parallel", "arbitrary"),
        name="proj",
    )(xn, wt)


def _outproj_kernel(a1, a2, a3, a4, w_ref, g_ref, gn_ref, x_ref, o_ref, *xn_ref, nj, tn):
    j = pl.program_id(1)
    acc = None
    k0 = 0
    for a in (a1, a2, a3, a4):
        part = _mm(a[...], w_ref[k0:k0 + a.shape[1], :])
        acc = part if acc is None else acc + part
        k0 += a.shape[1]
    o_ref[:, pl.ds(pl.multiple_of(j * tn, tn), tn)] = acc

    @pl.when(j == nj - 1)
    def _():
        cols = [slice(k * tn, (k + 1) * tn) for k in range(nj)]
        ssq = None
        for sl in cols:
            p = o_ref[:, sl]
            s = jnp.sum(p * p, axis=-1, keepdims=True)
            ssq = s if ssq is None else ssq + s
        inv = lax.rsqrt(ssq * (1.0 / D_MODEL) + NORM_EPS)
        ssq = None
        for sl in cols:
            new = x_ref[:, sl] + o_ref[:, sl] * inv * g_ref[:, sl]
            o_ref[:, sl] = new
            s = jnp.sum(new * new, axis=-1, keepdims=True)
            ssq = s if ssq is None else ssq + s
        if xn_ref:
            inv = lax.rsqrt(ssq * (1.0 / D_MODEL) + NORM_EPS)
            for sl in cols:
                xn_ref[0][:, sl] = (o_ref[:, sl] * inv * gn_ref[:, sl]).astype(bf16)


def _outproj(parts, w, l, g, x, g_next, tn=512):
    m, d = x.shape
    nj = d // tn
    tm = 512 if parts[0].dtype == bf16 else 256
    emit = g_next is not None
    row_spec = pl.BlockSpec((tm, d), lambda i, j: (i, 0))
    in_specs = [pl.BlockSpec((tm, p.shape[1]), lambda i, j: (i, 0)) for p in parts]
    in_specs += [pl.BlockSpec((None, d, tn), lambda i, j: (l, 0, j))]
    in_specs += [pl.BlockSpec((1, d), lambda i, j: (0, 0)), pl.BlockSpec((1, d), lambda i, j: (0, 0)),
                 pl.BlockSpec((tm, d), lambda i, j: (i, 0), pipeline_mode=pl.Buffered(1))]
    out = pl.pallas_call(
        functools.partial(_outproj_kernel, nj=nj, tn=tn),
        grid=(m // tm, nj),
        in_specs=in_specs,
        out_specs=[row_spec, row_spec] if emit else [row_spec],
        out_shape=[jax.ShapeDtypeStruct((m, d), f32)] + ([jax.ShapeDtypeStruct((m, d), bf16)] if emit else []),
        compiler_params=_params("parallel", "arbitrary"),
        name="outproj",
    )(*parts, w, g, g_next if emit else g, x)
    return (out[0], out[1]) if emit else (out[0], None)


def _s5_kernel(u_ref, gate_ref, h0_ref, lre_ref, lim_ref, step_ref, bre_ref, bim_ref, cre_ref, cim_ref,
               d_ref, wglu_ref, bglu_ref, y_ref, hl_ref, sre, sim, car, cai, *, tt, nj, chain):
    j = pl.program_id(1)
    nrt = tt // SUBLANE
    lr = lre_ref[...]
    li = lim_ref[...]
    st = jnp.exp(step_ref[...])
    kidx = (lax.broadcasted_iota(jnp.int32, (SUBLANE, S5_NS), 0) + 1).astype(f32)
    mag = jnp.exp(kidx * (lr * st))
    ang = kidx * (li * st)
    pw_re = mag * jnp.cos(ang)
    pw_im = mag * jnp.sin(ang)
    ab_re = pw_re[0:1]
    ab_im = pw_im[0:1]
    den = lr * lr + li * li
    f_re = ((ab_re - 1.0) * lr + ab_im * li) / den
    f_im = (ab_im * lr - (ab_re - 1.0) * li) / den

    u = u_ref[...]
    for blk in range(S5_BLOCKS):
        cols = slice(blk * S5_LANES, (blk + 1) * S5_LANES)
        fr = f_re[:, cols]
        fi = f_im[:, cols]
        b_re = bre_ref[blk]
        b_im = bim_ref[blk]
        ub = u[:, blk * LANE:(blk + 1) * LANE].astype(bf16)
        sre[:, cols] = _mm(ub, fr * b_re - fi * b_im)
        sim[:, cols] = _mm(ub, fr * b_im + fi * b_re)

    @pl.when(j == 0)
    def _():
        if chain:
            car[...] = jnp.broadcast_to(h0_ref[0, 0:1], (SUBLANE, S5_NS))
            cai[...] = jnp.broadcast_to(h0_ref[0, 1:2], (SUBLANE, S5_NS))
        else:
            car[...] = h0_ref[pl.ds(0, S5_NS, stride=2), :].T
            cai[...] = h0_ref[pl.ds(1, S5_NS, stride=2), :].T

    rowi = lax.broadcasted_iota(jnp.int32, (SUBLANE, S5_LANES), 0)
    for ch in range(S5_NS // S5_LANES):
        cols = slice(ch * S5_LANES, (ch + 1) * S5_LANES)
        p_re = pw_re[:, cols]
        p_im = pw_im[:, cols]
        steps = [(d, jnp.where(rowi >= d, p_re[d - 1:d], 0.0), jnp.where(rowi >= d, p_im[d - 1:d], 0.0))
                 for d in (1, 2, 4)]

        def body(r, carry, cols=cols, p_re=p_re, p_im=p_im, steps=steps):
            rows = pl.ds(pl.multiple_of(r * SUBLANE, SUBLANE), SUBLANE)
            xr = sre[rows, cols]
            xi = sim[rows, cols]
            for d, ar, ai in steps:
                sr = pltpu.roll(xr, d, 0)
                si = pltpu.roll(xi, d, 0)
                xr, xi = xr + ar * sr - ai * si, xi + ar * si + ai * sr
            if chain:
                cr, ci = carry
            else:
                seq = pl.ds(j * nrt + r, 1)
                cr = jnp.broadcast_to(car[seq, cols], (SUBLANE, S5_LANES))
                ci = jnp.broadcast_to(cai[seq, cols], (SUBLANE, S5_LANES))
            xr, xi = xr + p_re * cr - p_im * ci, xi + p_re * ci + p_im * cr
            sre[rows, cols] = xr
            sim[rows, cols] = xi
            if chain:
                return (jnp.broadcast_to(xr[SUBLANE - 1:SUBLANE], (SUBLANE, S5_LANES)),
                        jnp.broadcast_to(xi[SUBLANE - 1:SUBLANE], (SUBLANE, S5_LANES)))
            car[seq, cols] = xr[SUBLANE - 1:SUBLANE]
            cai[seq, cols] = xi[SUBLANE - 1:SUBLANE]
            return carry

        if chain:
            cr, ci = lax.fori_loop(0, nrt, body, (car[:, cols], cai[:, cols]))
            car[:, cols] = cr
            cai[:, cols] = ci
        else:
            lax.fori_loop(0, nrt, body, 0)

    if chain:
        hl_ref[0, 0:1] = car[0:1, :]
        hl_ref[0, 1:2] = cai[0:1, :]
    else:
        @pl.when(j == nj - 1)
        def _():
            hl_ref[pl.ds(0, S5_NS, stride=2), :] = car[...].T
            hl_ref[pl.ds(1, S5_NS, stride=2), :] = cai[...].T

    ys = []
    for blk in range(S5_BLOCKS):
        cols = slice(blk * S5_LANES, (blk + 1) * S5_LANES)
        ys.append(_mm(sre[:, cols], cre_ref[blk]) - _mm(sim[:, cols], cim_ref[blk]))
    y = jnp.concatenate(ys, axis=1) + d_ref[...] * u
    z = jax.nn.gelu(y)
    gl = jax.nn.sigmoid(_mm(z, wglu_ref[...]) + bglu_ref[...])
    y_ref[...] = (z * gl * jax.nn.silu(gate_ref[...])).astype(y_ref.dtype)


def _s5(h, h0, l0, p, nseq, t, chain):
    m = nseq * t
    if chain:
        tt = min(t, 256)
        nj = t // tt
        grid = (nseq, nj)
        row = lambda b, j: b * nj + j
        h_in = pl.BlockSpec((None, 1, 2, S5_NS), lambda b, j: (l0, b, 0, 0))
        h_out = pl.BlockSpec((1, 2, S5_NS), lambda b, j: (b, 0, 0))
        h_shape = jax.ShapeDtypeStruct((nseq, 2, S5_NS), f32)
        carry_rows = SUBLANE
    else:
        assert t == SUBLANE and nseq == LANE
        tt = 64
        nj = m // tt
        grid = (1, nj)
        row = lambda b, j: j
        h_in = pl.BlockSpec((None, 2 * S5_NS, nseq), lambda b, j: (l0, 0, 0))
        h_out = pl.BlockSpec((2 * S5_NS, nseq), lambda b, j: (0, 0))
        h_shape = jax.ShapeDtypeStruct((2 * S5_NS, nseq), f32)
        carry_rows = nseq
    cu, cg = COL_S5_U // S5_WIDTH, COL_S5_GATE // S5_WIDTH
    vec = lambda n: pl.BlockSpec((1, n), lambda b, j: (0, 0))
    full3 = lambda s: pl.BlockSpec(s, lambda b, j: (0, 0, 0))
    return pl.pallas_call(
        functools.partial(_s5_kernel, tt=tt, nj=nj, chain=chain),
        grid=grid,
        in_specs=[pl.BlockSpec((tt, S5_WIDTH), lambda b, j: (row(b, j), cu)),
                  pl.BlockSpec((tt, S5_WIDTH), lambda b, j: (row(b, j), cg)),
                  h_in, vec(S5_NS), vec(S5_NS), vec(S5_NS),
                  full3((S5_BLOCKS, LANE, S5_LANES)), full3((S5_BLOCKS, LANE, S5_LANES)),
                  full3((S5_BLOCKS, S5_LANES, LANE)), full3((S5_BLOCKS, S5_LANES, LANE)),
                  vec(S5_WIDTH), pl.BlockSpec((S5_WIDTH, S5_WIDTH), lambda b, j: (0, 0)), vec(S5_WIDTH)],
        out_specs=[pl.BlockSpec((tt, S5_WIDTH), lambda b, j: (row(b, j), 0)), h_out],
        out_shape=[jax.ShapeDtypeStruct((m, S5_WIDTH), _mix_dtype(t)), h_shape],
        scratch_shapes=[pltpu.VMEM((tt, S5_NS), f32), pltpu.VMEM((tt, S5_NS), f32),
                        pltpu.VMEM((carry_rows, S5_NS), f32), pltpu.VMEM((carry_rows, S5_NS), f32)],
        compiler_params=_params("parallel" if chain else "arbitrary", "arbitrary"),
        name="s5_chain" if chain else "s5_tiles",
    )(h, h, h0, p["lam_re"], p["lam_im"], p["log_step"], p["b_re"], p["b_im"], p["c_re"], p["c_im"],
      p["d"], p["w_glu"], p["b_glu"])


def _gla_kernel(q_ref, k_ref, v_ref, gate_ref, glr_ref, wg_ref, bg_ref, ng_ref, s0_ref, acc_ref, y_ref, s_ref,
                st_ref, b_ref, *, tt, c, nj):
    del acc_ref
    j = pl.program_id(1)

    @pl.when(j == 0)
    def _():
        for hd in range(GLA_HEADS):
            st_ref[hd] = s0_ref[0, hd].T

    la = jax.nn.log_sigmoid(_mm(glr_ref[...], wg_ref[...]) + bg_ref[...]) * (1.0 / GLA_GATE_TAU)
    b_ref[...] = _mm_mask_l(jnp.where(_chunk_tril(tt, c), 1.0, 0.0), la)

    rowi = lax.broadcasted_iota(jnp.int32, (c, GLA_DK), 0)
    lanei = lax.broadcasted_iota(jnp.int32, (c, GLA_DK), 1)
    ones = jnp.ones((GLA_DK, GLA_DK), bf16)
    vzero = jnp.zeros((GLA_DK - c, GLA_DV), f32)
    scale = GLA_DK ** -0.5
    heads = range(GLA_HEADS)
    kl = [slice(hd * GLA_DK, (hd + 1) * GLA_DK) for hd in heads]
    vl = [slice(hd * GLA_DV, (hd + 1) * GLA_DV) for hd in heads]

    def chunk(ci, carry):
        rows = pl.ds(pl.multiple_of(ci * c, c), c)
        qh = [q_ref[rows, kl[hd]] * scale for hd in heads]
        kh = [k_ref[rows, kl[hd]] for hd in heads]
        vh = [v_ref[rows, vl[hd]] for hd in heads]
        bh = [b_ref[rows, kl[hd]] for hd in heads]
        prod = [jnp.concatenate(
            [qh[hd] * kh[hd][jj:jj + 1] * jnp.exp(jnp.where(rowi >= jj, bh[hd] - bh[hd][jj:jj + 1], -jnp.inf))
             for jj in range(c)], axis=0) for hd in heads]
        sums = [_mm(prod[hd], ones) for hd in heads]
        att = []
        for hd in heads:
            a = jnp.zeros((c, GLA_DK), f32)
            for jj in range(c):
                a = a + jnp.where(lanei == jj, sums[hd][jj * c:(jj + 1) * c], 0.0)
            att.append(a)
        s_t = [st_ref[hd] for hd in heads]
        o_in = [_mm(att[hd], jnp.concatenate([vh[hd], vzero], axis=0)) for hd in heads]
        o_st = [_mm_nt(qh[hd] * jnp.exp(bh[hd]), s_t[hd]) for hd in heads]
        b_last = [bh[hd][c - 1:c] for hd in heads]
        upd = [_mm_tn(vh[hd], kh[hd] * jnp.exp(b_last[hd] - bh[hd])) for hd in heads]
        for hd in heads:
            st_ref[hd] = s_t[hd] * jnp.exp(b_last[hd]) + upd[hd]
            o = o_in[hd] + o_st[hd]
            on = o * lax.rsqrt(jnp.mean(o * o, axis=-1, keepdims=True) + NORM_EPS) * ng_ref[:, vl[hd]]
            y_ref[rows, vl[hd]] = (on * jax.nn.silu(gate_ref[rows, vl[hd]])).astype(y_ref.dtype)
        return carry

    lax.fori_loop(0, tt // c, chunk, 0, unroll=math.gcd(tt // c, 4))

    @pl.when(j == nj - 1)
    def _():
        for hd in range(GLA_HEADS):
            s_ref[0, hd] = st_ref[hd].T


def _stacked(acc, shape):
    if acc is None:
        return jnp.zeros((SUBLANE, LANE), f32), None
    assert acc.shape == shape
    return acc, 1


def _gla(h, s0, l0, acc, l, depth, p, nseq, t):
    m = nseq * t
    acc_shape = (depth, nseq, GLA_HEADS, GLA_DK, GLA_DV)
    acc, alias_to = _stacked(acc, acc_shape)
    c = min(t, 16)
    tt = min(t, 256)
    nj = t // tt
    row = lambda b, j: b * nj + j
    vec = lambda n: pl.BlockSpec((1, n), lambda b, j: (0, 0))
    st_spec = lambda li: pl.BlockSpec((None, 1, GLA_HEADS, GLA_DK, GLA_DV), lambda b, j: (li, b, 0, 0, 0))
    return pl.pallas_call(
        functools.partial(_gla_kernel, tt=tt, c=c, nj=nj),
        grid=(nseq, nj),
        in_specs=[pl.BlockSpec((tt, GLA_QK), lambda b, j: (row(b, j), COL_G_Q // GLA_QK)),
                  pl.BlockSpec((tt, GLA_QK), lambda b, j: (row(b, j), COL_G_K // GLA_QK)),
                  pl.BlockSpec((tt, GLA_WIDTH), lambda b, j: (row(b, j), COL_G_V // GLA_WIDTH)),
                  pl.BlockSpec((tt, GLA_WIDTH), lambda b, j: (row(b, j), COL_G_GATE // GLA_WIDTH)),
                  pl.BlockSpec((tt, LANE), lambda b, j: (row(b, j), COL_G_LR // LANE)),
                  pl.BlockSpec((LANE, GLA_QK), lambda b, j: (0, 0)), vec(GLA_QK), vec(GLA_WIDTH), st_spec(l0),
                  pl.BlockSpec(memory_space=pl.ANY)],
        out_specs=[pl.BlockSpec((tt, GLA_WIDTH), lambda b, j: (row(b, j), 0)), st_spec(l)],
        out_shape=[jax.ShapeDtypeStruct((m, GLA_WIDTH), _mix_dtype(t)), jax.ShapeDtypeStruct(acc_shape, f32)],
        scratch_shapes=[pltpu.VMEM((GLA_HEADS, GLA_DV, GLA_DK), f32), pltpu.VMEM((tt, GLA_QK), f32)],
        input_output_aliases={} if alias_to is None else {9: alias_to},
        compiler_params=_params("parallel", "arbitrary"),
        name="gla",
    )(h, h, h, h, h, p["w_gate"], p["b_gate"], p["norm_g"], s0, acc)


def _seg_sum(x, e2):
    n = x.shape[0]
    stacked = jnp.concatenate([x[:, i * LANE:(i + 1) * LANE] for i in range(RWKV_PAIRS)], axis=0)
    s = _mm(stacked, e2)
    return jnp.concatenate([s[i * n:(i + 1) * n] for i in range(RWKV_PAIRS)], axis=1)


def _rwkv_kernel(r_ref, k_ref, v_ref, gate_ref, wa_ref, pr_ref, pk_ref, pv_ref, pwa_ref,
                 mur_ref, muk_ref, muv_ref, muwa_ref, w0_ref, w2_ref, a0_ref, a2_ref, kk_ref, ka_ref, rk_ref,
                 lnw_ref, lnb_ref, s0_ref, acc_ref, y_ref, s_ref,
                 st, c_r, c_k, c_v, c_wa, x_kkg, x_rg, x_bi, x_ki, x_bend, x_kend, x_gam, x_v, x_bon,
                 x_tk, x_rq, x_uv, x_yv, x_y, *, tt, c, nj, chain):
    del acc_ref
    j = pl.program_id(1)
    hd = RWKV_HEAD

    if chain:
        @pl.when(j == 0)
        def _():
            st[...] = s0_ref[0]
            c_r[...] = jnp.broadcast_to(pr_ref[0], c_r.shape)
            c_k[...] = jnp.broadcast_to(pk_ref[0], c_k.shape)
            c_v[...] = jnp.broadcast_to(pv_ref[0], c_v.shape)
            c_wa[...] = jnp.broadcast_to(pwa_ref[0], c_wa.shape)
    else:
        @pl.when(j == 0)
        def _():
            st[...] = jnp.zeros(st.shape, f32)

    def shift(x_ref, c_ref, p_ref, mu_ref):
        x = x_ref[...]
        rowid = lax.broadcasted_iota(jnp.int32, x.shape, 0)
        rolled = pltpu.roll(x, 1, 0)
        if chain:
            prev = jnp.where(rowid == 0, c_ref[0:1, :], rolled)
            c_ref[...] = jnp.broadcast_to(x[tt - 1:tt, :], c_ref.shape)
        else:
            prev = jnp.where(rowid % c == 0, p_ref[...], rolled)
        return x + (prev - x) * mu_ref[...]

    r = shift(r_ref, c_r, pr_ref, mur_ref)
    k = shift(k_ref, c_k, pk_ref, muk_ref)
    v = shift(v_ref, c_v, pv_ref, muv_ref)
    wa = shift(wa_ref, c_wa, pwa_ref, muwa_ref)

    w_log = -jax.nn.softplus(-(w0_ref[...] + _mm(jnp.tanh(wa), w2_ref[...]))) - 0.5
    lw = -jnp.exp(w_log)
    a = jax.nn.sigmoid(a0_ref[...] + _mm(wa, a2_ref[...]))

    lane = lax.broadcasted_iota(jnp.int32, (LANE, LANE), 1)
    sub = lax.broadcasted_iota(jnp.int32, (LANE, LANE), 0)
    bd = (lane // RWKV_HEAD) == (sub // RWKV_HEAD)
    e2 = jnp.where(bd, 1.0, 0.0).astype(f32)

    kk = k * kk_ref[...]
    kk = kk / jnp.maximum(jnp.sqrt(_seg_sum(kk * kk, e2)), 1e-12)
    k2 = k * (1.0 + (a - 1.0) * ka_ref[...])
    bv = kk * a
    ri = lax.broadcasted_iota(jnp.int32, (tt, tt), 0)
    qi = lax.broadcasted_iota(jnp.int32, (tt, tt), 1)
    same = (ri // c) == (qi // c)
    strict = same & (qi < ri)
    incl = same & (qi <= ri)
    eye = jnp.where(qi == ri, 1.0, 0.0).astype(f32)
    cl = _mm_mask_l(jnp.where(incl, 1.0, 0.0), lw)
    if c == tt:
        cll = jnp.broadcast_to(cl[tt - 1:tt, :], cl.shape)
    else:
        cll = _mm_mask_l(jnp.where(same, 1.0, 0.0), lw)
    ginv = jnp.exp(-cl)
    gend = jnp.exp(cll - cl)
    x_kkg[...] = kk * jnp.exp(cl - lw)
    x_rg[...] = r * jnp.exp(cl)
    x_bi[...] = bv * ginv
    x_ki[...] = k2 * ginv
    x_bend[...] = bv * gend
    x_kend[...] = k2 * gend
    x_gam[...] = jnp.exp(cll)
    x_v[...] = v
    x_bon[...] = _seg_sum(r * k2 * rk_ref[...], e2) * v

    head_a = lax.broadcasted_iota(jnp.int32, (tt, LANE), 1) < RWKV_HEAD

    def sel(xa, xb):
        return jnp.where(head_a, xa, xb)

    zero_tt = jnp.zeros((tt, tt), f32)

    def tri_inv(ams):
        def bdiag(x):
            return jnp.concatenate([jnp.concatenate([x[:, 0:tt], zero_tt], axis=1),
                                    jnp.concatenate([zero_tt, x[:, tt:2 * tt]], axis=1)], axis=0)

        if tt % LANE:
            ns = [-am for am in ams]
            ts = [eye + n for n in ns]
            if c > 2:
                ns = [_mm(n, n) for n in ns]
                width = 4
                while width < c:
                    both = [_mm(jnp.concatenate([t, n], axis=0), n) for t, n in zip(ts, ns)]
                    ts = [t + b[0:tt] for t, b in zip(ts, both)]
                    ns = [b[tt:2 * tt] for b in both]
                    width *= 2
                ts = [t + _mm(t, n) for t, n in zip(ts, ns)]
            return ts
        ns = [jnp.concatenate([-ams[i], -ams[i + 1]], axis=1) for i in range(0, len(ams), 2)]
        eye2 = jnp.concatenate([eye, eye], axis=1)
        ts = [eye2 + n for n in ns]
        if c > 2:
            ns = [_mm(n, bdiag(n)) for n in ns]
            width = 4
            while width < c:
                both = [_mm(jnp.concatenate([t, n], axis=0), bdiag(n)) for t, n in zip(ts, ns)]
                ts = [t + b[0:tt] for t, b in zip(ts, both)]
                ns = [b[tt:2 * tt] for b in both]
                width *= 2
            last = [_mm(t, bdiag(n)) for t, n in zip(ts, ns)]
            ts = [t + m for t, m in zip(ts, last)]
        return [t[:, hh * tt:(hh + 1) * tt] for t in ts for hh in (0, 1)]

    def lanes(p):
        return slice(p * LANE, (p + 1) * LANE)

    for g0 in range(0, RWKV_PAIRS, RWKV_GROUP):
        grp = list(range(g0, g0 + RWKV_GROUP))
        kkg = [x_kkg[:, lanes(p)] for p in grp]
        rg = [x_rg[:, lanes(p)] for p in grp]
        vp = [x_v[:, lanes(p)] for p in grp]
        lhs = [jnp.concatenate([jnp.where(head_a, a, 0.0), jnp.where(head_a, 0.0, a),
                                jnp.where(head_a, b, 0.0), jnp.where(head_a, 0.0, b)], axis=0)
               for a, b in zip(kkg, rg)]
        if tt % LANE == 0:
            pbk = [_mm_nt(x, jnp.concatenate([x_bi[:, lanes(p)], x_ki[:, lanes(p)]], axis=0))
                   for x, p in zip(lhs, grp)]
            pb = [x[:, 0:tt] for x in pbk]
            pk = [x[:, tt:2 * tt] for x in pbk]
        else:
            pb = [_mm_nt(x, x_bi[:, lanes(p)]) for x, p in zip(lhs, grp)]
            pk = [_mm_nt(x, x_ki[:, lanes(p)]) for x, p in zip(lhs, grp)]
        tinv = tri_inv([jnp.where(strict, x[hh * tt:(hh + 1) * tt], 0.0) for x in pb for hh in (0, 1)])
        ar = [jnp.where(incl, x[hh * tt:(hh + 1) * tt], 0.0) for x in pb for hh in (2, 3)]
        bmr = [jnp.concatenate([jnp.where(strict, x[hh * tt:(hh + 1) * tt], 0.0),
                                jnp.where(incl, x[(hh + 2) * tt:(hh + 3) * tt], 0.0)], axis=0)
               for x in pk for hh in (0, 1)]
        mv = [_mm(x, vp[i // 2]) for i, x in enumerate(bmr)]
        bmv = [sel(mv[2 * i][0:tt], mv[2 * i + 1][0:tt]) for i in range(RWKV_GROUP)]
        brv = [sel(mv[2 * i][tt:2 * tt], mv[2 * i + 1][tt:2 * tt]) for i in range(RWKV_GROUP)]
        rhs_t = [jnp.concatenate([a, b], axis=1) for a, b in zip(bmv, kkg)]
        mt = [_mm(x, rhs_t[i // 2]) for i, x in enumerate(tinv)]
        uv = [-sel(mt[2 * i][:, 0:LANE], mt[2 * i + 1][:, 0:LANE]) for i in range(RWKV_GROUP)]
        tk = [sel(mt[2 * i][:, LANE:2 * LANE], mt[2 * i + 1][:, LANE:2 * LANE]) for i in range(RWKV_GROUP)]
        rhs_r = [jnp.concatenate([a, b], axis=1) for a, b in zip(uv, tk)]
        mr = [_mm(x, rhs_r[i // 2]) for i, x in enumerate(ar)]
        for i, p in enumerate(grp):
            x_uv[:, lanes(p)] = uv[i]
            x_tk[:, lanes(p)] = tk[i]
            x_yv[:, lanes(p)] = sel(mr[2 * i][:, 0:LANE], mr[2 * i + 1][:, 0:LANE]) + brv[i]
            x_rq[:, lanes(p)] = rg[i] - sel(mr[2 * i][:, LANE:2 * LANE], mr[2 * i + 1][:, LANE:2 * LANE])

    def chunk(ci, carry):
        row0 = pl.multiple_of(ci * c, c)
        rows = pl.ds(row0, c)
        pairs = range(RWKV_PAIRS)
        if not chain:
            for p in pairs:
                st[p, 0:hd, 0:hd] = s0_ref[ci, 2 * p]
                st[p, hd:2 * hd, hd:2 * hd] = s0_ref[ci, 2 * p + 1]
        s = [st[p] for p in pairs]
        zy = [_mm_nt(jnp.concatenate([x_tk[rows, lanes(p)], x_rq[rows, lanes(p)]], axis=0), s[p]) for p in pairs]
        u = [x_uv[rows, lanes(p)] - zy[p][0:c] for p in pairs]
        for p in pairs:
            x_y[rows, lanes(p)] = zy[p][c:2 * c] + x_yv[rows, lanes(p)]
        upd = [_mm_tn(jnp.concatenate([u[p], x_v[rows, lanes(p)]], axis=0),
                      jnp.concatenate([x_bend[rows, lanes(p)], x_kend[rows, lanes(p)]], axis=0)) for p in pairs]
        for p in pairs:
            s_new = s[p] * x_gam[pl.ds(row0, 1), lanes(p)] + jnp.where(bd, upd[p], 0.0)
            if chain:
                st[p] = s_new
            else:
                s_ref[ci, 2 * p] = s_new[0:hd, 0:hd]
                s_ref[ci, 2 * p + 1] = s_new[hd:2 * hd, hd:2 * hd]
        return carry

    lax.fori_loop(0, tt // c, chunk, 0)

    y = x_y[...]
    mean = _seg_sum(y, e2) * (1.0 / RWKV_HEAD)
    yc = y - mean
    var = _seg_sum(yc * yc, e2) * (1.0 / RWKV_HEAD)
    yn = yc * lax.rsqrt(var + RWKV_GN_EPS) * lnw_ref[...] + lnb_ref[...] + x_bon[...]
    y_ref[...] = (yn * jax.nn.silu(gate_ref[...])).astype(y_ref.dtype)

    if chain:
        @pl.when(j == nj - 1)
        def _():
            s_ref[0] = st[...]


def _rwkv(h, prev, s0, l0, acc, l, depth, p, nseq, t, chain):
    m = nseq * t
    w = RWKV_WIDTH
    acc_shape = (depth, nseq) + ((RWKV_PAIRS, LANE, LANE) if chain else (RWKV_HEADS, RWKV_HEAD, RWKV_HEAD))
    acc, alias_to = _stacked(acc, acc_shape)
    if chain:
        tt = min(t, 128)
        c = tt
        nj = t // tt
        grid = (nseq, nj)
        row = lambda b, j: b * nj + j
        prev = tuple(x[:, None, :] for x in prev)
        prow = lambda n: pl.BlockSpec((1, 1, n), lambda b, j: (b, 0, 0))
        st_spec = lambda li: pl.BlockSpec((None, 1, RWKV_PAIRS, LANE, LANE), lambda b, j: (li, b, 0, 0, 0))
    else:
        c = t
        tt = 64
        nj = m // tt
        grid = (1, nj)
        row = lambda b, j: j
        prev = tuple(jnp.repeat(x, t, axis=0) for x in prev)
        prow = lambda n: pl.BlockSpec((tt, n), lambda b, j: (j, 0))
        st_spec = lambda li: pl.BlockSpec((None, tt // c, RWKV_HEADS, RWKV_HEAD, RWKV_HEAD),
                                          lambda b, j: (li, j, 0, 0, 0))
    vec = lambda n: pl.BlockSpec((1, n), lambda b, j: (0, 0))
    tok = lambda col: pl.BlockSpec((tt, w), lambda b, j: (row(b, j), col // w))
    big = lambda: pltpu.VMEM((tt, w), f32)
    return pl.pallas_call(
        functools.partial(_rwkv_kernel, tt=tt, c=c, nj=nj, chain=chain),
        grid=grid,
        in_specs=[tok(COL_RW_R), tok(COL_RW_K), tok(COL_RW_V), tok(COL_RW_GATE),
                  pl.BlockSpec((tt, LANE), lambda b, j: (row(b, j), COL_RW_WA // LANE)),
                  prow(w), prow(w), prow(w), prow(LANE),
                  vec(w), vec(w), vec(w), vec(LANE),
                  vec(w), pl.BlockSpec((LANE, w), lambda b, j: (0, 0)), vec(w),
                  pl.BlockSpec((LANE, w), lambda b, j: (0, 0)), vec(w), vec(w), vec(w), vec(w), vec(w), st_spec(l0),
                  pl.BlockSpec(memory_space=pl.ANY)],
        out_specs=[pl.BlockSpec((tt, w), lambda b, j: (row(b, j), 0)), st_spec(l)],
        out_shape=[jax.ShapeDtypeStruct((m, w), _mix_dtype(t)), jax.ShapeDtypeStruct(acc_shape, f32)],
        input_output_aliases={} if alias_to is None else {23: alias_to},
        scratch_shapes=[pltpu.VMEM((RWKV_PAIRS, LANE, LANE), f32),
                        pltpu.VMEM((SUBLANE, w), f32), pltpu.VMEM((SUBLANE, w), f32),
                        pltpu.VMEM((SUBLANE, w), f32), pltpu.VMEM((SUBLANE, LANE), f32)] + [big() for _ in range(14)],
        compiler_params=_params("parallel" if chain else "arbitrary", "arbitrary"),
        name="rwkv_chain" if chain else "rwkv_tiles",
    )(h, h, h, h, h, *prev, p["mu_r"], p["mu_k"], p["mu_v"], p["mu_wa"], p["w0"], p["w2"], p["a0"], p["a2"],
      p["k_k"], p["k_a"], p["r_k"], p["ln_w"], p["ln_b"], s0, acc)


def _xattn_kernel(q_ref, k_ref, v_ref, o_ref):
    scale = XA_HEAD_DIM ** -0.5
    heads = range(XA_HEADS)
    ln = [slice(hd * XA_HEAD_DIM, (hd + 1) * XA_HEAD_DIM) for hd in heads]
    rows = [pl.ds(hd, N_MEM, stride=XA_HEADS) for hd in heads]
    s = [_mm_nt(q_ref[:, ln[hd]] * scale, k_ref[0, rows[hd], :]) for hd in heads]
    s = [x - jnp.max(x, axis=-1, keepdims=True) for x in s]
    e = [jnp.exp(x) for x in s]
    pr = [x / jnp.sum(x, axis=-1, keepdims=True) for x in e]
    o = [_mm(pr[hd], v_ref[0, rows[hd], :]) for hd in heads]
    for hd in heads:
        o_ref[:, ln[hd]] = o[hd].astype(o_ref.dtype)


def _xattn(h, mem_k, mem_v, l0, nseq, t):
    m = nseq * t
    tq = min(t, 512)
    nj = t // tq
    mem_spec = pl.BlockSpec((None, 1, N_MEM * XA_HEADS, XA_HEAD_DIM), lambda b, j: (l0, b, 0, 0))
    return pl.pallas_call(
        _xattn_kernel,
        grid=(nseq, nj),
        in_specs=[pl.BlockSpec((tq, XA_WIDTH), lambda b, j: (b * nj + j, COL_XA_Q // XA_WIDTH)), mem_spec, mem_spec],
        out_specs=pl.BlockSpec((tq, XA_WIDTH), lambda b, j: (b * nj + j, 0)),
        out_shape=jax.ShapeDtypeStruct((m, XA_WIDTH), _mix_dtype(t)),
        compiler_params=_params("parallel", "arbitrary"),
        name="xattn",
    )(h, mem_k, mem_v)


_RW0 = 5136
W_IN_SEGMENTS = ((_RW0, _RW0 + 1536), (_RW0 + 1536, _RW0 + 3072), (_RW0 + 3072, _RW0 + 4608), (9872, 11408),
                 (0, 1024), (1024, 2048), (3072, 4096), (4096, 5120), (2048, 2560), (2560, 3072), (11408, 11920),
                 (_RW0 + 4608, _RW0 + 4736), (5120, 5136))


def _reorder_cast_kernel(w_ref, o_ref):
    dst = 0
    for a, b in W_IN_SEGMENTS:
        o_ref[dst:dst + (b - a), :] = w_ref[a:b, :].astype(bf16)
        dst += b - a
    o_ref[dst:, :] = jnp.zeros((H_COLS - dst, o_ref.shape[1]), bf16)


def _prep_w_in(w_in, tk=256):
    depth, d, n = w_in.shape
    return pl.pallas_call(
        _reorder_cast_kernel,
        grid=(depth, d // tk),
        in_specs=[pl.BlockSpec((None, n, tk), lambda l, i: (l, 0, i))],
        out_specs=pl.BlockSpec((None, H_COLS, tk), lambda l, i: (l, 0, i)),
        out_shape=jax.ShapeDtypeStruct((depth, H_COLS, d), bf16),
        compiler_params=_params("parallel", "parallel"),
        name="w_in_layout",
    )(jnp.swapaxes(w_in, 1, 2))


def _block_diag_s5(b, c):
    gl = S5_GROUPS // S5_BLOCKS
    eye = jnp.eye(gl, dtype=b.dtype)
    bb = b.reshape(S5_BLOCKS, gl, S5_STATE, S5_GROUP)
    bbd = jnp.einsum("kgpn,gh->kgnhp", bb, eye).reshape(S5_BLOCKS, gl * S5_GROUP, gl * S5_STATE)
    cc = c.reshape(S5_BLOCKS, gl, S5_GROUP, S5_STATE)
    cbd = jnp.einsum("kgnp,gh->kgphn", cc, eye).reshape(S5_BLOCKS, gl * S5_STATE, gl * S5_GROUP)
    return bbd, cbd


def _layer_params(l, a):
    row = lambda x: x.reshape(1, -1)
    bre, cre = _block_diag_s5(a["s5_b_re"][l], a["s5_c_re"][l])
    bim, cim = _block_diag_s5(a["s5_b_im"][l], a["s5_c_im"][l])
    s5 = dict(lam_re=row(a["s5_lam_re"][l]), lam_im=row(a["s5_lam_im"][l]),
              log_step=row(jnp.broadcast_to(a["s5_log_step"][l][:, None], (S5_GROUPS, S5_STATE))),
              b_re=bre, b_im=bim, c_re=cre.astype(bf16), c_im=cim.astype(bf16),
              d=row(a["s5_d"][l]), w_glu=a["s5_w_glu"][l].astype(bf16), b_glu=row(a["s5_b_glu"][l]))
    wg = jnp.zeros((LANE, GLA_QK), f32).at[:GLA_GATE_RANK].set(a["gla_w_gate"][l])
    gla = dict(w_gate=wg, b_gate=row(a["gla_b_gate"][l]), norm_g=row(a["gla_norm_g"][l]))
    mu = a["rw_mu"][l]
    w = RWKV_WIDTH
    z64 = jnp.zeros((64, w), f32)
    rw = dict(mu_r=row(mu[0:w]), mu_k=row(mu[w:2 * w]), mu_v=row(mu[2 * w:3 * w]), mu_wa=row(mu[3 * w:]),
              w0=row(a["rw_w0"][l]), w2=jnp.concatenate([a["rw_w2"][l], z64], 0).astype(bf16),
              a0=row(a["rw_a0"][l]), a2=jnp.concatenate([z64, a["rw_a2"][l]], 0).astype(bf16),
              k_k=row(a["rw_k_k"][l]), k_a=row(a["rw_k_a"][l]), r_k=row(a["rw_r_k"][l]),
              ln_w=row(a["rw_ln_w"][l]), ln_b=row(a["rw_ln_b"][l]))
    return s5, gla, rw


def _unpack_rwkv_state(s):
    b = s.shape[0]
    s = s.reshape(b, RWKV_PAIRS, 2, RWKV_HEAD, 2, RWKV_HEAD)
    return jnp.stack([s[:, :, 0, :, 0, :], s[:, :, 1, :, 1, :]], axis=2).reshape(b, RWKV_HEADS, RWKV_HEAD, RWKV_HEAD)


def _split_shift(x):
    w = RWKV_WIDTH
    return tuple(x[:, a:b] for a, b in ((0, w), (w, 2 * w), (2 * w, 3 * w), (3 * w, 3 * w + LANE)))


def _mixer_layer(x, xn, nseq, t, mem_k, mem_v, s_s5, s_gla, s_rw, l0, acc_gla, acc_rw, l, depth, prev, chain, lp,
                 w_in_t, w_out_b, g_pre, g_post, g_next):
    s5p, glap, rwp = lp
    h = _norm_mm(x, g_pre, w_in_t, l) if xn is None else _proj(xn, w_in_t, l)
    y_s5, s5_state = _s5(h, s_s5, l0, s5p, nseq, t, chain)
    y_gla, s_gla_new = _gla(h, s_gla, l0, acc_gla, l, depth, glap, nseq, t)
    y_rw, s_rw_new = _rwkv(h, prev, s_rw, l0, acc_rw, l, depth, rwp, nseq, t, chain)
    y_xa = _xattn(h, mem_k, mem_v, l0, nseq, t)
    x_new, xn_new = _outproj((y_s5, y_gla, y_rw, y_xa), w_out_b, l, g_post, x, g_next)
    last = h.reshape(nseq, t, H_COLS)[:, t - 1]
    w = RWKV_WIDTH
    shift = jnp.concatenate([last[:, COL_RW_R:COL_RW_R + w], last[:, COL_RW_K:COL_RW_K + w],
                             last[:, COL_RW_V:COL_RW_V + w], last[:, COL_RW_WA:COL_RW_WA + LANE]], axis=1)
    return x_new, xn_new, s5_state, s_gla_new, s_rw_new, shift


def kernel(x_prompt, x_sample, mem_prompt, cache_mem_k, cache_mem_v, state_s5, state_gla, state_rwkv, state_rwkv_shift, g_pre, g_post, w_in, w_out, s5_lam_re, s5_lam_im, s5_log_step, s5_b_re, s5_b_im, s5_c_re, s5_c_im, s5_d, s5_w_glu, s5_b_glu, gla_w_gate, gla_b_gate, gla_norm_g, rw_mu, rw_w0, rw_w2, rw_a0, rw_a2, rw_k_k, rw_k_a, rw_r_k, rw_ln_w, rw_ln_b, mem_g, w_mk, w_mv):
    raw = dict(s5_lam_re=s5_lam_re, s5_lam_im=s5_lam_im, s5_log_step=s5_log_step, s5_b_re=s5_b_re, s5_b_im=s5_b_im,
               s5_c_re=s5_c_re, s5_c_im=s5_c_im, s5_d=s5_d, s5_w_glu=s5_w_glu, s5_b_glu=s5_b_glu,
               gla_w_gate=gla_w_gate, gla_b_gate=gla_b_gate, gla_norm_g=gla_norm_g, rw_mu=rw_mu, rw_w0=rw_w0,
               rw_w2=rw_w2, rw_a0=rw_a0, rw_a2=rw_a2, rw_k_k=rw_k_k, rw_k_a=rw_k_a, rw_r_k=rw_r_k,
               rw_ln_w=rw_ln_w, rw_ln_b=rw_ln_b)
    bp, tp, d = x_prompt.shape
    bs, ts, _ = x_sample.shape
    depth = w_in.shape[0]
    w_in_t = _prep_w_in(w_in)
    w_out_b = w_out.astype(bf16)
    w_kv_t = jnp.swapaxes(jnp.concatenate([w_mk, w_mv], axis=2), 1, 2).astype(bf16)
    mem_rows = N_MEM * XA_HEADS

    xp = x_prompt.reshape(bp * tp, d)
    xs = x_sample.reshape(bs * ts, d)
    mem2d = mem_prompt.reshape(bp * N_MEM, d)
    zp = lambda *s: jnp.zeros(s, f32)
    prev_p = _split_shift(zp(bp, RWKV_SHIFT_COLS))

    gla_p = gla_s = rw_p = rw_s = xnp = xns = None
    zero_gla = zp(1, bp, GLA_HEADS, GLA_DK, GLA_DV)
    zero_rw = zp(1, bp, RWKV_PAIRS, LANE, LANE)
    zero_s5 = zp(1, bp, 2, S5_NS)
    s5_in = state_s5.reshape(depth, bs, S5_NS * 2).transpose(0, 2, 1)
    mem_k_s = cache_mem_k.reshape(depth, bs, mem_rows, XA_HEAD_DIM)
    mem_v_s = cache_mem_v.reshape(depth, bs, mem_rows, XA_HEAD_DIM)

    outs = [[] for _ in range(6)]
    for l in range(depth):
        lp = _layer_params(l, raw)
        gp, gq = g_pre[l].reshape(1, d), g_post[l].reshape(1, d)
        gn = g_pre[l + 1].reshape(1, d) if l + 1 < depth else None
        kv = _norm_mm(mem2d, mem_g[l].reshape(1, d), w_kv_t, l)
        k_mem = kv[:, :XA_WIDTH].reshape(bp, N_MEM, XA_WIDTH)
        v_mem = kv[:, XA_WIDTH:].reshape(bp, N_MEM, XA_WIDTH)
        xp, xnp, h1, gla_p, rw_p, p1 = _mixer_layer(
            xp, xnp, bp, tp, k_mem.reshape(1, bp, mem_rows, XA_HEAD_DIM), v_mem.reshape(1, bp, mem_rows, XA_HEAD_DIM),
            zero_s5, zero_gla, zero_rw, 0, gla_p, rw_p, l, depth,
            prev_p, True, lp, w_in_t, w_out_b, gp, gq, gn)
        xs, xns, h2, gla_s, rw_s, p2 = _mixer_layer(
            xs, xns, bs, ts, mem_k_s, mem_v_s,
            s5_in, state_gla, state_rwkv, l, gla_s, rw_s, l, depth,
            _split_shift(state_rwkv_shift[l]), False, lp, w_in_t, w_out_b, gp, gq, gn)
        vals = (k_mem.reshape(bp, N_MEM, XA_HEADS, XA_HEAD_DIM), v_mem.reshape(bp, N_MEM, XA_HEADS, XA_HEAD_DIM),
                h1, h2, p1, p2)
        for o, val in zip(outs, vals):
            o.append(val)

    mk, mv, s5_p, s5_s, sh_p, sh_s = (jnp.stack(o) for o in outs)
    s5_p = s5_p.transpose(0, 1, 3, 2).reshape(depth, bp, S5_GROUPS, S5_STATE, 2)
    s5_s = s5_s.transpose(0, 2, 1).reshape(depth, bs, S5_GROUPS, S5_STATE, 2)
    rw_p_heads = _unpack_rwkv_state(rw_p.reshape(depth * bp, RWKV_PAIRS, LANE, LANE))
    rw_p_heads = rw_p_heads.reshape(depth, bp, RWKV_HEADS, RWKV_HEAD, RWKV_HEAD)
    return (xp.reshape(bp, tp, d), xs.reshape(bs, ts, d), mk, mv, s5_p, s5_s, gla_p, gla_s, rw_p_heads, rw_s,
            sh_p, sh_s)
```

```python
import functools
import math

import jax
import jax.numpy as jnp
from jax import lax
from jax.experimental import pallas as pl
from jax.experimental.pallas import tpu as pltpu

f32 = jnp.float32
bf16 = jnp.bfloat16

D_MODEL = 4096
DEPTH = 4
N_MEM = 256
NORM_EPS = 1e-6

S5_WIDTH = 1024
S5_GROUP = 16
S5_GROUPS = 64
S5_STATE = 64
S5_NS = S5_GROUPS * S5_STATE
S5_BLOCKS = 8
S5_LANES = 512

GLA_WIDTH = 1024
GLA_HEADS = 4
GLA_DV = 256
GLA_DK = 128
GLA_QK = 512
GLA_GATE_RANK = 16
GLA_GATE_TAU = 16.0

RWKV_WIDTH = 1536
RWKV_HEAD = 64
RWKV_HEADS = 24
RWKV_PAIRS = 12
RWKV_GROUP = 12
RWKV_GN_EPS = RWKV_HEAD * 1e-5
RWKV_SHIFT_COLS = 3 * RWKV_WIDTH + 128

XA_HEADS = 4
XA_HEAD_DIM = 128
XA_WIDTH = 512

LANE = 128
SUBLANE = 8
VMEM_LIMIT = 56 * 1024 * 1024

H_COLS = 12288
COL_RW_R, COL_RW_K, COL_RW_V, COL_RW_GATE = 0, 1536, 3072, 4608
COL_S5_U, COL_S5_GATE, COL_G_V, COL_G_GATE = 6144, 7168, 8192, 9216
COL_G_Q, COL_G_K, COL_XA_Q = 10240, 10752, 11264
COL_RW_WA, COL_G_LR = 11776, 11904


def _mm(a, b):
    return jnp.dot(a.astype(bf16), b.astype(bf16), preferred_element_type=f32)


def _mm_nt(a, b):
    return lax.dot_general(a.astype(bf16), b.astype(bf16), (((1,), (1,)), ((), ())), preferred_element_type=f32)


def _mm_tn(a, b):
    return lax.dot_general(a.astype(bf16), b.astype(bf16), (((0,), (0,)), ((), ())), preferred_element_type=f32)


def _split2(x):
    hi = x.astype(bf16)
    return hi, (x - hi.astype(f32)).astype(bf16)


def _mm_mask_l(mask01, x):
    hi, lo = _split2(x)
    m = mask01.astype(bf16)
    return jnp.dot(m, hi, preferred_element_type=f32) + jnp.dot(m, lo, preferred_element_type=f32)


def _mix_dtype(t):
    return bf16 if t % (2 * SUBLANE) == 0 else f32


def _params(*sem):
    return pltpu.CompilerParams(dimension_semantics=sem, vmem_limit_bytes=VMEM_LIMIT)


def _chunk_tril(n, c, strict=False):
    r = lax.broadcasted_iota(jnp.int32, (n, n), 0)
    q = lax.broadcasted_iota(jnp.int32, (n, n), 1)
    same = (r // c) == (q // c)
    return same & ((q < r) if strict else (q <= r))


def _norm_mm_kernel(x_ref, g_ref, w_ref, o_ref, xn_ref):
    @pl.when(pl.program_id(1) == 0)
    def _():
        rows = 256
        for r0 in range(0, x_ref.shape[0], rows):
            x = x_ref[r0:r0 + rows, :]
            ms = jnp.mean(x * x, axis=-1, keepdims=True)
            xn_ref[r0:r0 + rows, :] = (x * lax.rsqrt(ms + NORM_EPS) * g_ref[...]).astype(bf16)

    o_ref[...] = lax.dot_general(xn_ref[...], w_ref[...], (((1,), (1,)), ((), ())), preferred_element_type=f32)


def _norm_mm(x, g, wt, l):
    m, d = x.shape
    n = wt.shape[1]
    tm = min(m, 1024)
    tn = 768 if n % 768 == 0 else 512
    xmode = dict(pipeline_mode=pl.Buffered(1))
    return pl.pallas_call(
        _norm_mm_kernel,
        grid=(m // tm, n // tn),
        in_specs=[pl.BlockSpec((tm, d), lambda i, j: (i, 0), **xmode),
                  pl.BlockSpec((1, d), lambda i, j: (0, 0)),
                  pl.BlockSpec((None, tn, d), lambda i, j: (l, j, 0))],
        out_specs=pl.BlockSpec((tm, tn), lambda i, j: (i, j)),
        out_shape=jax.ShapeDtypeStruct((m, n), f32),
        scratch_shapes=[pltpu.VMEM((tm, d), bf16)],
        compiler_params=_params("parallel", "arbitrary"),
        name="norm_mm",
    )(x, g, wt)


def _proj_kernel(xn_ref, w_ref, o_ref):
    o_ref[...] = lax.dot_general(xn_ref[...], w_ref[...], (((1,), (1,)), ((), ())), preferred_element_type=f32)


def _proj(xn, wt, l):
    m, d = xn.shape
    n = wt.shape[1]
    tm = min(m, 1024)
    tn = 768 if n % 768 == 0 else 512
    return pl.pallas_call(
        _proj_kernel,
        grid=(m // tm, n // tn),
        in_specs=[pl.BlockSpec((tm, d), lambda i, j: (i, 0)),
                  pl.BlockSpec((None, tn, d), lambda i, j: (l, j, 0))],
        out_specs=pl.BlockSpec((tm, tn), lambda i, j: (i, j)),
        out_shape=jax.ShapeDtypeStruct((m, n), f32),
        compiler_params=_params("parallel", "arbitrary"),
        name="proj",
    )(xn, wt)


def _outproj_kernel(a1, a2, a3, a4, w_ref, g_ref, gn_ref, x_ref, o_ref, *xn_ref, nj, tn):
    j = pl.program_id(1)
    acc = None
    k0 = 0
    for a in (a1, a2, a3, a4):
        part = _mm(a[...], w_ref[k0:k0 + a.shape[1], :])
        acc = part if acc is None else acc + part
        k0 += a.shape[1]
    o_ref[:, pl.ds(pl.multiple_of(j * tn, tn), tn)] = acc

    @pl.when(j == nj - 1)
    def _():
        cols = [slice(k * tn, (k + 1) * tn) for k in range(nj)]
        ssq = None
        for sl in cols:
            p = o_ref[:, sl]
            s = jnp.sum(p * p, axis=-1, keepdims=True)
            ssq = s if ssq is None else ssq + s
        inv = lax.rsqrt(ssq * (1.0 / D_MODEL) + NORM_EPS)
        ssq = None
        for sl in cols:
            new = x_ref[:, sl] + o_ref[:, sl] * inv * g_ref[:, sl]
            o_ref[:, sl] = new
            s = jnp.sum(new * new, axis=-1, keepdims=True)
            ssq = s if ssq is None else ssq + s
        if xn_ref:
            inv = lax.rsqrt(ssq * (1.0 / D_MODEL) + NORM_EPS)
            for sl in cols:
                xn_ref[0][:, sl] = (o_ref[:, sl] * inv * gn_ref[:, sl]).astype(bf16)


def _outproj(parts, w, l, g, x, g_next, tn=512):
    m, d = x.shape
    nj = d // tn
    tm = 512 if parts[0].dtype == bf16 else 256
    emit = g_next is not None
    row_spec = pl.BlockSpec((tm, d), lambda i, j: (i, 0))
    in_specs = [pl.BlockSpec((tm, p.shape[1]), lambda i, j: (i, 0)) for p in parts]
    in_specs += [pl.BlockSpec((None, d, tn), lambda i, j: (l, 0, j))]
    in_specs += [pl.BlockSpec((1, d), lambda i, j: (0, 0)), pl.BlockSpec((1, d), lambda i, j: (0, 0)),
                 pl.BlockSpec((tm, d), lambda i, j: (i, 0), pipeline_mode=pl.Buffered(1))]
    out = pl.pallas_call(
        functools.partial(_outproj_kernel, nj=nj, tn=tn),
        grid=(m // tm, nj),
        in_specs=in_specs,
        out_specs=[row_spec, row_spec] if emit else [row_spec],
        out_shape=[jax.ShapeDtypeStruct((m, d), f32)] + ([jax.ShapeDtypeStruct((m, d), bf16)] if emit else []),
        compiler_params=_params("parallel", "arbitrary"),
        name="outproj",
    )(*parts, w, g, g_next if emit else g, x)
    return (out[0], out[1]) if emit else (out[0], None)


def _s5_kernel(u_ref, gate_ref, h0_ref, lre_ref, lim_ref, step_ref, bre_ref, bim_ref, cre_ref, cim_ref,
               d_ref, wglu_ref, bglu_ref, y_ref, hl_ref, sre, sim, car, cai, *, tt, nj, chain):
    j = pl.program_id(1)
    nrt = tt // SUBLANE
    lr = lre_ref[...]
    li = lim_ref[...]
    st = jnp.exp(step_ref[...])
    kidx = (lax.broadcasted_iota(jnp.int32, (SUBLANE, S5_NS), 0) + 1).astype(f32)
    mag = jnp.exp(kidx * (lr * st))
    ang = kidx * (li * st)
    pw_re = mag * jnp.cos(ang)
    pw_im = mag * jnp.sin(ang)
    ab_re = pw_re[0:1]
    ab_im = pw_im[0:1]
    den = lr * lr + li * li
    f_re = ((ab_re - 1.0) * lr + ab_im * li) / den
    f_im = (ab_im * lr - (ab_re - 1.0) * li) / den

    u = u_ref[...]
    for blk in range(S5_BLOCKS):
        cols = slice(blk * S5_LANES, (blk + 1) * S5_LANES)
        fr = f_re[:, cols]
        fi = f_im[:, cols]
        b_re = bre_ref[blk]
        b_im = bim_ref[blk]
        ub = u[:, blk * LANE:(blk + 1) * LANE].astype(bf16)
        sre[:, cols] = _mm(ub, fr * b_re - fi * b_im)
        sim[:, cols] = _mm(ub, fr * b_im + fi * b_re)

    @pl.when(j == 0)
    def _():
        if chain:
            car[...] = jnp.broadcast_to(h0_ref[0, 0:1], (SUBLANE, S5_NS))
            cai[...] = jnp.broadcast_to(h0_ref[0, 1:2], (SUBLANE, S5_NS))
        else:
            car[...] = h0_ref[pl.ds(0, S5_NS, stride=2), :].T
            cai[...] = h0_ref[pl.ds(1, S5_NS, stride=2), :].T

    rowi = lax.broadcasted_iota(jnp.int32, (SUBLANE, S5_LANES), 0)
    for ch in range(S5_NS // S5_LANES):
        cols = slice(ch * S5_LANES, (ch + 1) * S5_LANES)
        p_re = pw_re[:, cols]
        p_im = pw_im[:, cols]
        steps = [(d, jnp.where(rowi >= d, p_re[d - 1:d], 0.0), jnp.where(rowi >= d, p_im[d - 1:d], 0.0))
                 for d in (1, 2, 4)]

        def body(r, carry, cols=cols, p_re=p_re, p_im=p_im, steps=steps):
            rows = pl.ds(pl.multiple_of(r * SUBLANE, SUBLANE), SUBLANE)
            xr = sre[rows, cols]
            xi = sim[rows, cols]
            for d, ar, ai in steps:
                sr = pltpu.roll(xr, d, 0)
                si = pltpu.roll(xi, d, 0)
                xr, xi = xr + ar * sr - ai * si, xi + ar * si + ai * sr
            if chain:
                cr, ci = carry
            else:
                seq = pl.ds(j * nrt + r, 1)
                cr = jnp.broadcast_to(car[seq, cols], (SUBLANE, S5_LANES))
                ci = jnp.broadcast_to(cai[seq, cols], (SUBLANE, S5_LANES))
            xr, xi = xr + p_re * cr - p_im * ci, xi + p_re * ci + p_im * cr
            sre[rows, cols] = xr
            sim[rows, cols] = xi
            if chain:
                return (jnp.broadcast_to(xr[SUBLANE - 1:SUBLANE], (SUBLANE, S5_LANES)),
                        jnp.broadcast_to(xi[SUBLANE - 1:SUBLANE], (SUBLANE, S5_LANES)))
            car[seq, cols] = xr[SUBLANE - 1:SUBLANE]
            cai[seq, cols] = xi[SUBLANE - 1:SUBLANE]
            return carry

        if chain:
            cr, ci = lax.fori_loop(0, nrt, body, (car[:, cols], cai[:, cols]))
            car[:, cols] = cr
            cai[:, cols] = ci
        else:
            lax.fori_loop(0, nrt, body, 0)

    if chain:
        hl_ref[0, 0:1] = car[0:1, :]
        hl_ref[0, 1:2] = cai[0:1, :]
    else:
        @pl.when(j == nj - 1)
        def _():
            hl_ref[pl.ds(0, S5_NS, stride=2), :] = car[...].T
            hl_ref[pl.ds(1, S5_NS, stride=2), :] = cai[...].T

    ys = []
    for blk in range(S5_BLOCKS):
        cols = slice(blk * S5_LANES, (blk + 1) * S5_LANES)
        ys.append(_mm(sre[:, cols], cre_ref[blk]) - _mm(sim[:, cols], cim_ref[blk]))
    y = jnp.concatenate(ys, axis=1) + d_ref[...] * u
    z = jax.nn.gelu(y)
    gl = jax.nn.sigmoid(_mm(z, wglu_ref[...]) + bglu_ref[...])
    y_ref[...] = (z * gl * jax.nn.silu(gate_ref[...])).astype(y_ref.dtype)


def _s5(h, h0, l0, p, nseq, t, chain):
    m = nseq * t
    if chain:
        tt = min(t, 512)
        nj = t // tt
        grid = (nseq, nj)
        row = lambda b, j: b * nj + j
        h_in = pl.BlockSpec((None, 1, 2, S5_NS), lambda b, j: (l0, b, 0, 0))
        h_out = pl.BlockSpec((1, 2, S5_NS), lambda b, j: (b, 0, 0))
        h_shape = jax.ShapeDtypeStruct((nseq, 2, S5_NS), f32)
        carry_rows = SUBLANE
    else:
        assert t == SUBLANE and nseq == LANE
        tt = 64
        nj = m // tt
        grid = (1, nj)
        row = lambda b, j: j
        h_in = pl.BlockSpec((None, 2 * S5_NS, nseq), lambda b, j: (l0, 0, 0))
        h_out = pl.BlockSpec((2 * S5_NS, nseq), lambda b, j: (0, 0))
        h_shape = jax.ShapeDtypeStruct((2 * S5_NS, nseq), f32)
        carry_rows = nseq
    cu, cg = COL_S5_U // S5_WIDTH, COL_S5_GATE // S5_WIDTH
    vec = lambda n: pl.BlockSpec((1, n), lambda b, j: (0, 0))
    full3 = lambda s: pl.BlockSpec(s, lambda b, j: (0, 0, 0))
    return pl.pallas_call(
        functools.partial(_s5_kernel, tt=tt, nj=nj, chain=chain),
        grid=grid,
        in_specs=[pl.BlockSpec((tt, S5_WIDTH), lambda b, j: (row(b, j), cu)),
                  pl.BlockSpec((tt, S5_WIDTH), lambda b, j: (row(b, j), cg)),
                  h_in, vec(S5_NS), vec(S5_NS), vec(S5_NS),
                  full3((S5_BLOCKS, LANE, S5_LANES)), full3((S5_BLOCKS, LANE, S5_LANES)),
                  full3((S5_BLOCKS, S5_LANES, LANE)), full3((S5_BLOCKS, S5_LANES, LANE)),
                  vec(S5_WIDTH), pl.BlockSpec((S5_WIDTH, S5_WIDTH), lambda b, j: (0, 0)), vec(S5_WIDTH)],
        out_specs=[pl.BlockSpec((tt, S5_WIDTH), lambda b, j: (row(b, j), 0)), h_out],
        out_shape=[jax.ShapeDtypeStruct((m, S5_WIDTH), _mix_dtype(t)), h_shape],
        scratch_shapes=[pltpu.VMEM((tt, S5_NS), f32), pltpu.VMEM((tt, S5_NS), f32),
                        pltpu.VMEM((carry_rows, S5_NS), f32), pltpu.VMEM((carry_rows, S5_NS), f32)],
        compiler_params=_params("parallel" if chain else "arbitrary", "arbitrary"),
        name="s5_chain" if chain else "s5_tiles",
    )(h, h, h0, p["lam_re"], p["lam_im"], p["log_step"], p["b_re"], p["b_im"], p["c_re"], p["c_im"],
      p["d"], p["w_glu"], p["b_glu"])


def _gla_kernel(q_ref, k_ref, v_ref, gate_ref, glr_ref, wg_ref, bg_ref, ng_ref, s0_ref, acc_ref, y_ref, s_ref,
                st_ref, b_ref, *, tt, c, nj, chain):
    del acc_ref
    j = pl.program_id(1)

    if chain:
        @pl.when(j == 0)
        def _():
            for hd in range(GLA_HEADS):
                st_ref[hd] = s0_ref[0, hd].T

    la = jax.nn.log_sigmoid(_mm(glr_ref[...], wg_ref[...]) + bg_ref[...]) * (1.0 / GLA_GATE_TAU)
    b_ref[...] = _mm_mask_l(jnp.where(_chunk_tril(tt, c), 1.0, 0.0), la)

    rowi = lax.broadcasted_iota(jnp.int32, (c, GLA_DK), 0)
    lanei = lax.broadcasted_iota(jnp.int32, (c, GLA_DK), 1)
    ones = jnp.ones((GLA_DK, GLA_DK), bf16)
    vzero = jnp.zeros((GLA_DK - c, GLA_DV), f32)
    scale = GLA_DK ** -0.5
    heads = range(GLA_HEADS)
    kl = [slice(hd * GLA_DK, (hd + 1) * GLA_DK) for hd in heads]
    vl = [slice(hd * GLA_DV, (hd + 1) * GLA_DV) for hd in heads]

    def chunk(ci, carry):
        rows = pl.ds(pl.multiple_of(ci * c, c), c)
        qh = [q_ref[rows, kl[hd]] * scale for hd in heads]
        kh = [k_ref[rows, kl[hd]] for hd in heads]
        vh = [v_ref[rows, vl[hd]] for hd in heads]
        bh = [b_ref[rows, kl[hd]] for hd in heads]
        prod = [jnp.concatenate(
            [qh[hd] * kh[hd][jj:jj + 1] * jnp.exp(jnp.where(rowi >= jj, bh[hd] - bh[hd][jj:jj + 1], -jnp.inf))
             for jj in range(c)], axis=0) for hd in heads]
        sums = [_mm(prod[hd], ones) for hd in heads]
        att = []
        for hd in heads:
            a = jnp.zeros((c, GLA_DK), f32)
            for jj in range(c):
                a = a + jnp.where(lanei == jj, sums[hd][jj * c:(jj + 1) * c], 0.0)
            att.append(a)
        s_t = [st_ref[hd] if chain else s0_ref[ci, hd].T for hd in heads]
        o_in = [_mm(att[hd], jnp.concatenate([vh[hd], vzero], axis=0)) for hd in heads]
        o_st = [_mm_nt(qh[hd] * jnp.exp(bh[hd]), s_t[hd]) for hd in heads]
        b_last = [bh[hd][c - 1:c] for hd in heads]
        upd = [_mm_tn(vh[hd], kh[hd] * jnp.exp(b_last[hd] - bh[hd])) for hd in heads]
        for hd in heads:
            s_new = s_t[hd] * jnp.exp(b_last[hd]) + upd[hd]
            if chain:
                st_ref[hd] = s_new
            else:
                s_ref[ci, hd] = s_new.T
            o = o_in[hd] + o_st[hd]
            on = o * lax.rsqrt(jnp.mean(o * o, axis=-1, keepdims=True) + NORM_EPS) * ng_ref[:, vl[hd]]
            y_ref[rows, vl[hd]] = (on * jax.nn.silu(gate_ref[rows, vl[hd]])).astype(y_ref.dtype)
        return carry

    lax.fori_loop(0, tt // c, chunk, 0, unroll=math.gcd(tt // c, 8))

    if chain:
        @pl.when(j == nj - 1)
        def _():
            for hd in range(GLA_HEADS):
                s_ref[0, hd] = st_ref[hd].T


def _stacked(acc, shape):
    if acc is None:
        return jnp.zeros((SUBLANE, LANE), f32), None
    assert acc.shape == shape
    return acc, 1


def _gla(h, s0, l0, acc, l, depth, p, nseq, t, chain):
    m = nseq * t
    acc_shape = (depth, nseq, GLA_HEADS, GLA_DK, GLA_DV)
    acc, alias_to = _stacked(acc, acc_shape)
    if chain:
        c = 16
        tt = min(t, 128)
        nj = t // tt
        grid = (nseq, nj)
        row = lambda b, j: b * nj + j
        st_spec = lambda li: pl.BlockSpec((None, 1, GLA_HEADS, GLA_DK, GLA_DV), lambda b, j: (li, b, 0, 0, 0))
    else:
        c = t
        tt = 8 * c
        nj = m // tt
        grid = (1, nj)
        row = lambda b, j: j
        st_spec = lambda li: pl.BlockSpec((None, tt // c, GLA_HEADS, GLA_DK, GLA_DV), lambda b, j: (li, j, 0, 0, 0))
    vec = lambda n: pl.BlockSpec((1, n), lambda b, j: (0, 0))
    return pl.pallas_call(
        functools.partial(_gla_kernel, tt=tt, c=c, nj=nj, chain=chain),
        grid=grid,
        in_specs=[pl.BlockSpec((tt, GLA_QK), lambda b, j: (row(b, j), COL_G_Q // GLA_QK)),
                  pl.BlockSpec((tt, GLA_QK), lambda b, j: (row(b, j), COL_G_K // GLA_QK)),
                  pl.BlockSpec((tt, GLA_WIDTH), lambda b, j: (row(b, j), COL_G_V // GLA_WIDTH)),
                  pl.BlockSpec((tt, GLA_WIDTH), lambda b, j: (row(b, j), COL_G_GATE // GLA_WIDTH)),
                  pl.BlockSpec((tt, LANE), lambda b, j: (row(b, j), COL_G_LR // LANE)),
                  pl.BlockSpec((LANE, GLA_QK), lambda b, j: (0, 0)), vec(GLA_QK), vec(GLA_WIDTH), st_spec(l0),
                  pl.BlockSpec(memory_space=pl.ANY)],
        out_specs=[pl.BlockSpec((tt, GLA_WIDTH), lambda b, j: (row(b, j), 0)), st_spec(l)],
        out_shape=[jax.ShapeDtypeStruct((m, GLA_WIDTH), _mix_dtype(t)), jax.ShapeDtypeStruct(acc_shape, f32)],
        scratch_shapes=[pltpu.VMEM((GLA_HEADS, GLA_DV, GLA_DK), f32), pltpu.VMEM((tt, GLA_QK), f32)],
        input_output_aliases={} if alias_to is None else {9: alias_to},
        compiler_params=_params("parallel" if chain else "arbitrary", "arbitrary"),
        name="gla_chain" if chain else "gla_tiles",
    )(h, h, h, h, h, p["w_gate"], p["b_gate"], p["norm_g"], s0, acc)


def _seg_sum(x, e2):
    n = x.shape[0]
    stacked = jnp.concatenate([x[:, i * LANE:(i + 1) * LANE] for i in range(RWKV_PAIRS)], axis=0)
    s = _mm(stacked, e2)
    return jnp.concatenate([s[i * n:(i + 1) * n] for i in range(RWKV_PAIRS)], axis=1)


def _rwkv_kernel(r_ref, k_ref, v_ref, gate_ref, wa_ref, pr_ref, pk_ref, pv_ref, pwa_ref,
                 mur_ref, muk_ref, muv_ref, muwa_ref, w0_ref, w2_ref, a0_ref, a2_ref, kk_ref, ka_ref, rk_ref,
                 lnw_ref, lnb_ref, s0_ref, acc_ref, y_ref, s_ref,
                 st, c_r, c_k, c_v, c_wa, x_kkg, x_rg, x_bi, x_ki, x_bend, x_kend, x_gam, x_v, x_bon,
                 x_tk, x_rq, x_uv, x_yv, x_y, *, tt, c, nj, chain):
    del acc_ref
    j = pl.program_id(1)
    hd = RWKV_HEAD

    if chain:
        @pl.when(j == 0)
        def _():
            st[...] = s0_ref[0]
            c_r[...] = jnp.broadcast_to(pr_ref[0], c_r.shape)
            c_k[...] = jnp.broadcast_to(pk_ref[0], c_k.shape)
            c_v[...] = jnp.broadcast_to(pv_ref[0], c_v.shape)
            c_wa[...] = jnp.broadcast_to(pwa_ref[0], c_wa.shape)
    else:
        @pl.when(j == 0)
        def _():
            st[...] = jnp.zeros(st.shape, f32)

    def shift(x_ref, c_ref, p_ref, mu_ref):
        x = x_ref[...]
        rowid = lax.broadcasted_iota(jnp.int32, x.shape, 0)
        rolled = pltpu.roll(x, 1, 0)
        if chain:
            prev = jnp.where(rowid == 0, c_ref[0:1, :], rolled)
            c_ref[...] = jnp.broadcast_to(x[tt - 1:tt, :], c_ref.shape)
        else:
            prev = jnp.where(rowid % c == 0, p_ref[...], rolled)
        return x + (prev - x) * mu_ref[...]

    r = shift(r_ref, c_r, pr_ref, mur_ref)
    k = shift(k_ref, c_k, pk_ref, muk_ref)
    v = shift(v_ref, c_v, pv_ref, muv_ref)
    wa = shift(wa_ref, c_wa, pwa_ref, muwa_ref)

    w_log = -jax.nn.softplus(-(w0_ref[...] + _mm(jnp.tanh(wa), w2_ref[...]))) - 0.5
    lw = -jnp.exp(w_log)
    a = jax.nn.sigmoid(a0_ref[...] + _mm(wa, a2_ref[...]))

    lane = lax.broadcasted_iota(jnp.int32, (LANE, LANE), 1)
    sub = lax.broadcasted_iota(jnp.int32, (LANE, LANE), 0)
    bd = (lane // RWKV_HEAD) == (sub // RWKV_HEAD)
    e2 = jnp.where(bd, 1.0, 0.0).astype(f32)

    kk = k * kk_ref[...]
    kk = kk / jnp.maximum(jnp.sqrt(_seg_sum(kk * kk, e2)), 1e-12)
    k2 = k * (1.0 + (a - 1.0) * ka_ref[...])
    bv = kk * a
    ri = lax.broadcasted_iota(jnp.int32, (tt, tt), 0)
    qi = lax.broadcasted_iota(jnp.int32, (tt, tt), 1)
    same = (ri // c) == (qi // c)
    strict = same & (qi < ri)
    incl = same & (qi <= ri)
    eye = jnp.where(qi == ri, 1.0, 0.0).astype(f32)
    cl = _mm_mask_l(jnp.where(incl, 1.0, 0.0), lw)
    if c == tt:
        cll = jnp.broadcast_to(cl[tt - 1:tt, :], cl.shape)
    else:
        cll = _mm_mask_l(jnp.where(same, 1.0, 0.0), lw)
    ginv = jnp.exp(-cl)
    gend = jnp.exp(cll - cl)
    x_kkg[...] = kk * jnp.exp(cl - lw)
    x_rg[...] = r * jnp.exp(cl)
    x_bi[...] = bv * ginv
    x_ki[...] = k2 * ginv
    x_bend[...] = bv * gend
    x_kend[...] = k2 * gend
    x_gam[...] = jnp.exp(cll)
    x_v[...] = v
    x_bon[...] = _seg_sum(r * k2 * rk_ref[...], e2) * v

    head_a = lax.broadcasted_iota(jnp.int32, (tt, LANE), 1) < RWKV_HEAD

    def sel(xa, xb):
        return jnp.where(head_a, xa, xb)

    zero_tt = jnp.zeros((tt, tt), f32)

    def tri_inv(ams):
        def bdiag(x):
            return jnp.concatenate([jnp.concatenate([x[:, 0:tt], zero_tt], axis=1),
                                    jnp.concatenate([zero_tt, x[:, tt:2 * tt]], axis=1)], axis=0)

        if tt % LANE:
            ns = [-am for am in ams]
            ts = [eye + n for n in ns]
            if c > 2:
                ns = [_mm(n, n) for n in ns]
                width = 4
                while width < c:
                    both = [_mm(jnp.concatenate([t, n], axis=0), n) for t, n in zip(ts, ns)]
                    ts = [t + b[0:tt] for t, b in zip(ts, both)]
                    ns = [b[tt:2 * tt] for b in both]
                    width *= 2
                ts = [t + _mm(t, n) for t, n in zip(ts, ns)]
            return ts
        ns = [jnp.concatenate([-ams[i], -ams[i + 1]], axis=1) for i in range(0, len(ams), 2)]
        eye2 = jnp.concatenate([eye, eye], axis=1)
        ts = [eye2 + n for n in ns]
        if c > 2:
            ns = [_mm(n, bdiag(n)) for n in ns]
            width = 4
            while width < c:
                both = [_mm(jnp.concatenate([t, n], axis=0), bdiag(n)) for t, n in zip(ts, ns)]
                ts = [t + b[0:tt] for t, b in zip(ts, both)]
                ns = [b[tt:2 * tt] for b in both]
                width *= 2
            last = [_mm(t, bdiag(n)) for t, n in zip(ts, ns)]
            ts = [t + m for t, m in zip(ts, last)]
        return [t[:, hh * tt:(hh + 1) * tt] for t in ts for hh in (0, 1)]

    def lanes(p):
        return slice(p * LANE, (p + 1) * LANE)

    for g0 in range(0, RWKV_PAIRS, RWKV_GROUP):
        grp = list(range(g0, g0 + RWKV_GROUP))
        kkg = [x_kkg[:, lanes(p)] for p in grp]
        rg = [x_rg[:, lanes(p)] for p in grp]
        vp = [x_v[:, lanes(p)] for p in grp]
        lhs = [jnp.concatenate([jnp.where(head_a, a, 0.0), jnp.where(head_a, 0.0, a),
                                jnp.where(head_a, b, 0.0), jnp.where(head_a, 0.0, b)], axis=0)
               for a, b in zip(kkg, rg)]
        if tt % LANE == 0:
            pbk = [_mm_nt(x, jnp.concatenate([x_bi[:, lanes(p)], x_ki[:, lanes(p)]], axis=0))
                   for x, p in zip(lhs, grp)]
            pb = [x[:, 0:tt] for x in pbk]
            pk = [x[:, tt:2 * tt] for x in pbk]
        else:
            pb = [_mm_nt(x, x_bi[:, lanes(p)]) for x, p in zip(lhs, grp)]
            pk = [_mm_nt(x, x_ki[:, lanes(p)]) for x, p in zip(lhs, grp)]
        tinv = tri_inv([jnp.where(strict, x[hh * tt:(hh + 1) * tt], 0.0) for x in pb for hh in (0, 1)])
        ar = [jnp.where(incl, x[hh * tt:(hh + 1) * tt], 0.0) for x in pb for hh in (2, 3)]
        bmr = [jnp.concatenate([jnp.where(strict, x[hh * tt:(hh + 1) * tt], 0.0),
                                jnp.where(incl, x[(hh + 2) * tt:(hh + 3) * tt], 0.0)], axis=0)
               for x in pk for hh in (0, 1)]
        mv = [_mm(x, vp[i // 2]) for i, x in enumerate(bmr)]
        bmv = [sel(mv[2 * i][0:tt], mv[2 * i + 1][0:tt]) for i in range(RWKV_GROUP)]
        brv = [sel(mv[2 * i][tt:2 * tt], mv[2 * i + 1][tt:2 * tt]) for i in range(RWKV_GROUP)]
        rhs_t = [jnp.concatenate([a, b], axis=1) for a, b in zip(bmv, kkg)]
        mt = [_mm(x, rhs_t[i // 2]) for i, x in enumerate(tinv)]
        uv = [-sel(mt[2 * i][:, 0:LANE], mt[2 * i + 1][:, 0:LANE]) for i in range(RWKV_GROUP)]
        tk = [sel(mt[2 * i][:, LANE:2 * LANE], mt[2 * i + 1][:, LANE:2 * LANE]) for i in range(RWKV_GROUP)]
        rhs_r = [jnp.concatenate([a, b], axis=1) for a, b in zip(uv, tk)]
        mr = [_mm(x, rhs_r[i // 2]) for i, x in enumerate(ar)]
        for i, p in enumerate(grp):
            x_uv[:, lanes(p)] = uv[i]
            x_tk[:, lanes(p)] = tk[i]
            x_yv[:, lanes(p)] = sel(mr[2 * i][:, 0:LANE], mr[2 * i + 1][:, 0:LANE]) + brv[i]
            x_rq[:, lanes(p)] = rg[i] - sel(mr[2 * i][:, LANE:2 * LANE], mr[2 * i + 1][:, LANE:2 * LANE])

    def chunk(ci, carry):
        row0 = pl.multiple_of(ci * c, c)
        rows = pl.ds(row0, c)
        pairs = range(RWKV_PAIRS)
        if not chain:
            for p in pairs:
                st[p, 0:hd, 0:hd] = s0_ref[ci, 2 * p]
                st[p, hd:2 * hd, hd:2 * hd] = s0_ref[ci, 2 * p + 1]
        s = [st[p] for p in pairs]
        zy = [_mm_nt(jnp.concatenate([x_tk[rows, lanes(p)], x_rq[rows, lanes(p)]], axis=0), s[p]) for p in pairs]
        u = [x_uv[rows, lanes(p)] - zy[p][0:c] for p in pairs]
        for p in pairs:
            x_y[rows, lanes(p)] = zy[p][c:2 * c] + x_yv[rows, lanes(p)]
        upd = [_mm_tn(jnp.concatenate([u[p], x_v[rows, lanes(p)]], axis=0),
                      jnp.concatenate([x_bend[rows, lanes(p)], x_kend[rows, lanes(p)]], axis=0)) for p in pairs]
        for p in pairs:
            s_new = s[p] * x_gam[pl.ds(row0, 1), lanes(p)] + jnp.where(bd, upd[p], 0.0)
            if chain:
                st[p] = s_new
            else:
                s_ref[ci, 2 * p] = s_new[0:hd, 0:hd]
                s_ref[ci, 2 * p + 1] = s_new[hd:2 * hd, hd:2 * hd]
        return carry

    lax.fori_loop(0, tt // c, chunk, 0)

    y = x_y[...]
    mean = _seg_sum(y, e2) * (1.0 / RWKV_HEAD)
    yc = y - mean
    var = _seg_sum(yc * yc, e2) * (1.0 / RWKV_HEAD)
    yn = yc * lax.rsqrt(var + RWKV_GN_EPS) * lnw_ref[...] + lnb_ref[...] + x_bon[...]
    y_ref[...] = (yn * jax.nn.silu(gate_ref[...])).astype(y_ref.dtype)

    if chain:
        @pl.when(j == nj - 1)
        def _():
            s_ref[0] = st[...]


def _rwkv(h, prev, s0, l0, acc, l, depth, p, nseq, t, chain):
    m = nseq * t
    w = RWKV_WIDTH
    acc_shape = (depth, nseq) + ((RWKV_PAIRS, LANE, LANE) if chain else (RWKV_HEADS, RWKV_HEAD, RWKV_HEAD))
    acc, alias_to = _stacked(acc, acc_shape)
    if chain:
        tt = min(t, 128)
        c = tt
        nj = t // tt
        grid = (nseq, nj)
        row = lambda b, j: b * nj + j
        prev = tuple(x[:, None, :] for x in prev)
        prow = lambda n: pl.BlockSpec((1, 1, n), lambda b, j: (b, 0, 0))
        st_spec = lambda li: pl.BlockSpec((None, 1, RWKV_PAIRS, LANE, LANE), lambda b, j: (li, b, 0, 0, 0))
    else:
        c = t
        tt = 64
        nj = m // tt
        grid = (1, nj)
        row = lambda b, j: j
        prev = tuple(jnp.repeat(x, t, axis=0) for x in prev)
        prow = lambda n: pl.BlockSpec((tt, n), lambda b, j: (j, 0))
        st_spec = lambda li: pl.BlockSpec((None, tt // c, RWKV_HEADS, RWKV_HEAD, RWKV_HEAD),
                                          lambda b, j: (li, j, 0, 0, 0))
    vec = lambda n: pl.BlockSpec((1, n), lambda b, j: (0, 0))
    tok = lambda col: pl.BlockSpec((tt, w), lambda b, j: (row(b, j), col // w))
    big = lambda: pltpu.VMEM((tt, w), f32)
    return pl.pallas_call(
        functools.partial(_rwkv_kernel, tt=tt, c=c, nj=nj, chain=chain),
        grid=grid,
        in_specs=[tok(COL_RW_R), tok(COL_RW_K), tok(COL_RW_V), tok(COL_RW_GATE),
                  pl.BlockSpec((tt, LANE), lambda b, j: (row(b, j), COL_RW_WA // LANE)),
                  prow(w), prow(w), prow(w), prow(LANE),
                  vec(w), vec(w), vec(w), vec(LANE),
                  vec(w), pl.BlockSpec((LANE, w), lambda b, j: (0, 0)), vec(w),
                  pl.BlockSpec((LANE, w), lambda b, j: (0, 0)), vec(w), vec(w), vec(w), vec(w), vec(w), st_spec(l0),
                  pl.BlockSpec(memory_space=pl.ANY)],
        out_specs=[pl.BlockSpec((tt, w), lambda b, j: (row(b, j), 0)), st_spec(l)],
        out_shape=[jax.ShapeDtypeStruct((m, w), _mix_dtype(t)), jax.ShapeDtypeStruct(acc_shape, f32)],
        input_output_aliases={} if alias_to is None else {23: alias_to},
        scratch_shapes=[pltpu.VMEM((RWKV_PAIRS, LANE, LANE), f32),
                        pltpu.VMEM((SUBLANE, w), f32), pltpu.VMEM((SUBLANE, w), f32),
                        pltpu.VMEM((SUBLANE, w), f32), pltpu.VMEM((SUBLANE, LANE), f32)] + [big() for _ in range(14)],
        compiler_params=_params("parallel" if chain else "arbitrary", "arbitrary"),
        name="rwkv_chain" if chain else "rwkv_tiles",
    )(h, h, h, h, h, *prev, p["mu_r"], p["mu_k"], p["mu_v"], p["mu_wa"], p["w0"], p["w2"], p["a0"], p["a2"],
      p["k_k"], p["k_a"], p["r_k"], p["ln_w"], p["ln_b"], s0, acc)


def _xattn_kernel(q_ref, k_ref, v_ref, o_ref, *, nb, tq):
    scale = XA_HEAD_DIM ** -0.5
    probs = [(i, hd) for i in range(nb) for hd in range(XA_HEADS)]
    ln = [slice(hd * XA_HEAD_DIM, (hd + 1) * XA_HEAD_DIM) for hd in range(XA_HEADS)]
    qrows = [slice(i * tq, (i + 1) * tq) for i in range(nb)]
    mrows = [pl.ds(hd, N_MEM, stride=XA_HEADS) for hd in range(XA_HEADS)]
    s = [_mm_nt(q_ref[qrows[i], ln[hd]] * scale, k_ref[i, mrows[hd], :]) for i, hd in probs]
    s = [x - jnp.max(x, axis=-1, keepdims=True) for x in s]
    e = [jnp.exp(x) for x in s]
    pr = [x / jnp.sum(x, axis=-1, keepdims=True) for x in e]
    o = [_mm(pr[n], v_ref[i, mrows[hd], :]) for n, (i, hd) in enumerate(probs)]
    for n, (i, hd) in enumerate(probs):
        o_ref[qrows[i], ln[hd]] = o[n].astype(o_ref.dtype)


def _xattn(h, mem_k, mem_v, l0, nseq, t):
    m = nseq * t
    tq = min(t, 512)
    nj = t // tq
    nb = 8 if (nj == 1 and nseq % 8 == 0) else 1
    mem_spec = pl.BlockSpec((None, nb, N_MEM * XA_HEADS, XA_HEAD_DIM), lambda b, j: (l0, b, 0, 0))
    return pl.pallas_call(
        functools.partial(_xattn_kernel, nb=nb, tq=tq),
        grid=(nseq // nb, nj),
        in_specs=[pl.BlockSpec((nb * tq, XA_WIDTH), lambda b, j: (b * nj + j, COL_XA_Q // XA_WIDTH)),
                  mem_spec, mem_spec],
        out_specs=pl.BlockSpec((nb * tq, XA_WIDTH), lambda b, j: (b * nj + j, 0)),
        out_shape=jax.ShapeDtypeStruct((m, XA_WIDTH), _mix_dtype(t)),
        compiler_params=_params("parallel", "arbitrary"),
        name="xattn",
    )(h, mem_k, mem_v)


_RW0 = 5136
W_IN_SEGMENTS = ((_RW0, _RW0 + 1536), (_RW0 + 1536, _RW0 + 3072), (_RW0 + 3072, _RW0 + 4608), (9872, 11408),
                 (0, 1024), (1024, 2048), (3072, 4096), (4096, 5120), (2048, 2560), (2560, 3072), (11408, 11920),
                 (_RW0 + 4608, _RW0 + 4736), (5120, 5136))


def _reorder_cast_kernel(w_ref, o_ref):
    dst = 0
    for a, b in W_IN_SEGMENTS:
        o_ref[dst:dst + (b - a), :] = w_ref[a:b, :].astype(bf16)
        dst += b - a
    o_ref[dst:, :] = jnp.zeros((H_COLS - dst, o_ref.shape[1]), bf16)


def _prep_w_in(w_in, tk=256):
    depth, d, n = w_in.shape
    return pl.pallas_call(
        _reorder_cast_kernel,
        grid=(depth, d // tk),
        in_specs=[pl.BlockSpec((None, n, tk), lambda l, i: (l, 0, i))],
        out_specs=pl.BlockSpec((None, H_COLS, tk), lambda l, i: (l, 0, i)),
        out_shape=jax.ShapeDtypeStruct((depth, H_COLS, d), bf16),
        compiler_params=_params("parallel", "parallel"),
        name="w_in_layout",
    )(jnp.swapaxes(w_in, 1, 2))


def _block_diag_s5(b, c):
    gl = S5_GROUPS // S5_BLOCKS
    eye = jnp.eye(gl, dtype=b.dtype)
    bb = b.reshape(S5_BLOCKS, gl, S5_STATE, S5_GROUP)
    bbd = jnp.einsum("kgpn,gh->kgnhp", bb, eye).reshape(S5_BLOCKS, gl * S5_GROUP, gl * S5_STATE)
    cc = c.reshape(S5_BLOCKS, gl, S5_GROUP, S5_STATE)
    cbd = jnp.einsum("kgnp,gh->kgphn", cc, eye).reshape(S5_BLOCKS, gl * S5_STATE, gl * S5_GROUP)
    return bbd, cbd


def _layer_params(l, a):
    row = lambda x: x.reshape(1, -1)
    bre, cre = _block_diag_s5(a["s5_b_re"][l], a["s5_c_re"][l])
    bim, cim = _block_diag_s5(a["s5_b_im"][l], a["s5_c_im"][l])
    s5 = dict(lam_re=row(a["s5_lam_re"][l]), lam_im=row(a["s5_lam_im"][l]),
              log_step=row(jnp.broadcast_to(a["s5_log_step"][l][:, None], (S5_GROUPS, S5_STATE))),
              b_re=bre, b_im=bim, c_re=cre.astype(bf16), c_im=cim.astype(bf16),
              d=row(a["s5_d"][l]), w_glu=a["s5_w_glu"][l].astype(bf16), b_glu=row(a["s5_b_glu"][l]))
    wg = jnp.zeros((LANE, GLA_QK), f32).at[:GLA_GATE_RANK].set(a["gla_w_gate"][l])
    gla = dict(w_gate=wg, b_gate=row(a["gla_b_gate"][l]), norm_g=row(a["gla_norm_g"][l]))
    mu = a["rw_mu"][l]
    w = RWKV_WIDTH
    z64 = jnp.zeros((64, w), f32)
    rw = dict(mu_r=row(mu[0:w]), mu_k=row(mu[w:2 * w]), mu_v=row(mu[2 * w:3 * w]), mu_wa=row(mu[3 * w:]),
              w0=row(a["rw_w0"][l]), w2=jnp.concatenate([a["rw_w2"][l], z64], 0).astype(bf16),
              a0=row(a["rw_a0"][l]), a2=jnp.concatenate([z64, a["rw_a2"][l]], 0).astype(bf16),
              k_k=row(a["rw_k_k"][l]), k_a=row(a["rw_k_a"][l]), r_k=row(a["rw_r_k"][l]),
              ln_w=row(a["rw_ln_w"][l]), ln_b=row(a["rw_ln_b"][l]))
    return s5, gla, rw


def _unpack_rwkv_state(s):
    b = s.shape[0]
    s = s.reshape(b, RWKV_PAIRS, 2, RWKV_HEAD, 2, RWKV_HEAD)
    return jnp.stack([s[:, :, 0, :, 0, :], s[:, :, 1, :, 1, :]], axis=2).reshape(b, RWKV_HEADS, RWKV_HEAD, RWKV_HEAD)


def _split_shift(x):
    w = RWKV_WIDTH
    return tuple(x[:, a:b] for a, b in ((0, w), (w, 2 * w), (2 * w, 3 * w), (3 * w, 3 * w + LANE)))


def _mixer_layer(x, xn, nseq, t, mem_k, mem_v, s_s5, s_gla, s_rw, l0, acc_gla, acc_rw, l, depth, prev, chain, lp,
                 w_in_t, w_out_b, g_pre, g_post, g_next):
    s5p, glap, rwp = lp
    h = _norm_mm(x, g_pre, w_in_t, l) if xn is None else _proj(xn, w_in_t, l)
    y_s5, s5_state = _s5(h, s_s5, l0, s5p, nseq, t, chain)
    y_gla, s_gla_new = _gla(h, s_gla, l0, acc_gla, l, depth, glap, nseq, t, chain)
    y_rw, s_rw_new = _rwkv(h, prev, s_rw, l0, acc_rw, l, depth, rwp, nseq, t, chain)
    y_xa = _xattn(h, mem_k, mem_v, l0, nseq, t)
    x_new, xn_new = _outproj((y_s5, y_gla, y_rw, y_xa), w_out_b, l, g_post, x, g_next)
    last = h.reshape(nseq, t, H_COLS)[:, t - 1]
    w = RWKV_WIDTH
    shift = jnp.concatenate([last[:, COL_RW_R:COL_RW_R + w], last[:, COL_RW_K:COL_RW_K + w],
                             last[:, COL_RW_V:COL_RW_V + w], last[:, COL_RW_WA:COL_RW_WA + LANE]], axis=1)
    return x_new, xn_new, s5_state, s_gla_new, s_rw_new, shift


def kernel(x_prompt, x_sample, mem_prompt, cache_mem_k, cache_mem_v, state_s5, state_gla, state_rwkv, state_rwkv_shift, g_pre, g_post, w_in, w_out, s5_lam_re, s5_lam_im, s5_log_step, s5_b_re, s5_b_im, s5_c_re, s5_c_im, s5_d, s5_w_glu, s5_b_glu, gla_w_gate, gla_b_gate, gla_norm_g, rw_mu, rw_w0, rw_w2, rw_a0, rw_a2, rw_k_k, rw_k_a, rw_r_k, rw_ln_w, rw_ln_b, mem_g, w_mk, w_mv):
    raw = dict(s5_lam_re=s5_lam_re, s5_lam_im=s5_lam_im, s5_log_step=s5_log_step, s5_b_re=s5_b_re, s5_b_im=s5_b_im,
               s5_c_re=s5_c_re, s5_c_im=s5_c_im, s5_d=s5_d, s5_w_glu=s5_w_glu, s5_b_glu=s5_b_glu,
               gla_w_gate=gla_w_gate, gla_b_gate=gla_b_gate, gla_norm_g=gla_norm_g, rw_mu=rw_mu, rw_w0=rw_w0,
               rw_w2=rw_w2, rw_a0=rw_a0, rw_a2=rw_a2, rw_k_k=rw_k_k, rw_k_a=rw_k_a, rw_r_k=rw_r_k,
               rw_ln_w=rw_ln_w, rw_ln_b=rw_ln_b)
    bp, tp, d = x_prompt.shape
    bs, ts, _ = x_sample.shape
    depth = w_in.shape[0]
    w_in_t = _prep_w_in(w_in)
    w_out_b = w_out.astype(bf16)
    w_kv_t = jnp.swapaxes(jnp.concatenate([w_mk, w_mv], axis=2), 1, 2).astype(bf16)
    mem_rows = N_MEM * XA_HEADS

    xp = x_prompt.reshape(bp * tp, d)
    xs = x_sample.reshape(bs * ts, d)
    mem2d = mem_prompt.reshape(bp * N_MEM, d)
    zp = lambda *s: jnp.zeros(s, f32)
    prev_p = _split_shift(zp(bp, RWKV_SHIFT_COLS))

    gla_p = gla_s = rw_p = rw_s = xnp = xns = None
    zero_gla = zp(1, bp, GLA_HEADS, GLA_DK, GLA_DV)
    zero_rw = zp(1, bp, RWKV_PAIRS, LANE, LANE)
    zero_s5 = zp(1, bp, 2, S5_NS)
    s5_in = state_s5.reshape(depth, bs, S5_NS * 2).transpose(0, 2, 1)
    mem_k_s = cache_mem_k.reshape(depth, bs, mem_rows, XA_HEAD_DIM)
    mem_v_s = cache_mem_v.reshape(depth, bs, mem_rows, XA_HEAD_DIM)

    outs = [[] for _ in range(6)]
    for l in range(depth):
        lp = _layer_params(l, raw)
        gp, gq = g_pre[l].reshape(1, d), g_post[l].reshape(1, d)
        gn = g_pre[l + 1].reshape(1, d) if l + 1 < depth else None
        kv = _norm_mm(mem2d, mem_g[l].reshape(1, d), w_kv_t, l)
        k_mem = kv[:, :XA_WIDTH].reshape(bp, N_MEM, XA_WIDTH)
        v_mem = kv[:, XA_WIDTH:].reshape(bp, N_MEM, XA_WIDTH)
        xp, xnp, h1, gla_p, rw_p, p1 = _mixer_layer(
            xp, xnp, bp, tp, k_mem.reshape(1, bp, mem_rows, XA_HEAD_DIM), v_mem.reshape(1, bp, mem_rows, XA_HEAD_DIM),
            zero_s5, zero_gla, zero_rw, 0, gla_p, rw_p, l, depth,
            prev_p, True, lp, w_in_t, w_out_b, gp, gq, gn)
        xs, xns, h2, gla_s, rw_s, p2 = _mixer_layer(
            xs, xns, bs, ts, mem_k_s, mem_v_s,
            s5_in, state_gla, state_rwkv, l, gla_s, rw_s, l, depth,
            _split_shift(state_rwkv_shift[l]), False, lp, w_in_t, w_out_b, gp, gq, gn)
        vals = (k_mem.reshape(bp, N_MEM, XA_HEADS, XA_HEAD_DIM), v_mem.reshape(bp, N_MEM, XA_HEADS, XA_HEAD_DIM),
                h1, h2, p1, p2)
        for o, val in zip(outs, vals):
            o.append(val)

    mk, mv, s5_p, s5_s, sh_p, sh_s = (jnp.stack(o) for o in outs)
    s5_p = s5_p.transpose(0, 1, 3, 2).reshape(depth, bp, S5_GROUPS, S5_STATE, 2)
    s5_s = s5_s.transpose(0, 2, 1).reshape(depth, bs, S5_GROUPS, S5_STATE, 2)
    rw_p_heads = _unpack_rwkv_state(rw_p.reshape(depth * bp, RWKV_PAIRS, LANE, LANE))
    rw_p_heads = rw_p_heads.reshape(depth, bp, RWKV_HEADS, RWKV_HEAD, RWKV_HEAD)
    return (xp.reshape(bp, tp, d), xs.reshape(bs, ts, d), mk, mv, s5_p, s5_s, gla_p, gla_s, rw_p_heads, rw_s,
            sh_p, sh_s)
```

```python
import functools
import math

import jax
import jax.numpy as jnp
from jax import lax
from jax.experimental import pallas as pl
from jax.experimental.pallas import tpu as pltpu

f32 = jnp.float32
bf16 = jnp.bfloat16

D_MODEL = 4096
DEPTH = 4
N_MEM = 256
NORM_EPS = 1e-6

S5_WIDTH = 1024
S5_GROUP = 16
S5_GROUPS = 64
S5_STATE = 64
S5_NS = S5_GROUPS * S5_STATE
S5_BLOCKS = 8
S5_LANES = 512

GLA_WIDTH = 1024
GLA_HEADS = 4
GLA_DV = 256
GLA_DK = 128
GLA_QK = 512
GLA_GATE_RANK = 16
GLA_GATE_TAU = 16.0

RWKV_WIDTH = 1536
RWKV_HEAD = 64
RWKV_HEADS = 24
RWKV_PAIRS = 12
RWKV_GROUP = 12
RWKV_GN_EPS = RWKV_HEAD * 1e-5
RWKV_SHIFT_COLS = 3 * RWKV_WIDTH + 128

XA_HEADS = 4
XA_HEAD_DIM = 128
XA_WIDTH = 512

LANE = 128
SUBLANE = 8
VMEM_LIMIT = 56 * 1024 * 1024

H_COLS = 12288
COL_RW_R, COL_RW_K, COL_RW_V, COL_RW_GATE = 0, 1536, 3072, 4608
COL_S5_U, COL_S5_GATE, COL_G_V, COL_G_GATE = 6144, 7168, 8192, 9216
COL_G_Q, COL_G_K, COL_XA_Q = 10240, 10752, 11264
COL_RW_WA, COL_G_LR = 11776, 11904


def _mm(a, b):
    return jnp.dot(a.astype(bf16), b.astype(bf16), preferred_element_type=f32)


def _mm_nt(a, b):
    return lax.dot_general(a.astype(bf16), b.astype(bf16), (((1,), (1,)), ((), ())), preferred_element_type=f32)


def _mm_tn(a, b):
    return lax.dot_general(a.astype(bf16), b.astype(bf16), (((0,), (0,)), ((), ())), preferred_element_type=f32)


def _split2(x):
    hi = x.astype(bf16)
    return hi, (x - hi.astype(f32)).astype(bf16)


def _mm_mask_l(mask01, x):
    hi, lo = _split2(x)
    m = mask01.astype(bf16)
    return jnp.dot(m, hi, preferred_element_type=f32) + jnp.dot(m, lo, preferred_element_type=f32)


def _mix_dtype(t):
    return bf16 if t % (2 * SUBLANE) == 0 else f32


def _params(*sem):
    return pltpu.CompilerParams(dimension_semantics=sem, vmem_limit_bytes=VMEM_LIMIT)


def _chunk_tril(n, c, strict=False):
    r = lax.broadcasted_iota(jnp.int32, (n, n), 0)
    q = lax.broadcasted_iota(jnp.int32, (n, n), 1)
    same = (r // c) == (q // c)
    return same & ((q < r) if strict else (q <= r))


def _norm_mm_kernel(x_ref, g_ref, w_ref, o_ref, xn_ref):
    @pl.when(pl.program_id(1) == 0)
    def _():
        rows = 256
        for r0 in range(0, x_ref.shape[0], rows):
            x = x_ref[r0:r0 + rows, :]
            ms = jnp.mean(x * x, axis=-1, keepdims=True)
            xn_ref[r0:r0 + rows, :] = (x * lax.rsqrt(ms + NORM_EPS) * g_ref[...]).astype(bf16)

    o_ref[...] = lax.dot_general(xn_ref[...], w_ref[...], (((1,), (1,)), ((), ())), preferred_element_type=f32)


def _norm_mm(x, g, wt, l):
    m, d = x.shape
    n = wt.shape[1]
    tm = min(m, 1024)
    tn = 768 if n % 768 == 0 else 512
    xmode = dict(pipeline_mode=pl.Buffered(1))
    return pl.pallas_call(
        _norm_mm_kernel,
        grid=(m // tm, n // tn),
        in_specs=[pl.BlockSpec((tm, d), lambda i, j: (i, 0), **xmode),
                  pl.BlockSpec((1, d), lambda i, j: (0, 0)),
                  pl.BlockSpec((None, tn, d), lambda i, j: (l, j, 0))],
        out_specs=pl.BlockSpec((tm, tn), lambda i, j: (i, j)),
        out_shape=jax.ShapeDtypeStruct((m, n), f32),
        scratch_shapes=[pltpu.VMEM((tm, d), bf16)],
        compiler_params=_params("parallel", "arbitrary"),
        name="norm_mm",
    )(x, g, wt)


def _proj_kernel(xn_ref, w_ref, o_ref):
    o_ref[...] = lax.dot_general(xn_ref[...], w_ref[...], (((1,), (1,)), ((), ())), preferred_element_type=f32)


def _proj(xn, wt, l):
    m, d = xn.shape
    n = wt.shape[1]
    tm = min(m, 1024)
    tn = 768 if n % 768 == 0 else 512
    return pl.pallas_call(
        _proj_kernel,
        grid=(m // tm, n // tn),
        in_specs=[pl.BlockSpec((tm, d), lambda i, j: (i, 0)),
                  pl.BlockSpec((None, tn, d), lambda i, j: (l, j, 0))],
        out_specs=pl.BlockSpec((tm, tn), lambda i, j: (i, j)),
        out_shape=jax.ShapeDtypeStruct((m, n), f32),
        compiler_params=_params("parallel", "arbitrary"),
        name="proj",
    )(xn, wt)


def _outproj_kernel(a1, a2, a3, a4, w_ref, g_ref, gn_ref, x_ref, o_ref, *xn_ref, nj, tn):
    j = pl.program_id(1)
    acc = None
    k0 = 0
    for a in (a1, a2, a3, a4):
        part = _mm(a[...], w_ref[k0:k0 + a.shape[1], :])
        acc = part if acc is None else acc + part
        k0 += a.shape[1]
    o_ref[:, pl.ds(pl.multiple_of(j * tn, tn), tn)] = acc

    @pl.when(j == nj - 1)
    def _():
        cols = [slice(k * tn, (k + 1) * tn) for k in range(nj)]
        ssq = None
        for sl in cols:
            p = o_ref[:, sl]
            s = jnp.sum(p * p, axis=-1, keepdims=True)
            ssq = s if ssq is None else ssq + s
        inv = lax.rsqrt(ssq * (1.0 / D_MODEL) + NORM_EPS)
        ssq = None
        for sl in cols:
            new = x_ref[:, sl] + o_ref[:, sl] * inv * g_ref[:, sl]
            o_ref[:, sl] = new
            s = jnp.sum(new * new, axis=-1, keepdims=True)
            ssq = s if ssq is None else ssq + s
        if xn_ref:
            inv = lax.rsqrt(ssq * (1.0 / D_MODEL) + NORM_EPS)
            for sl in cols:
                xn_ref[0][:, sl] = (o_ref[:, sl] * inv * gn_ref[:, sl]).astype(bf16)


def _outproj(parts, w, l, g, x, g_next, tn=512):
    m, d = x.shape
    nj = d // tn
    tm = 512 if parts[0].dtype == bf16 else 256
    emit = g_next is not None
    row_spec = pl.BlockSpec((tm, d), lambda i, j: (i, 0))
    in_specs = [pl.BlockSpec((tm, p.shape[1]), lambda i, j: (i, 0)) for p in parts]
    in_specs += [pl.BlockSpec((None, d, tn), lambda i, j: (l, 0, j))]
    in_specs += [pl.BlockSpec((1, d), lambda i, j: (0, 0)), pl.BlockSpec((1, d), lambda i, j: (0, 0)),
                 pl.BlockSpec((tm, d), lambda i, j: (i, 0), pipeline_mode=pl.Buffered(1))]
    out = pl.pallas_call(
        functools.partial(_outproj_kernel, nj=nj, tn=tn),
        grid=(m // tm, nj),
        in_specs=in_specs,
        out_specs=[row_spec, row_spec] if emit else [row_spec],
        out_shape=[jax.ShapeDtypeStruct((m, d), f32)] + ([jax.ShapeDtypeStruct((m, d), bf16)] if emit else []),
        compiler_params=_params("parallel", "arbitrary"),
        name="outproj",
    )(*parts, w, g, g_next if emit else g, x)
    return (out[0], out[1]) if emit else (out[0], None)


def _s5_kernel(u_ref, gate_ref, h0_ref, lre_ref, lim_ref, step_ref, bre_ref, bim_ref, cre_ref, cim_ref,
               d_ref, wglu_ref, bglu_ref, y_ref, hl_ref, sre, sim, car, cai, *, tt, nj, chain):
    j = pl.program_id(1)
    nrt = tt // SUBLANE
    lr = lre_ref[...]
    li = lim_ref[...]
    st = jnp.exp(step_ref[...])
    kidx = (lax.broadcasted_iota(jnp.int32, (SUBLANE, S5_NS), 0) + 1).astype(f32)
    mag = jnp.exp(kidx * (lr * st))
    ang = kidx * (li * st)
    pw_re = mag * jnp.cos(ang)
    pw_im = mag * jnp.sin(ang)
    ab_re = pw_re[0:1]
    ab_im = pw_im[0:1]
    den = lr * lr + li * li
    f_re = ((ab_re - 1.0) * lr + ab_im * li) / den
    f_im = (ab_im * lr - (ab_re - 1.0) * li) / den

    u = u_ref[...]
    for blk in range(S5_BLOCKS):
        cols = slice(blk * S5_LANES, (blk + 1) * S5_LANES)
        fr = f_re[:, cols]
        fi = f_im[:, cols]
        b_re = bre_ref[blk]
        b_im = bim_ref[blk]
        ub = u[:, blk * LANE:(blk + 1) * LANE].astype(bf16)
        sre[:, cols] = _mm(ub, fr * b_re - fi * b_im)
        sim[:, cols] = _mm(ub, fr * b_im + fi * b_re)

    @pl.when(j == 0)
    def _():
        if chain:
            car[...] = jnp.broadcast_to(h0_ref[0, 0:1], (SUBLANE, S5_NS))
            cai[...] = jnp.broadcast_to(h0_ref[0, 1:2], (SUBLANE, S5_NS))
        else:
            car[...] = h0_ref[pl.ds(0, S5_NS, stride=2), :].T
            cai[...] = h0_ref[pl.ds(1, S5_NS, stride=2), :].T

    rowi = lax.broadcasted_iota(jnp.int32, (SUBLANE, S5_LANES), 0)
    for ch in range(S5_NS // S5_LANES):
        cols = slice(ch * S5_LANES, (ch + 1) * S5_LANES)
        p_re = pw_re[:, cols]
        p_im = pw_im[:, cols]
        steps = [(d, jnp.where(rowi >= d, p_re[d - 1:d], 0.0), jnp.where(rowi >= d, p_im[d - 1:d], 0.0))
                 for d in (1, 2, 4)]

        def body(r, carry, cols=cols, p_re=p_re, p_im=p_im, steps=steps):
            rows = pl.ds(pl.multiple_of(r * SUBLANE, SUBLANE), SUBLANE)
            xr = sre[rows, cols]
            xi = sim[rows, cols]
            for d, ar, ai in steps:
                sr = pltpu.roll(xr, d, 0)
                si = pltpu.roll(xi, d, 0)
                xr, xi = xr + ar * sr - ai * si, xi + ar * si + ai * sr
            if chain:
                cr, ci = carry
            else:
                seq = pl.ds(j * nrt + r, 1)
                cr = jnp.broadcast_to(car[seq, cols], (SUBLANE, S5_LANES))
                ci = jnp.broadcast_to(cai[seq, cols], (SUBLANE, S5_LANES))
            xr, xi = xr + p_re * cr - p_im * ci, xi + p_re * ci + p_im * cr
            sre[rows, cols] = xr
            sim[rows, cols] = xi
            if chain:
                return (jnp.broadcast_to(xr[SUBLANE - 1:SUBLANE], (SUBLANE, S5_LANES)),
                        jnp.broadcast_to(xi[SUBLANE - 1:SUBLANE], (SUBLANE, S5_LANES)))
            car[seq, cols] = xr[SUBLANE - 1:SUBLANE]
            cai[seq, cols] = xi[SUBLANE - 1:SUBLANE]
            return carry

        if chain:
            cr, ci = lax.fori_loop(0, nrt, body, (car[:, cols], cai[:, cols]))
            car[:, cols] = cr
            cai[:, cols] = ci
        else:
            lax.fori_loop(0, nrt, body, 0)

    if chain:
        hl_ref[0, 0:1] = car[0:1, :]
        hl_ref[0, 1:2] = cai[0:1, :]
    else:
        @pl.when(j == nj - 1)
        def _():
            hl_ref[pl.ds(0, S5_NS, stride=2), :] = car[...].T
            hl_ref[pl.ds(1, S5_NS, stride=2), :] = cai[...].T

    ys = []
    for blk in range(S5_BLOCKS):
        cols = slice(blk * S5_LANES, (blk + 1) * S5_LANES)
        ys.append(_mm(sre[:, cols], cre_ref[blk]) - _mm(sim[:, cols], cim_ref[blk]))
    y = jnp.concatenate(ys, axis=1) + d_ref[...] * u
    z = jax.nn.gelu(y)
    gl = jax.nn.sigmoid(_mm(z, wglu_ref[...]) + bglu_ref[...])
    y_ref[...] = (z * gl * jax.nn.silu(gate_ref[...])).astype(y_ref.dtype)


def _s5(h, h0, l0, p, nseq, t, chain):
    m = nseq * t
    if chain:
        tt = min(t, 512)
        nj = t // tt
        grid = (nseq, nj)
        row = lambda b, j: b * nj + j
        h_in = pl.BlockSpec((None, 1, 2, S5_NS), lambda b, j: (l0, b, 0, 0))
        h_out = pl.BlockSpec((1, 2, S5_NS), lambda b, j: (b, 0, 0))
        h_shape = jax.ShapeDtypeStruct((nseq, 2, S5_NS), f32)
        carry_rows = SUBLANE
    else:
        assert t == SUBLANE and nseq == LANE
        tt = 64
        nj = m // tt
        grid = (1, nj)
        row = lambda b, j: j
        h_in = pl.BlockSpec((None, 2 * S5_NS, nseq), lambda b, j: (l0, 0, 0))
        h_out = pl.BlockSpec((2 * S5_NS, nseq), lambda b, j: (0, 0))
        h_shape = jax.ShapeDtypeStruct((2 * S5_NS, nseq), f32)
        carry_rows = nseq
    cu, cg = COL_S5_U // S5_WIDTH, COL_S5_GATE // S5_WIDTH
    vec = lambda n: pl.BlockSpec((1, n), lambda b, j: (0, 0))
    full3 = lambda s: pl.BlockSpec(s, lambda b, j: (0, 0, 0))
    return pl.pallas_call(
        functools.partial(_s5_kernel, tt=tt, nj=nj, chain=chain),
        grid=grid,
        in_specs=[pl.BlockSpec((tt, S5_WIDTH), lambda b, j: (row(b, j), cu)),
                  pl.BlockSpec((tt, S5_WIDTH), lambda b, j: (row(b, j), cg)),
                  h_in, vec(S5_NS), vec(S5_NS), vec(S5_NS),
                  full3((S5_BLOCKS, LANE, S5_LANES)), full3((S5_BLOCKS, LANE, S5_LANES)),
                  full3((S5_BLOCKS, S5_LANES, LANE)), full3((S5_BLOCKS, S5_LANES, LANE)),
                  vec(S5_WIDTH), pl.BlockSpec((S5_WIDTH, S5_WIDTH), lambda b, j: (0, 0)), vec(S5_WIDTH)],
        out_specs=[pl.BlockSpec((tt, S5_WIDTH), lambda b, j: (row(b, j), 0)), h_out],
        out_shape=[jax.ShapeDtypeStruct((m, S5_WIDTH), _mix_dtype(t)), h_shape],
        scratch_shapes=[pltpu.VMEM((tt, S5_NS), f32), pltpu.VMEM((tt, S5_NS), f32),
                        pltpu.VMEM((carry_rows, S5_NS), f32), pltpu.VMEM((carry_rows, S5_NS), f32)],
        compiler_params=_params("parallel" if chain else "arbitrary", "arbitrary"),
        name="s5_chain" if chain else "s5_tiles",
    )(h, h, h0, p["lam_re"], p["lam_im"], p["log_step"], p["b_re"], p["b_im"], p["c_re"], p["c_im"],
      p["d"], p["w_glu"], p["b_glu"])


def _gla_kernel(q_ref, k_ref, v_ref, gate_ref, glr_ref, wg_ref, bg_ref, ng_ref, s0_ref, acc_ref, y_ref, s_ref,
                st_ref, b_ref, *, tt, c, nj, chain):
    del acc_ref
    j = pl.program_id(1)

    if chain:
        @pl.when(j == 0)
        def _():
            for hd in range(GLA_HEADS):
                st_ref[hd] = s0_ref[0, hd].T

    la = jax.nn.log_sigmoid(_mm(glr_ref[...], wg_ref[...]) + bg_ref[...]) * (1.0 / GLA_GATE_TAU)
    b_ref[...] = _mm_mask_l(jnp.where(_chunk_tril(tt, c), 1.0, 0.0), la)

    rowi = lax.broadcasted_iota(jnp.int32, (c, GLA_DK), 0)
    lanei = lax.broadcasted_iota(jnp.int32, (c, GLA_DK), 1)
    ones = jnp.ones((GLA_DK, GLA_DK), bf16)
    vzero = jnp.zeros((GLA_DK - c, GLA_DV), f32)
    scale = GLA_DK ** -0.5
    heads = range(GLA_HEADS)
    kl = [slice(hd * GLA_DK, (hd + 1) * GLA_DK) for hd in heads]
    vl = [slice(hd * GLA_DV, (hd + 1) * GLA_DV) for hd in heads]

    def chunk(ci, carry):
        rows = pl.ds(pl.multiple_of(ci * c, c), c)
        qh = [q_ref[rows, kl[hd]] * scale for hd in heads]
        kh = [k_ref[rows, kl[hd]] for hd in heads]
        vh = [v_ref[rows, vl[hd]] for hd in heads]
        bh = [b_ref[rows, kl[hd]] for hd in heads]
        prod = [jnp.concatenate(
            [qh[hd] * kh[hd][jj:jj + 1] * jnp.exp(jnp.where(rowi >= jj, bh[hd] - bh[hd][jj:jj + 1], -jnp.inf))
             for jj in range(c)], axis=0) for hd in heads]
        sums = [_mm(prod[hd], ones) for hd in heads]
        att = []
        for hd in heads:
            a = jnp.zeros((c, GLA_DK), f32)
            for jj in range(c):
                a = a + jnp.where(lanei == jj, sums[hd][jj * c:(jj + 1) * c], 0.0)
            att.append(a)
        s_t = [st_ref[hd] if chain else s0_ref[ci, hd].T for hd in heads]
        o_in = [_mm(att[hd], jnp.concatenate([vh[hd], vzero], axis=0)) for hd in heads]
        o_st = [_mm_nt(qh[hd] * jnp.exp(bh[hd]), s_t[hd]) for hd in heads]
        b_last = [bh[hd][c - 1:c] for hd in heads]
        upd = [_mm_tn(vh[hd], kh[hd] * jnp.exp(b_last[hd] - bh[hd])) for hd in heads]
        for hd in heads:
            s_new = s_t[hd] * jnp.exp(b_last[hd]) + upd[hd]
            if chain:
                st_ref[hd] = s_new
            else:
                s_ref[ci, hd] = s_new.T
            o = o_in[hd] + o_st[hd]
            on = o * lax.rsqrt(jnp.mean(o * o, axis=-1, keepdims=True) + NORM_EPS) * ng_ref[:, vl[hd]]
            y_ref[rows, vl[hd]] = (on * jax.nn.silu(gate_ref[rows, vl[hd]])).astype(y_ref.dtype)
        return carry

    lax.fori_loop(0, tt // c, chunk, 0, unroll=math.gcd(tt // c, 8))

    if chain:
        @pl.when(j == nj - 1)
        def _():
            for hd in range(GLA_HEADS):
                s_ref[0, hd] = st_ref[hd].T


def _stacked(acc, shape):
    if acc is None:
        return jnp.zeros((SUBLANE, LANE), f32), None
    assert acc.shape == shape
    return acc, 1


def _gla(h, s0, l0, acc, l, depth, p, nseq, t, chain):
    m = nseq * t
    acc_shape = (depth, nseq, GLA_HEADS, GLA_DK, GLA_DV)
    acc, alias_to = _stacked(acc, acc_shape)
    if chain:
        c = 16
        tt = min(t, 128)
        nj = t // tt
        grid = (nseq, nj)
        row = lambda b, j: b * nj + j
        st_spec = lambda li: pl.BlockSpec((None, 1, GLA_HEADS, GLA_DK, GLA_DV), lambda b, j: (li, b, 0, 0, 0))
    else:
        c = t
        tt = 8 * c
        nj = m // tt
        grid = (1, nj)
        row = lambda b, j: j
        st_spec = lambda li: pl.BlockSpec((None, tt // c, GLA_HEADS, GLA_DK, GLA_DV), lambda b, j: (li, j, 0, 0, 0))
    vec = lambda n: pl.BlockSpec((1, n), lambda b, j: (0, 0))
    return pl.pallas_call(
        functools.partial(_gla_kernel, tt=tt, c=c, nj=nj, chain=chain),
        grid=grid,
        in_specs=[pl.BlockSpec((tt, GLA_QK), lambda b, j: (row(b, j), COL_G_Q // GLA_QK)),
                  pl.BlockSpec((tt, GLA_QK), lambda b, j: (row(b, j), COL_G_K // GLA_QK)),
                  pl.BlockSpec((tt, GLA_WIDTH), lambda b, j: (row(b, j), COL_G_V // GLA_WIDTH)),
                  pl.BlockSpec((tt, GLA_WIDTH), lambda b, j: (row(b, j), COL_G_GATE // GLA_WIDTH)),
                  pl.BlockSpec((tt, LANE), lambda b, j: (row(b, j), COL_G_LR // LANE)),
                  pl.BlockSpec((LANE, GLA_QK), lambda b, j: (0, 0)), vec(GLA_QK), vec(GLA_WIDTH), st_spec(l0),
                  pl.BlockSpec(memory_space=pl.ANY)],
        out_specs=[pl.BlockSpec((tt, GLA_WIDTH), lambda b, j: (row(b, j), 0)), st_spec(l)],
        out_shape=[jax.ShapeDtypeStruct((m, GLA_WIDTH), _mix_dtype(t)), jax.ShapeDtypeStruct(acc_shape, f32)],
        scratch_shapes=[pltpu.VMEM((GLA_HEADS, GLA_DV, GLA_DK), f32), pltpu.VMEM((tt, GLA_QK), f32)],
        input_output_aliases={} if alias_to is None else {9: alias_to},
        compiler_params=_params("parallel" if chain else "arbitrary", "arbitrary"),
        name="gla_chain" if chain else "gla_tiles",
    )(h, h, h, h, h, p["w_gate"], p["b_gate"], p["norm_g"], s0, acc)


def _seg_sum(x):
    n = x.shape[0]
    w = 2 * LANE
    lane = lax.broadcasted_iota(jnp.int32, (w, w), 1)
    sub = lax.broadcasted_iota(jnp.int32, (w, w), 0)
    ones_bd = jnp.where((lane // RWKV_HEAD) == (sub // RWKV_HEAD), 1.0, 0.0)
    stacked = jnp.concatenate([x[:, i * w:(i + 1) * w] for i in range(RWKV_WIDTH // w)], axis=0)
    s = _mm(stacked, ones_bd)
    return jnp.concatenate([s[i * n:(i + 1) * n] for i in range(RWKV_WIDTH // w)], axis=1)


def _rwkv_kernel(r_ref, k_ref, v_ref, gate_ref, wa_ref, pr_ref, pk_ref, pv_ref, pwa_ref,
                 mur_ref, muk_ref, muv_ref, muwa_ref, w0_ref, w2_ref, a0_ref, a2_ref, kk_ref, ka_ref, rk_ref,
                 lnw_ref, lnb_ref, s0_ref, acc_ref, y_ref, s_ref,
                 st, c_r, c_k, c_v, c_wa, x_kkg, x_rg, x_bi, x_ki, x_bend, x_kend, x_gam, x_v, x_bon,
                 x_tk, x_rq, x_uv, x_yv, x_y, *, tt, c, nj, chain):
    del acc_ref
    j = pl.program_id(1)
    hd = RWKV_HEAD

    if chain:
        @pl.when(j == 0)
        def _():
            st[...] = s0_ref[0]
            c_r[...] = jnp.broadcast_to(pr_ref[0], c_r.shape)
            c_k[...] = jnp.broadcast_to(pk_ref[0], c_k.shape)
            c_v[...] = jnp.broadcast_to(pv_ref[0], c_v.shape)
            c_wa[...] = jnp.broadcast_to(pwa_ref[0], c_wa.shape)
    else:
        @pl.when(j == 0)
        def _():
            st[...] = jnp.zeros(st.shape, f32)

    def shift(x_ref, c_ref, p_ref, mu_ref):
        x = x_ref[...]
        rowid = lax.broadcasted_iota(jnp.int32, x.shape, 0)
        rolled = pltpu.roll(x, 1, 0)
        if chain:
            prev = jnp.where(rowid == 0, c_ref[0:1, :], rolled)
            c_ref[...] = jnp.broadcast_to(x[tt - 1:tt, :], c_ref.shape)
        else:
            prev = jnp.where(rowid % c == 0, p_ref[...], rolled)
        return x + (prev - x) * mu_ref[...]

    r = shift(r_ref, c_r, pr_ref, mur_ref)
    k = shift(k_ref, c_k, pk_ref, muk_ref)
    v = shift(v_ref, c_v, pv_ref, muv_ref)
    wa = shift(wa_ref, c_wa, pwa_ref, muwa_ref)

    w_log = -jax.nn.softplus(-(w0_ref[...] + _mm(jnp.tanh(wa), w2_ref[...]))) - 0.5
    lw = -jnp.exp(w_log)
    a = jax.nn.sigmoid(a0_ref[...] + _mm(wa, a2_ref[...]))

    lane = lax.broadcasted_iota(jnp.int32, (LANE, LANE), 1)
    sub = lax.broadcasted_iota(jnp.int32, (LANE, LANE), 0)
    bd = (lane // RWKV_HEAD) == (sub // RWKV_HEAD)

    kk = k * kk_ref[...]
    kk = kk / jnp.maximum(jnp.sqrt(_seg_sum(kk * kk)), 1e-12)
    k2 = k * (1.0 + (a - 1.0) * ka_ref[...])
    bv = kk * a
    ri = lax.broadcasted_iota(jnp.int32, (tt, tt), 0)
    qi = lax.broadcasted_iota(jnp.int32, (tt, tt), 1)
    same = (ri // c) == (qi // c)
    strict = same & (qi < ri)
    incl = same & (qi <= ri)
    eye = jnp.where(qi == ri, 1.0, 0.0).astype(f32)
    cl = _mm_mask_l(jnp.where(incl, 1.0, 0.0), lw)
    if c == tt:
        cll = jnp.broadcast_to(cl[tt - 1:tt, :], cl.shape)
    else:
        cll = _mm_mask_l(jnp.where(same, 1.0, 0.0), lw)
    ginv = jnp.exp(-cl)
    gend = jnp.exp(cll - cl)
    x_kkg[...] = kk * jnp.exp(cl - lw)
    x_rg[...] = r * jnp.exp(cl)
    x_bi[...] = bv * ginv
    x_ki[...] = k2 * ginv
    x_bend[...] = bv * gend
    x_kend[...] = k2 * gend
    x_gam[...] = jnp.exp(cll)
    x_v[...] = v
    x_bon[...] = _seg_sum(r * k2 * rk_ref[...]) * v

    head_a = lax.broadcasted_iota(jnp.int32, (tt, LANE), 1) < RWKV_HEAD

    zero_tt = jnp.zeros((tt, tt), f32)

    def tri_inv(ams):
        def bdiag(x):
            return jnp.concatenate([jnp.concatenate([x[:, 0:tt], zero_tt], axis=1),
                                    jnp.concatenate([zero_tt, x[:, tt:2 * tt]], axis=1)], axis=0)

        if tt % LANE:
            ns = [-am for am in ams]
            ts = [eye + n for n in ns]
            if c > 2:
                ns = [_mm(n, n) for n in ns]
                width = 4
                while width < c:
                    both = [_mm(jnp.concatenate([t, n], axis=0), n) for t, n in zip(ts, ns)]
                    ts = [t + b[0:tt] for t, b in zip(ts, both)]
                    ns = [b[tt:2 * tt] for b in both]
                    width *= 2
                ts = [t + _mm(t, n) for t, n in zip(ts, ns)]
            return [(ts[i], ts[i + 1]) for i in range(0, len(ts), 2)]
        ns = [jnp.concatenate([-ams[i], -ams[i + 1]], axis=1) for i in range(0, len(ams), 2)]
        eye2 = jnp.concatenate([eye, eye], axis=1)
        ts = [eye2 + n for n in ns]
        if c > 2:
            ns = [_mm(n, bdiag(n)) for n in ns]
            width = 4
            while width < c:
                both = [_mm(jnp.concatenate([t, n], axis=0), bdiag(n)) for t, n in zip(ts, ns)]
                ts = [t + b[0:tt] for t, b in zip(ts, both)]
                ns = [b[tt:2 * tt] for b in both]
                width *= 2
            last = [_mm(t, bdiag(n)) for t, n in zip(ts, ns)]
            ts = [t + m for t, m in zip(ts, last)]
        return ts

    wide = tt % LANE == 0

    def join(xa, xb):
        return jnp.concatenate([xa, xb], axis=1) if wide else (xa, xb)

    def pair_mm(lhs, rhs):
        m = (lax.broadcasted_iota(jnp.int32, (1, rhs.shape[1]), 1) % LANE) < RWKV_HEAD
        if wide:
            return _mm(lhs, jnp.concatenate([jnp.where(m, rhs, 0.0), jnp.where(m, 0.0, rhs)], axis=0))
        return jnp.where(m, _mm(lhs[0], rhs), _mm(lhs[1], rhs))

    def lanes(p):
        return slice(p * LANE, (p + 1) * LANE)

    for g0 in range(0, RWKV_PAIRS, RWKV_GROUP):
        grp = list(range(g0, g0 + RWKV_GROUP))
        kkg = [x_kkg[:, lanes(p)] for p in grp]
        rg = [x_rg[:, lanes(p)] for p in grp]
        vp = [x_v[:, lanes(p)] for p in grp]
        lhs = [jnp.concatenate([jnp.where(head_a, a, 0.0), jnp.where(head_a, 0.0, a),
                                jnp.where(head_a, b, 0.0), jnp.where(head_a, 0.0, b)], axis=0)
               for a, b in zip(kkg, rg)]
        if tt % LANE == 0:
            pbk = [_mm_nt(x, jnp.concatenate([x_bi[:, lanes(p)], x_ki[:, lanes(p)]], axis=0))
                   for x, p in zip(lhs, grp)]
            pb = [x[:, 0:tt] for x in pbk]
            pk = [x[:, tt:2 * tt] for x in pbk]
        else:
            pb = [_mm_nt(x, x_bi[:, lanes(p)]) for x, p in zip(lhs, grp)]
            pk = [_mm_nt(x, x_ki[:, lanes(p)]) for x, p in zip(lhs, grp)]
        def blk(x, hh, mask):
            return jnp.where(mask, x[hh * tt:(hh + 1) * tt], 0.0)

        tinv = tri_inv([blk(x, hh, strict) for x in pb for hh in (0, 1)])
        ar = [join(blk(x, 2, incl), blk(x, 3, incl)) for x in pb]
        bmr = [join(jnp.concatenate([blk(x, 0, strict), blk(x, 2, incl)], axis=0),
                    jnp.concatenate([blk(x, 1, strict), blk(x, 3, incl)], axis=0)) for x in pk]
        mv = [pair_mm(x, vp[i]) for i, x in enumerate(bmr)]
        mt = [pair_mm(x, jnp.concatenate([mv[i][0:tt], kkg[i]], axis=1)) for i, x in enumerate(tinv)]
        uv = [-x[:, 0:LANE] for x in mt]
        tk = [x[:, LANE:2 * LANE] for x in mt]
        mr = [pair_mm(x, jnp.concatenate([uv[i], tk[i]], axis=1)) for i, x in enumerate(ar)]
        for i, p in enumerate(grp):
            x_uv[:, lanes(p)] = uv[i]
            x_tk[:, lanes(p)] = tk[i]
            x_yv[:, lanes(p)] = mr[i][:, 0:LANE] + mv[i][tt:2 * tt]
            x_rq[:, lanes(p)] = rg[i] - mr[i][:, LANE:2 * LANE]

    def chunk(ci, carry):
        row0 = pl.multiple_of(ci * c, c)
        rows = pl.ds(row0, c)
        pairs = range(RWKV_PAIRS)
        if not chain:
            for p in pairs:
                st[p, 0:hd, 0:hd] = s0_ref[ci, 2 * p]
                st[p, hd:2 * hd, hd:2 * hd] = s0_ref[ci, 2 * p + 1]
        s = [st[p] for p in pairs]
        zy = [_mm_nt(jnp.concatenate([x_tk[rows, lanes(p)], x_rq[rows, lanes(p)]], axis=0), s[p]) for p in pairs]
        u = [x_uv[rows, lanes(p)] - zy[p][0:c] for p in pairs]
        for p in pairs:
            x_y[rows, lanes(p)] = zy[p][c:2 * c] + x_yv[rows, lanes(p)]
        upd = [_mm_tn(jnp.concatenate([u[p], x_v[rows, lanes(p)]], axis=0),
                      jnp.concatenate([x_bend[rows, lanes(p)], x_kend[rows, lanes(p)]], axis=0)) for p in pairs]
        for p in pairs:
            s_new = s[p] * x_gam[pl.ds(row0, 1), lanes(p)] + jnp.where(bd, upd[p], 0.0)
            if chain:
                st[p] = s_new
            else:
                s_ref[ci, 2 * p] = s_new[0:hd, 0:hd]
                s_ref[ci, 2 * p + 1] = s_new[hd:2 * hd, hd:2 * hd]
        return carry

    lax.fori_loop(0, tt // c, chunk, 0, unroll=1 if chain else 2)

    y = x_y[...]
    mean = _seg_sum(y) * (1.0 / RWKV_HEAD)
    yc = y - mean
    var = _seg_sum(yc * yc) * (1.0 / RWKV_HEAD)
    yn = yc * lax.rsqrt(var + RWKV_GN_EPS) * lnw_ref[...] + lnb_ref[...] + x_bon[...]
    y_ref[...] = (yn * jax.nn.silu(gate_ref[...])).astype(y_ref.dtype)

    if chain:
        @pl.when(j == nj - 1)
        def _():
            s_ref[0] = st[...]


def _rwkv(h, prev, s0, l0, acc, l, depth, p, nseq, t, chain):
    m = nseq * t
    w = RWKV_WIDTH
    acc_shape = (depth, nseq) + ((RWKV_PAIRS, LANE, LANE) if chain else (RWKV_HEADS, RWKV_HEAD, RWKV_HEAD))
    acc, alias_to = _stacked(acc, acc_shape)
    if chain:
        tt = min(t, 128)
        c = tt
        nj = t // tt
        grid = (nseq, nj)
        row = lambda b, j: b * nj + j
        prev = tuple(x[:, None, :] for x in prev)
        prow = lambda n: pl.BlockSpec((1, 1, n), lambda b, j: (b, 0, 0))
        st_spec = lambda li: pl.BlockSpec((None, 1, RWKV_PAIRS, LANE, LANE), lambda b, j: (li, b, 0, 0, 0))
    else:
        c = t
        tt = 64
        nj = m // tt
        grid = (1, nj)
        row = lambda b, j: j
        prev = tuple(jnp.repeat(x, t, axis=0) for x in prev)
        prow = lambda n: pl.BlockSpec((tt, n), lambda b, j: (j, 0))
        st_spec = lambda li: pl.BlockSpec((None, tt // c, RWKV_HEADS, RWKV_HEAD, RWKV_HEAD),
                                          lambda b, j: (li, j, 0, 0, 0))
    vec = lambda n: pl.BlockSpec((1, n), lambda b, j: (0, 0))
    tok = lambda col: pl.BlockSpec((tt, w), lambda b, j: (row(b, j), col // w))
    big = lambda: pltpu.VMEM((tt, w), f32)
    return pl.pallas_call(
        functools.partial(_rwkv_kernel, tt=tt, c=c, nj=nj, chain=chain),
        grid=grid,
        in_specs=[tok(COL_RW_R), tok(COL_RW_K), tok(COL_RW_V), tok(COL_RW_GATE),
                  pl.BlockSpec((tt, LANE), lambda b, j: (row(b, j), COL_RW_WA // LANE)),
                  prow(w), prow(w), prow(w), prow(LANE),
                  vec(w), vec(w), vec(w), vec(LANE),
                  vec(w), pl.BlockSpec((LANE, w), lambda b, j: (0, 0)), vec(w),
                  pl.BlockSpec((LANE, w), lambda b, j: (0, 0)), vec(w), vec(w), vec(w), vec(w), vec(w), st_spec(l0),
                  pl.BlockSpec(memory_space=pl.ANY)],
        out_specs=[pl.BlockSpec((tt, w), lambda b, j: (row(b, j), 0)), st_spec(l)],
        out_shape=[jax.ShapeDtypeStruct((m, w), _mix_dtype(t)), jax.ShapeDtypeStruct(acc_shape, f32)],
        input_output_aliases={} if alias_to is None else {23: alias_to},
        scratch_shapes=[pltpu.VMEM((RWKV_PAIRS, LANE, LANE), f32),
                        pltpu.VMEM((SUBLANE, w), f32), pltpu.VMEM((SUBLANE, w), f32),
                        pltpu.VMEM((SUBLANE, w), f32), pltpu.VMEM((SUBLANE, LANE), f32)] + [big() for _ in range(14)],
        compiler_params=_params("parallel" if chain else "arbitrary", "arbitrary"),
        name="rwkv_chain" if chain else "rwkv_tiles",
    )(h, h, h, h, h, *prev, p["mu_r"], p["mu_k"], p["mu_v"], p["mu_wa"], p["w0"], p["w2"], p["a0"], p["a2"],
      p["k_k"], p["k_a"], p["r_k"], p["ln_w"], p["ln_b"], s0, acc)


def _xattn_kernel(q_ref, k_ref, v_ref, o_ref, *, nb, tq):
    scale = XA_HEAD_DIM ** -0.5
    probs = [(i, hd) for i in range(nb) for hd in range(XA_HEADS)]
    ln = [slice(hd * XA_HEAD_DIM, (hd + 1) * XA_HEAD_DIM) for hd in range(XA_HEADS)]
    qrows = [slice(i * tq, (i + 1) * tq) for i in range(nb)]
    mrows = [pl.ds(hd, N_MEM, stride=XA_HEADS) for hd in range(XA_HEADS)]
    s = [_mm_nt(q_ref[qrows[i], ln[hd]] * scale, k_ref[i, mrows[hd], :]) for i, hd in probs]
    s = [x - jnp.max(x, axis=-1, keepdims=True) for x in s]
    e = [jnp.exp(x) for x in s]
    pr = [x / jnp.sum(x, axis=-1, keepdims=True) for x in e]
    o = [_mm(pr[n], v_ref[i, mrows[hd], :]) for n, (i, hd) in enumerate(probs)]
    for n, (i, hd) in enumerate(probs):
        o_ref[qrows[i], ln[hd]] = o[n].astype(o_ref.dtype)


def _xattn(h, mem_k, mem_v, l0, nseq, t):
    m = nseq * t
    tq = min(t, 512)
    nj = t // tq
    nb = 8 if (nj == 1 and nseq % 8 == 0) else 1
    mem_spec = pl.BlockSpec((None, nb, N_MEM * XA_HEADS, XA_HEAD_DIM), lambda b, j: (l0, b, 0, 0))
    return pl.pallas_call(
        functools.partial(_xattn_kernel, nb=nb, tq=tq),
        grid=(nseq // nb, nj),
        in_specs=[pl.BlockSpec((nb * tq, XA_WIDTH), lambda b, j: (b * nj + j, COL_XA_Q // XA_WIDTH)),
                  mem_spec, mem_spec],
        out_specs=pl.BlockSpec((nb * tq, XA_WIDTH), lambda b, j: (b * nj + j, 0)),
        out_shape=jax.ShapeDtypeStruct((m, XA_WIDTH), _mix_dtype(t)),
        compiler_params=_params("parallel", "arbitrary"),
        name="xattn",
    )(h, mem_k, mem_v)


_RW0 = 5136
W_IN_SEGMENTS = ((_RW0, _RW0 + 1536), (_RW0 + 1536, _RW0 + 3072), (_RW0 + 3072, _RW0 + 4608), (9872, 11408),
                 (0, 1024), (1024, 2048), (3072, 4096), (4096, 5120), (2048, 2560), (2560, 3072), (11408, 11920),
                 (_RW0 + 4608, _RW0 + 4736), (5120, 5136))


def _reorder_cast_kernel(w_ref, o_ref):
    dst = 0
    for a, b in W_IN_SEGMENTS:
        o_ref[dst:dst + (b - a), :] = w_ref[a:b, :].astype(bf16)
        dst += b - a
    o_ref[dst:, :] = jnp.zeros((H_COLS - dst, o_ref.shape[1]), bf16)


def _prep_w_in(w_in, tk=256):
    depth, d, n = w_in.shape
    return pl.pallas_call(
        _reorder_cast_kernel,
        grid=(depth, d // tk),
        in_specs=[pl.BlockSpec((None, n, tk), lambda l, i: (l, 0, i))],
        out_specs=pl.BlockSpec((None, H_COLS, tk), lambda l, i: (l, 0, i)),
        out_shape=jax.ShapeDtypeStruct((depth, H_COLS, d), bf16),
        compiler_params=_params("parallel", "parallel"),
        name="w_in_layout",
    )(jnp.swapaxes(w_in, 1, 2))


def _block_diag_s5(b, c):
    gl = S5_GROUPS // S5_BLOCKS
    eye = jnp.eye(gl, dtype=b.dtype)
    bb = b.reshape(S5_BLOCKS, gl, S5_STATE, S5_GROUP)
    bbd = jnp.einsum("kgpn,gh->kgnhp", bb, eye).reshape(S5_BLOCKS, gl * S5_GROUP, gl * S5_STATE)
    cc = c.reshape(S5_BLOCKS, gl, S5_GROUP, S5_STATE)
    cbd = jnp.einsum("kgnp,gh->kgphn", cc, eye).reshape(S5_BLOCKS, gl * S5_STATE, gl * S5_GROUP)
    return bbd, cbd


def _layer_params(l, a):
    row = lambda x: x.reshape(1, -1)
    bre, cre = _block_diag_s5(a["s5_b_re"][l], a["s5_c_re"][l])
    bim, cim = _block_diag_s5(a["s5_b_im"][l], a["s5_c_im"][l])
    s5 = dict(lam_re=row(a["s5_lam_re"][l]), lam_im=row(a["s5_lam_im"][l]),
              log_step=row(jnp.broadcast_to(a["s5_log_step"][l][:, None], (S5_GROUPS, S5_STATE))),
              b_re=bre, b_im=bim, c_re=cre.astype(bf16), c_im=cim.astype(bf16),
              d=row(a["s5_d"][l]), w_glu=a["s5_w_glu"][l].astype(bf16), b_glu=row(a["s5_b_glu"][l]))
    wg = jnp.zeros((LANE, GLA_QK), f32).at[:GLA_GATE_RANK].set(a["gla_w_gate"][l])
    gla = dict(w_gate=wg, b_gate=row(a["gla_b_gate"][l]), norm_g=row(a["gla_norm_g"][l]))
    mu = a["rw_mu"][l]
    w = RWKV_WIDTH
    z64 = jnp.zeros((64, w), f32)
    rw = dict(mu_r=row(mu[0:w]), mu_k=row(mu[w:2 * w]), mu_v=row(mu[2 * w:3 * w]), mu_wa=row(mu[3 * w:]),
              w0=row(a["rw_w0"][l]), w2=jnp.concatenate([a["rw_w2"][l], z64], 0).astype(bf16),
              a0=row(a["rw_a0"][l]), a2=jnp.concatenate([z64, a["rw_a2"][l]], 0).astype(bf16),
              k_k=row(a["rw_k_k"][l]), k_a=row(a["rw_k_a"][l]), r_k=row(a["rw_r_k"][l]),
              ln_w=row(a["rw_ln_w"][l]), ln_b=row(a["rw_ln_b"][l]))
    return s5, gla, rw


def _unpack_rwkv_state(s):
    b = s.shape[0]
    s = s.reshape(b, RWKV_PAIRS, 2, RWKV_HEAD, 2, RWKV_HEAD)
    return jnp.stack([s[:, :, 0, :, 0, :], s[:, :, 1, :, 1, :]], axis=2).reshape(b, RWKV_HEADS, RWKV_HEAD, RWKV_HEAD)


def _split_shift(x):
    w = RWKV_WIDTH
    return tuple(x[:, a:b] for a, b in ((0, w), (w, 2 * w), (2 * w, 3 * w), (3 * w, 3 * w + LANE)))


def _mixer_layer(x, xn, nseq, t, mem_k, mem_v, s_s5, s_gla, s_rw, l0, acc_gla, acc_rw, l, depth, prev, chain, lp,
                 w_in_t, w_out_b, g_pre, g_post, g_next):
    s5p, glap, rwp = lp
    h = _norm_mm(x, g_pre, w_in_t, l) if xn is None else _proj(xn, w_in_t, l)
    y_s5, s5_state = _s5(h, s_s5, l0, s5p, nseq, t, chain)
    y_gla, s_gla_new = _gla(h, s_gla, l0, acc_gla, l, depth, glap, nseq, t, chain)
    y_rw, s_rw_new = _rwkv(h, prev, s_rw, l0, acc_rw, l, depth, rwp, nseq, t, chain)
    y_xa = _xattn(h, mem_k, mem_v, l0, nseq, t)
    x_new, xn_new = _outproj((y_s5, y_gla, y_rw, y_xa), w_out_b, l, g_post, x, g_next)
    last = h.reshape(nseq, t, H_COLS)[:, t - 1]
    w = RWKV_WIDTH
    shift = jnp.concatenate([last[:, COL_RW_R:COL_RW_R + w], last[:, COL_RW_K:COL_RW_K + w],
                             last[:, COL_RW_V:COL_RW_V + w], last[:, COL_RW_WA:COL_RW_WA + LANE]], axis=1)
    return x_new, xn_new, s5_state, s_gla_new, s_rw_new, shift


def kernel(x_prompt, x_sample, mem_prompt, cache_mem_k, cache_mem_v, state_s5, state_gla, state_rwkv, state_rwkv_shift, g_pre, g_post, w_in, w_out, s5_lam_re, s5_lam_im, s5_log_step, s5_b_re, s5_b_im, s5_c_re, s5_c_im, s5_d, s5_w_glu, s5_b_glu, gla_w_gate, gla_b_gate, gla_norm_g, rw_mu, rw_w0, rw_w2, rw_a0, rw_a2, rw_k_k, rw_k_a, rw_r_k, rw_ln_w, rw_ln_b, mem_g, w_mk, w_mv):
    raw = dict(s5_lam_re=s5_lam_re, s5_lam_im=s5_lam_im, s5_log_step=s5_log_step, s5_b_re=s5_b_re, s5_b_im=s5_b_im,
               s5_c_re=s5_c_re, s5_c_im=s5_c_im, s5_d=s5_d, s5_w_glu=s5_w_glu, s5_b_glu=s5_b_glu,
               gla_w_gate=gla_w_gate, gla_b_gate=gla_b_gate, gla_norm_g=gla_norm_g, rw_mu=rw_mu, rw_w0=rw_w0,
               rw_w2=rw_w2, rw_a0=rw_a0, rw_a2=rw_a2, rw_k_k=rw_k_k, rw_k_a=rw_k_a, rw_r_k=rw_r_k,
               rw_ln_w=rw_ln_w, rw_ln_b=rw_ln_b)
    bp, tp, d = x_prompt.shape
    bs, ts, _ = x_sample.shape
    depth = w_in.shape[0]
    w_in_t = _prep_w_in(w_in)
    w_out_b = w_out.astype(bf16)
    w_kv_t = jnp.swapaxes(jnp.concatenate([w_mk, w_mv], axis=2), 1, 2).astype(bf16)
    mem_rows = N_MEM * XA_HEADS

    xp = x_prompt.reshape(bp * tp, d)
    xs = x_sample.reshape(bs * ts, d)
    mem2d = mem_prompt.reshape(bp * N_MEM, d)
    zp = lambda *s: jnp.zeros(s, f32)
    prev_p = _split_shift(zp(bp, RWKV_SHIFT_COLS))

    gla_p = gla_s = rw_p = rw_s = xnp = xns = None
    zero_gla = zp(1, bp, GLA_HEADS, GLA_DK, GLA_DV)
    zero_rw = zp(1, bp, RWKV_PAIRS, LANE, LANE)
    zero_s5 = zp(1, bp, 2, S5_NS)
    s5_in = state_s5.reshape(depth, bs, S5_NS * 2).transpose(0, 2, 1)
    mem_k_s = cache_mem_k.reshape(depth, bs, mem_rows, XA_HEAD_DIM)
    mem_v_s = cache_mem_v.reshape(depth, bs, mem_rows, XA_HEAD_DIM)

    outs = [[] for _ in range(6)]
    for l in range(depth):
        lp = _layer_params(l, raw)
        gp, gq = g_pre[l].reshape(1, d), g_post[l].reshape(1, d)
        gn = g_pre[l + 1].reshape(1, d) if l + 1 < depth else None
        kv = _norm_mm(mem2d, mem_g[l].reshape(1, d), w_kv_t, l)
        k_mem = kv[:, :XA_WIDTH].reshape(bp, N_MEM, XA_WIDTH)
        v_mem = kv[:, XA_WIDTH:].reshape(bp, N_MEM, XA_WIDTH)
        xp, xnp, h1, gla_p, rw_p, p1 = _mixer_layer(
            xp, xnp, bp, tp, k_mem.reshape(1, bp, mem_rows, XA_HEAD_DIM), v_mem.reshape(1, bp, mem_rows, XA_HEAD_DIM),
            zero_s5, zero_gla, zero_rw, 0, gla_p, rw_p, l, depth,
            prev_p, True, lp, w_in_t, w_out_b, gp, gq, gn)
        xs, xns, h2, gla_s, rw_s, p2 = _mixer_layer(
            xs, xns, bs, ts, mem_k_s, mem_v_s,
            s5_in, state_gla, state_rwkv, l, gla_s, rw_s, l, depth,
            _split_shift(state_rwkv_shift[l]), False, lp, w_in_t, w_out_b, gp, gq, gn)
        vals = (k_mem.reshape(bp, N_MEM, XA_HEADS, XA_HEAD_DIM), v_mem.reshape(bp, N_MEM, XA_HEADS, XA_HEAD_DIM),
                h1, h2, p1, p2)
        for o, val in zip(outs, vals):
            o.append(val)

    mk, mv, s5_p, s5_s, sh_p, sh_s = (jnp.stack(o) for o in outs)
    s5_p = s5_p.transpose(0, 1, 3, 2).reshape(depth, bp, S5_GROUPS, S5_STATE, 2)
    s5_s = s5_s.transpose(0, 2, 1).reshape(depth, bs, S5_GROUPS, S5_STATE, 2)
    rw_p_heads = _unpack_rwkv_state(rw_p.reshape(depth * bp, RWKV_PAIRS, LANE, LANE))
    rw_p_heads = rw_p_heads.reshape(depth, bp, RWKV_HEADS, RWKV_HEAD, RWKV_HEAD)
    return (xp.reshape(bp, tp, d), xs.reshape(bs, ts, d), mk, mv, s5_p, s5_s, gla_p, gla_s, rw_p_heads, rw_s,
            sh_p, sh_s)
```

```python
import functools
import math

import jax
import jax.numpy as jnp
from jax import lax
from jax.experimental import pallas as pl
from jax.experimental.pallas import tpu as pltpu

f32 = jnp.float32
bf16 = jnp.bfloat16

D_MODEL = 4096
N_MEM = 256
NORM_EPS = 1e-6

S5_WIDTH = 1024
S5_GROUP = 16
S5_GROUPS = 64
S5_STATE = 64
S5_NS = S5_GROUPS * S5_STATE
S5_BLOCKS = 8
S5_LANES = 512

GLA_WIDTH = 1024
GLA_HEADS = 4
GLA_DV = 256
GLA_DK = 128
GLA_QK = 512
GLA_GATE_RANK = 16
GLA_GATE_TAU = 16.0

RWKV_WIDTH = 1536
RWKV_HEAD = 64
RWKV_HEADS = 24
RWKV_PAIRS = 12
RWKV_GROUP = 12
RWKV_GN_EPS = RWKV_HEAD * 1e-5
RWKV_SHIFT_COLS = 3 * RWKV_WIDTH + 128

XA_HEADS = 4
XA_HEAD_DIM = 128
XA_WIDTH = 512

LANE = 128
SUBLANE = 8
VMEM_LIMIT = 56 * 1024 * 1024

H_COLS = 12288
COL_RW_R, COL_RW_K, COL_RW_V, COL_RW_GATE = 0, 1536, 3072, 4608
COL_S5_U, COL_S5_GATE, COL_G_V, COL_G_GATE = 6144, 7168, 8192, 9216
COL_G_Q, COL_G_K, COL_XA_Q = 10240, 10752, 11264
COL_RW_WA, COL_G_LR = 11776, 11904


def _mm(a, b):
    return jnp.dot(a.astype(bf16), b.astype(bf16), preferred_element_type=f32)


def _mm_nt(a, b):
    return lax.dot_general(a.astype(bf16), b.astype(bf16), (((1,), (1,)), ((), ())), preferred_element_type=f32)


def _mm_tn(a, b):
    return lax.dot_general(a.astype(bf16), b.astype(bf16), (((0,), (0,)), ((), ())), preferred_element_type=f32)


def _split2(x):
    hi = x.astype(bf16)
    return hi, (x - hi.astype(f32)).astype(bf16)


def _mm_mask_l(mask01, x):
    hi, lo = _split2(x)
    m = mask01.astype(bf16)
    return jnp.dot(m, hi, preferred_element_type=f32) + jnp.dot(m, lo, preferred_element_type=f32)


def _mix_dtype(t):
    return bf16 if t % (2 * SUBLANE) == 0 else f32


def _params(*sem):
    return pltpu.CompilerParams(dimension_semantics=sem, vmem_limit_bytes=VMEM_LIMIT)


def _chunk_tril(n, c):
    r = lax.broadcasted_iota(jnp.int32, (n, n), 0)
    q = lax.broadcasted_iota(jnp.int32, (n, n), 1)
    return ((r // c) == (q // c)) & (q <= r)


def _norm_mm_kernel(x_ref, g_ref, w_ref, o_ref, xn_ref):
    @pl.when(pl.program_id(1) == 0)
    def _():
        rows = 256
        for r0 in range(0, x_ref.shape[0], rows):
            x = x_ref[r0:r0 + rows, :]
            ms = jnp.mean(x * x, axis=-1, keepdims=True)
            xn_ref[r0:r0 + rows, :] = (x * lax.rsqrt(ms + NORM_EPS) * g_ref[...]).astype(bf16)

    o_ref[...] = lax.dot_general(xn_ref[...], w_ref[...], (((1,), (1,)), ((), ())), preferred_element_type=f32)


def _norm_mm(x, g, wt, l):
    m, d = x.shape
    n = wt.shape[1]
    tm = min(m, 1024)
    tn = 768 if n % 768 == 0 else 512
    xmode = dict(pipeline_mode=pl.Buffered(1))
    return pl.pallas_call(
        _norm_mm_kernel,
        grid=(m // tm, n // tn),
        in_specs=[pl.BlockSpec((tm, d), lambda i, j: (i, 0), **xmode),
                  pl.BlockSpec((1, d), lambda i, j: (0, 0)),
                  pl.BlockSpec((None, tn, d), lambda i, j: (l, j, 0))],
        out_specs=pl.BlockSpec((tm, tn), lambda i, j: (i, j)),
        out_shape=jax.ShapeDtypeStruct((m, n), f32),
        scratch_shapes=[pltpu.VMEM((tm, d), bf16)],
        compiler_params=_params("parallel", "arbitrary"),
        name="norm_mm",
    )(x, g, wt)


def _proj_kernel(xn_ref, w_ref, o_ref):
    o_ref[...] = lax.dot_general(xn_ref[...], w_ref[...], (((1,), (1,)), ((), ())), preferred_element_type=f32)


def _proj(xn, wt, l):
    m, d = xn.shape
    n = wt.shape[1]
    tm = min(m, 1024)
    tn = 768 if n % 768 == 0 else 512
    return pl.pallas_call(
        _proj_kernel,
        grid=(m // tm, n // tn),
        in_specs=[pl.BlockSpec((tm, d), lambda i, j: (i, 0)),
                  pl.BlockSpec((None, tn, d), lambda i, j: (l, j, 0))],
        out_specs=pl.BlockSpec((tm, tn), lambda i, j: (i, j)),
        out_shape=jax.ShapeDtypeStruct((m, n), f32),
        compiler_params=_params("parallel", "arbitrary"),
        name="proj",
    )(xn, wt)


def _outproj_kernel(a1, a2, a3, a4, w_ref, g_ref, gn_ref, x_ref, o_ref, *xn_ref, nj, tn):
    j = pl.program_id(1)
    acc = None
    k0 = 0
    for a in (a1, a2, a3, a4):
        part = _mm(a[...], w_ref[k0:k0 + a.shape[1], :])
        acc = part if acc is None else acc + part
        k0 += a.shape[1]
    o_ref[:, pl.ds(pl.multiple_of(j * tn, tn), tn)] = acc

    @pl.when(j == nj - 1)
    def _():
        cols = [slice(k * tn, (k + 1) * tn) for k in range(nj)]
        ssq = None
        for sl in cols:
            p = o_ref[:, sl]
            s = jnp.sum(p * p, axis=-1, keepdims=True)
            ssq = s if ssq is None else ssq + s
        inv = lax.rsqrt(ssq * (1.0 / D_MODEL) + NORM_EPS)
        ssq = None
        for sl in cols:
            new = x_ref[:, sl] + o_ref[:, sl] * inv * g_ref[:, sl]
            o_ref[:, sl] = new
            s = jnp.sum(new * new, axis=-1, keepdims=True)
            ssq = s if ssq is None else ssq + s
        if xn_ref:
            inv = lax.rsqrt(ssq * (1.0 / D_MODEL) + NORM_EPS)
            for sl in cols:
                xn_ref[0][:, sl] = (o_ref[:, sl] * inv * gn_ref[:, sl]).astype(bf16)


def _outproj(parts, w, l, g, x, g_next, tn=512):
    m, d = x.shape
    nj = d // tn
    tm = 512 if parts[0].dtype == bf16 else 256
    emit = g_next is not None
    row_spec = pl.BlockSpec((tm, d), lambda i, j: (i, 0))
    in_specs = [pl.BlockSpec((tm, p.shape[1]), lambda i, j: (i, 0)) for p in parts]
    in_specs += [pl.BlockSpec((None, d, tn), lambda i, j: (l, 0, j))]
    in_specs += [pl.BlockSpec((1, d), lambda i, j: (0, 0)), pl.BlockSpec((1, d), lambda i, j: (0, 0)),
                 pl.BlockSpec((tm, d), lambda i, j: (i, 0), pipeline_mode=pl.Buffered(1))]
    out = pl.pallas_call(
        functools.partial(_outproj_kernel, nj=nj, tn=tn),
        grid=(m // tm, nj),
        in_specs=in_specs,
        out_specs=[row_spec, row_spec] if emit else [row_spec],
        out_shape=[jax.ShapeDtypeStruct((m, d), f32)] + ([jax.ShapeDtypeStruct((m, d), bf16)] if emit else []),
        compiler_params=_params("parallel", "arbitrary"),
        name="outproj",
    )(*parts, w, g, g_next if emit else g, x)
    return (out[0], out[1]) if emit else (out[0], None)


def _s5_kernel(u_ref, gate_ref, h0_ref, lre_ref, lim_ref, step_ref, bre_ref, bim_ref, cre_ref, cim_ref,
               d_ref, wglu_ref, bglu_ref, y_ref, hl_ref, sre, sim, car, cai, pwr, pwi, bbr, bbi, *, tt, nj, chain):
    j = pl.program_id(1)
    nrt = tt // SUBLANE

    @pl.when(j == 0)
    def _():
        lr = lre_ref[...]
        li = lim_ref[...]
        st = jnp.exp(step_ref[...])
        kidx = (lax.broadcasted_iota(jnp.int32, (SUBLANE, S5_NS), 0) + 1).astype(f32)
        mag = jnp.exp(kidx * (lr * st))
        ang = kidx * (li * st)
        p_re = mag * jnp.cos(ang)
        p_im = mag * jnp.sin(ang)
        pwr[...] = p_re
        pwi[...] = p_im
        ab_re = p_re[0:1]
        ab_im = p_im[0:1]
        den = lr * lr + li * li
        f_re = ((ab_re - 1.0) * lr + ab_im * li) / den
        f_im = (ab_im * lr - (ab_re - 1.0) * li) / den
        for blk in range(S5_BLOCKS):
            cols = slice(blk * S5_LANES, (blk + 1) * S5_LANES)
            fr = f_re[:, cols]
            fi = f_im[:, cols]
            bbr[blk] = (fr * bre_ref[blk] - fi * bim_ref[blk]).astype(bf16)
            bbi[blk] = (fr * bim_ref[blk] + fi * bre_ref[blk]).astype(bf16)

    pw_re = pwr[...]
    pw_im = pwi[...]
    u = u_ref[...]
    for blk in range(S5_BLOCKS):
        cols = slice(blk * S5_LANES, (blk + 1) * S5_LANES)
        ub = u[:, blk * LANE:(blk + 1) * LANE].astype(bf16)
        sre[:, cols] = jnp.dot(ub, bbr[blk], preferred_element_type=f32)
        sim[:, cols] = jnp.dot(ub, bbi[blk], preferred_element_type=f32)

    @pl.when(j == 0)
    def _():
        if chain:
            car[...] = jnp.broadcast_to(h0_ref[0, 0:1], (SUBLANE, S5_NS))
            cai[...] = jnp.broadcast_to(h0_ref[0, 1:2], (SUBLANE, S5_NS))
        else:
            car[...] = h0_ref[pl.ds(0, S5_NS, stride=2), :].T
            cai[...] = h0_ref[pl.ds(1, S5_NS, stride=2), :].T

    rowi = lax.broadcasted_iota(jnp.int32, (SUBLANE, S5_LANES), 0)
    for ch in range(S5_NS // S5_LANES):
        cols = slice(ch * S5_LANES, (ch + 1) * S5_LANES)
        p_re = pw_re[:, cols]
        p_im = pw_im[:, cols]
        steps = [(d, jnp.where(rowi >= d, p_re[d - 1:d], 0.0), jnp.where(rowi >= d, p_im[d - 1:d], 0.0))
                 for d in (1, 2, 4)]

        def body(r, carry, cols=cols, p_re=p_re, p_im=p_im, steps=steps):
            rows = pl.ds(pl.multiple_of(r * SUBLANE, SUBLANE), SUBLANE)
            xr = sre[rows, cols]
            xi = sim[rows, cols]
            for d, ar, ai in steps:
                sr = pltpu.roll(xr, d, 0)
                si = pltpu.roll(xi, d, 0)
                xr, xi = xr + ar * sr - ai * si, xi + ar * si + ai * sr
            if chain:
                cr, ci = carry
            else:
                seq = pl.ds(j * nrt + r, 1)
                cr = jnp.broadcast_to(car[seq, cols], (SUBLANE, S5_LANES))
                ci = jnp.broadcast_to(cai[seq, cols], (SUBLANE, S5_LANES))
            xr, xi = xr + p_re * cr - p_im * ci, xi + p_re * ci + p_im * cr
            sre[rows, cols] = xr
            sim[rows, cols] = xi
            if chain:
                return (jnp.broadcast_to(xr[SUBLANE - 1:SUBLANE], (SUBLANE, S5_LANES)),
                        jnp.broadcast_to(xi[SUBLANE - 1:SUBLANE], (SUBLANE, S5_LANES)))
            car[seq, cols] = xr[SUBLANE - 1:SUBLANE]
            cai[seq, cols] = xi[SUBLANE - 1:SUBLANE]
            return carry

        if chain:
            cr, ci = lax.fori_loop(0, nrt, body, (car[:, cols], cai[:, cols]))
            car[:, cols] = cr
            cai[:, cols] = ci
        else:
            lax.fori_loop(0, nrt, body, 0)

    if chain:
        hl_ref[0, 0:1] = car[0:1, :]
        hl_ref[0, 1:2] = cai[0:1, :]
    else:
        @pl.when(j == nj - 1)
        def _():
            hl_ref[pl.ds(0, S5_NS, stride=2), :] = car[...].T
            hl_ref[pl.ds(1, S5_NS, stride=2), :] = cai[...].T

    ys = []
    for blk in range(S5_BLOCKS):
        cols = slice(blk * S5_LANES, (blk + 1) * S5_LANES)
        ys.append(_mm(sre[:, cols], cre_ref[blk]) - _mm(sim[:, cols], cim_ref[blk]))
    y = jnp.concatenate(ys, axis=1) + d_ref[...] * u
    z = jax.nn.gelu(y)
    gl = jax.nn.sigmoid(_mm(z, wglu_ref[...]) + bglu_ref[...])
    y_ref[...] = (z * gl * jax.nn.silu(gate_ref[...])).astype(y_ref.dtype)


def _s5(h, h0, l0, p, nseq, t, chain):
    m = nseq * t
    if chain:
        tt = min(t, 512)
        nj = t // tt
        grid = (nseq, nj)
        row = lambda b, j: b * nj + j
        h_in = pl.BlockSpec((None, 1, 2, S5_NS), lambda b, j: (l0, b, 0, 0))
        h_out = pl.BlockSpec((1, 2, S5_NS), lambda b, j: (b, 0, 0))
        h_shape = jax.ShapeDtypeStruct((nseq, 2, S5_NS), f32)
        carry_rows = SUBLANE
    else:
        assert t == SUBLANE and nseq == LANE
        tt = 64
        nj = m // tt
        grid = (1, nj)
        row = lambda b, j: j
        h_in = pl.BlockSpec((None, 2 * S5_NS, nseq), lambda b, j: (l0, 0, 0))
        h_out = pl.BlockSpec((2 * S5_NS, nseq), lambda b, j: (0, 0))
        h_shape = jax.ShapeDtypeStruct((2 * S5_NS, nseq), f32)
        carry_rows = nseq
    cu, cg = COL_S5_U // S5_WIDTH, COL_S5_GATE // S5_WIDTH
    vec = lambda n: pl.BlockSpec((1, n), lambda b, j: (0, 0))
    full3 = lambda s: pl.BlockSpec(s, lambda b, j: (0, 0, 0))
    return pl.pallas_call(
        functools.partial(_s5_kernel, tt=tt, nj=nj, chain=chain),
        grid=grid,
        in_specs=[pl.BlockSpec((tt, S5_WIDTH), lambda b, j: (row(b, j), cu)),
                  pl.BlockSpec((tt, S5_WIDTH), lambda b, j: (row(b, j), cg)),
                  h_in, vec(S5_NS), vec(S5_NS), vec(S5_NS),
                  full3((S5_BLOCKS, LANE, S5_LANES)), full3((S5_BLOCKS, LANE, S5_LANES)),
                  full3((S5_BLOCKS, S5_LANES, LANE)), full3((S5_BLOCKS, S5_LANES, LANE)),
                  vec(S5_WIDTH), pl.BlockSpec((S5_WIDTH, S5_WIDTH), lambda b, j: (0, 0)), vec(S5_WIDTH)],
        out_specs=[pl.BlockSpec((tt, S5_WIDTH), lambda b, j: (row(b, j), 0)), h_out],
        out_shape=[jax.ShapeDtypeStruct((m, S5_WIDTH), _mix_dtype(t)), h_shape],
        scratch_shapes=[pltpu.VMEM((tt, S5_NS), f32), pltpu.VMEM((tt, S5_NS), f32),
                        pltpu.VMEM((carry_rows, S5_NS), f32), pltpu.VMEM((carry_rows, S5_NS), f32),
                        pltpu.VMEM((SUBLANE, S5_NS), f32), pltpu.VMEM((SUBLANE, S5_NS), f32),
                        pltpu.VMEM((S5_BLOCKS, LANE, S5_LANES), bf16), pltpu.VMEM((S5_BLOCKS, LANE, S5_LANES), bf16)],
        compiler_params=_params("parallel" if chain else "arbitrary", "arbitrary"),
        name="s5_chain" if chain else "s5_tiles",
    )(h, h, h0, p["lam_re"], p["lam_im"], p["log_step"], p["b_re"], p["b_im"], p["c_re"], p["c_im"],
      p["d"], p["w_glu"], p["b_glu"])


def _gla_kernel(q_ref, k_ref, v_ref, gate_ref, glr_ref, wg_ref, bg_ref, ng_ref, s0_ref, acc_ref, y_ref, s_ref,
                st_ref, b_ref, *, tt, c, nj, chain):
    del acc_ref
    j = pl.program_id(1)

    if chain:
        @pl.when(j == 0)
        def _():
            for hd in range(GLA_HEADS):
                st_ref[hd] = s0_ref[0, hd].T

    la = jax.nn.log_sigmoid(_mm(glr_ref[...], wg_ref[...]) + bg_ref[...]) * (1.0 / GLA_GATE_TAU)
    b_ref[...] = _mm_mask_l(jnp.where(_chunk_tril(tt, c), 1.0, 0.0), la)

    rowi = lax.broadcasted_iota(jnp.int32, (c, GLA_DK), 0)
    lanei = lax.broadcasted_iota(jnp.int32, (c, GLA_DK), 1)
    ones = jnp.ones((GLA_DK, GLA_DK), bf16)
    vzero = jnp.zeros((GLA_DK - c, GLA_DV), f32)
    scale = GLA_DK ** -0.5
    heads = range(GLA_HEADS)
    kl = [slice(hd * GLA_DK, (hd + 1) * GLA_DK) for hd in heads]
    vl = [slice(hd * GLA_DV, (hd + 1) * GLA_DV) for hd in heads]

    def chunk(ci, carry):
        rows = pl.ds(pl.multiple_of(ci * c, c), c)
        qh = [q_ref[rows, kl[hd]] * scale for hd in heads]
        kh = [k_ref[rows, kl[hd]] for hd in heads]
        vh = [v_ref[rows, vl[hd]] for hd in heads]
        bh = [b_ref[rows, kl[hd]] for hd in heads]
        prod = [jnp.concatenate(
            [qh[hd] * kh[hd][jj:jj + 1] * jnp.exp(jnp.where(rowi >= jj, bh[hd] - bh[hd][jj:jj + 1], -jnp.inf))
             for jj in range(c)], axis=0) for hd in heads]
        sums = [_mm(prod[hd], ones) for hd in heads]
        att = []
        for hd in heads:
            a = jnp.zeros((c, GLA_DK), f32)
            for jj in range(c):
                a = a + jnp.where(lanei == jj, sums[hd][jj * c:(jj + 1) * c], 0.0)
            att.append(a)
        s_t = [st_ref[hd] if chain else s0_ref[ci, hd].T for hd in heads]
        o_in = [_mm(att[hd], jnp.concatenate([vh[hd], vzero], axis=0)) for hd in heads]
        o_st = [_mm_nt(qh[hd] * jnp.exp(bh[hd]), s_t[hd]) for hd in heads]
        b_last = [bh[hd][c - 1:c] for hd in heads]
        upd = [_mm_tn(vh[hd], kh[hd] * jnp.exp(b_last[hd] - bh[hd])) for hd in heads]
        for hd in heads:
            s_new = s_t[hd] * jnp.exp(b_last[hd]) + upd[hd]
            if chain:
                st_ref[hd] = s_new
            else:
                s_ref[ci, hd] = s_new.T
            o = o_in[hd] + o_st[hd]
            on = o * lax.rsqrt(jnp.mean(o * o, axis=-1, keepdims=True) + NORM_EPS) * ng_ref[:, vl[hd]]
            y_ref[rows, vl[hd]] = (on * jax.nn.silu(gate_ref[rows, vl[hd]])).astype(y_ref.dtype)
        return carry

    lax.fori_loop(0, tt // c, chunk, 0, unroll=math.gcd(tt // c, 8))

    if chain:
        @pl.when(j == nj - 1)
        def _():
            for hd in range(GLA_HEADS):
                s_ref[0, hd] = st_ref[hd].T


def _stacked(acc, shape):
    if acc is None:
        return jnp.zeros((SUBLANE, LANE), f32), None
    assert acc.shape == shape
    return acc, 1


def _gla(h, s0, l0, acc, l, depth, p, nseq, t, chain):
    m = nseq * t
    acc_shape = (depth, nseq, GLA_HEADS, GLA_DK, GLA_DV)
    acc, alias_to = _stacked(acc, acc_shape)
    if chain:
        c = 16
        tt = min(t, 128)
        nj = t // tt
        grid = (nseq, nj)
        row = lambda b, j: b * nj + j
        st_spec = lambda li: pl.BlockSpec((None, 1, GLA_HEADS, GLA_DK, GLA_DV), lambda b, j: (li, b, 0, 0, 0))
    else:
        c = t
        tt = 8 * c
        nj = m // tt
        grid = (1, nj)
        row = lambda b, j: j
        st_spec = lambda li: pl.BlockSpec((None, tt // c, GLA_HEADS, GLA_DK, GLA_DV), lambda b, j: (li, j, 0, 0, 0))
    vec = lambda n: pl.BlockSpec((1, n), lambda b, j: (0, 0))
    return pl.pallas_call(
        functools.partial(_gla_kernel, tt=tt, c=c, nj=nj, chain=chain),
        grid=grid,
        in_specs=[pl.BlockSpec((tt, GLA_QK), lambda b, j: (row(b, j), COL_G_Q // GLA_QK)),
                  pl.BlockSpec((tt, GLA_QK), lambda b, j: (row(b, j), COL_G_K // GLA_QK)),
                  pl.BlockSpec((tt, GLA_WIDTH), lambda b, j: (row(b, j), COL_G_V // GLA_WIDTH)),
                  pl.BlockSpec((tt, GLA_WIDTH), lambda b, j: (row(b, j), COL_G_GATE // GLA_WIDTH)),
                  pl.BlockSpec((tt, LANE), lambda b, j: (row(b, j), COL_G_LR // LANE)),
                  pl.BlockSpec((LANE, GLA_QK), lambda b, j: (0, 0)), vec(GLA_QK), vec(GLA_WIDTH), st_spec(l0),
                  pl.BlockSpec(memory_space=pl.ANY)],
        out_specs=[pl.BlockSpec((tt, GLA_WIDTH), lambda b, j: (row(b, j), 0)), st_spec(l)],
        out_shape=[jax.ShapeDtypeStruct((m, GLA_WIDTH), _mix_dtype(t)), jax.ShapeDtypeStruct(acc_shape, f32)],
        scratch_shapes=[pltpu.VMEM((GLA_HEADS, GLA_DV, GLA_DK), f32), pltpu.VMEM((tt, GLA_QK), f32)],
        input_output_aliases={} if alias_to is None else {9: alias_to},
        compiler_params=_params("parallel" if chain else "arbitrary", "arbitrary"),
        name="gla_chain" if chain else "gla_tiles",
    )(h, h, h, h, h, p["w_gate"], p["b_gate"], p["norm_g"], s0, acc)


def _seg_sum(x):
    n = x.shape[0]
    w = 2 * LANE
    lane = lax.broadcasted_iota(jnp.int32, (w, w), 1)
    sub = lax.broadcasted_iota(jnp.int32, (w, w), 0)
    ones_bd = jnp.where((lane // RWKV_HEAD) == (sub // RWKV_HEAD), 1.0, 0.0)
    stacked = jnp.concatenate([x[:, i * w:(i + 1) * w] for i in range(RWKV_WIDTH // w)], axis=0)
    s = _mm(stacked, ones_bd)
    return jnp.concatenate([s[i * n:(i + 1) * n] for i in range(RWKV_WIDTH // w)], axis=1)


def _rwkv_kernel(r_ref, k_ref, v_ref, gate_ref, wa_ref, pr_ref, pk_ref, pv_ref, pwa_ref,
                 mur_ref, muk_ref, muv_ref, muwa_ref, w0_ref, w2_ref, a0_ref, a2_ref, kk_ref, ka_ref, rk_ref,
                 lnw_ref, lnb_ref, s0_ref, acc_ref, y_ref, s_ref, lr_ref, lk_ref, lv_ref, lwa_ref,
                 st, c_r, c_k, c_v, c_wa, x_kkg, x_rg, x_bi, x_ki, x_bend, x_kend, x_gam, x_v, x_bon,
                 x_tk, x_rq, x_uv, x_yv, x_y, *, tt, c, nj, chain):
    del acc_ref
    j = pl.program_id(1)
    hd = RWKV_HEAD

    if chain:
        @pl.when(j == 0)
        def _():
            st[...] = s0_ref[0]
            c_r[...] = jnp.broadcast_to(pr_ref[0], c_r.shape)
            c_k[...] = jnp.broadcast_to(pk_ref[0], c_k.shape)
            c_v[...] = jnp.broadcast_to(pv_ref[0], c_v.shape)
            c_wa[...] = jnp.broadcast_to(pwa_ref[0], c_wa.shape)
    else:
        @pl.when(j == 0)
        def _():
            st[...] = jnp.zeros(st.shape, f32)

    def shift(x_ref, c_ref, p_ref, mu_ref, last_ref):
        x = x_ref[...]
        rowid = lax.broadcasted_iota(jnp.int32, x.shape, 0)
        rolled = pltpu.roll(x, 1, 0)
        if chain:
            prev = jnp.where(rowid == 0, c_ref[0:1, :], rolled)
            c_ref[...] = jnp.broadcast_to(x[tt - 1:tt, :], c_ref.shape)
            last_ref[0] = x[tt - 1:tt, :]
        else:
            prev = jnp.where(rowid % c == 0, p_ref[...], rolled)
            for i in range(tt // c):
                last_ref[i:i + 1, :] = x[(i + 1) * c - 1:(i + 1) * c, :]
        return x + (prev - x) * mu_ref[...]

    r = shift(r_ref, c_r, pr_ref, mur_ref, lr_ref)
    k = shift(k_ref, c_k, pk_ref, muk_ref, lk_ref)
    v = shift(v_ref, c_v, pv_ref, muv_ref, lv_ref)
    wa = shift(wa_ref, c_wa, pwa_ref, muwa_ref, lwa_ref)

    w_log = -jax.nn.softplus(-(w0_ref[...] + _mm(jnp.tanh(wa), w2_ref[...]))) - 0.5
    lw = -jnp.exp(w_log)
    a = jax.nn.sigmoid(a0_ref[...] + _mm(wa, a2_ref[...]))

    lane = lax.broadcasted_iota(jnp.int32, (LANE, LANE), 1)
    sub = lax.broadcasted_iota(jnp.int32, (LANE, LANE), 0)
    bd = (lane // RWKV_HEAD) == (sub // RWKV_HEAD)

    kk = k * kk_ref[...]
    kk = kk / jnp.maximum(jnp.sqrt(_seg_sum(kk * kk)), 1e-12)
    k2 = k * (1.0 + (a - 1.0) * ka_ref[...])
    bv = kk * a
    ri = lax.broadcasted_iota(jnp.int32, (tt, tt), 0)
    qi = lax.broadcasted_iota(jnp.int32, (tt, tt), 1)
    same = (ri // c) == (qi // c)
    strict = same & (qi < ri)
    incl = same & (qi <= ri)
    eye = jnp.where(qi == ri, 1.0, 0.0).astype(f32)
    cl = _mm_mask_l(jnp.where(incl, 1.0, 0.0), lw)
    if c == tt:
        cll = jnp.broadcast_to(cl[tt - 1:tt, :], cl.shape)
    else:
        cll = _mm_mask_l(jnp.where(same, 1.0, 0.0), lw)
    ginv = jnp.exp(-cl)
    gend = jnp.exp(cll - cl)
    x_kkg[...] = kk * jnp.exp(cl - lw)
    x_rg[...] = r * jnp.exp(cl)
    x_bi[...] = bv * ginv
    x_ki[...] = k2 * ginv
    x_bend[...] = bv * gend
    x_kend[...] = k2 * gend
    x_gam[...] = jnp.exp(cll)
    x_v[...] = v
    x_bon[...] = _seg_sum(r * k2 * rk_ref[...]) * v

    head_a = lax.broadcasted_iota(jnp.int32, (tt, LANE), 1) < RWKV_HEAD

    zero_tt = jnp.zeros((tt, tt), f32)

    def tri_inv(ams):
        def bdiag(x):
            return jnp.concatenate([jnp.concatenate([x[:, 0:tt], zero_tt], axis=1),
                                    jnp.concatenate([zero_tt, x[:, tt:2 * tt]], axis=1)], axis=0)

        if tt % LANE:
            ns = [-am for am in ams]
            ts = [eye + n for n in ns]
            if c > 2:
                ns = [_mm(n, n) for n in ns]
                width = 4
                while width < c:
                    both = [_mm(jnp.concatenate([t, n], axis=0), n) for t, n in zip(ts, ns)]
                    ts = [t + b[0:tt] for t, b in zip(ts, both)]
                    ns = [b[tt:2 * tt] for b in both]
                    width *= 2
                ts = [t + _mm(t, n) for t, n in zip(ts, ns)]
            return [(ts[i], ts[i + 1]) for i in range(0, len(ts), 2)]
        ns = [jnp.concatenate([-ams[i], -ams[i + 1]], axis=1) for i in range(0, len(ams), 2)]
        eye2 = jnp.concatenate([eye, eye], axis=1)
        ts = [eye2 + n for n in ns]
        if c > 2:
            ns = [_mm(n, bdiag(n)) for n in ns]
            width = 4
            while width < c:
                both = [_mm(jnp.concatenate([t, n], axis=0), bdiag(n)) for t, n in zip(ts, ns)]
                ts = [t + b[0:tt] for t, b in zip(ts, both)]
                ns = [b[tt:2 * tt] for b in both]
                width *= 2
            last = [_mm(t, bdiag(n)) for t, n in zip(ts, ns)]
            ts = [t + m for t, m in zip(ts, last)]
        return ts

    wide = tt % LANE == 0

    def join(xa, xb):
        return jnp.concatenate([xa, xb], axis=1) if wide else (xa, xb)

    def pair_mm(lhs, rhs):
        m = (lax.broadcasted_iota(jnp.int32, (1, rhs.shape[1]), 1) % LANE) < RWKV_HEAD
        if wide:
            return _mm(lhs, jnp.concatenate([jnp.where(m, rhs, 0.0), jnp.where(m, 0.0, rhs)], axis=0))
        return jnp.where(m, _mm(lhs[0], rhs), _mm(lhs[1], rhs))

    def lanes(p):
        return slice(p * LANE, (p + 1) * LANE)

    for g0 in range(0, RWKV_PAIRS, RWKV_GROUP):
        grp = list(range(g0, g0 + RWKV_GROUP))
        kkg = [x_kkg[:, lanes(p)] for p in grp]
        rg = [x_rg[:, lanes(p)] for p in grp]
        vp = [x_v[:, lanes(p)] for p in grp]
        lhs = [jnp.concatenate([jnp.where(head_a, a, 0.0), jnp.where(head_a, 0.0, a),
                                jnp.where(head_a, b, 0.0), jnp.where(head_a, 0.0, b)], axis=0)
               for a, b in zip(kkg, rg)]
        if tt % LANE == 0:
            pbk = [_mm_nt(x, jnp.concatenate([x_bi[:, lanes(p)], x_ki[:, lanes(p)]], axis=0))
                   for x, p in zip(lhs, grp)]
            pb = [x[:, 0:tt] for x in pbk]
            pk = [x[:, tt:2 * tt] for x in pbk]
        else:
            pb = [_mm_nt(x, x_bi[:, lanes(p)]) for x, p in zip(lhs, grp)]
            pk = [_mm_nt(x, x_ki[:, lanes(p)]) for x, p in zip(lhs, grp)]
        def blk(x, hh, mask):
            return jnp.where(mask, x[hh * tt:(hh + 1) * tt], 0.0)

        tinv = tri_inv([blk(x, hh, strict) for x in pb for hh in (0, 1)])
        ar = [join(blk(x, 2, incl), blk(x, 3, incl)) for x in pb]
        bmr = [join(jnp.concatenate([blk(x, 0, strict), blk(x, 2, incl)], axis=0),
                    jnp.concatenate([blk(x, 1, strict), blk(x, 3, incl)], axis=0)) for x in pk]
        mv = [pair_mm(x, vp[i]) for i, x in enumerate(bmr)]
        mt = [pair_mm(x, jnp.concatenate([mv[i][0:tt], kkg[i]], axis=1)) for i, x in enumerate(tinv)]
        uv = [-x[:, 0:LANE] for x in mt]
        tk = [x[:, LANE:2 * LANE] for x in mt]
        mr = [pair_mm(x, jnp.concatenate([uv[i], tk[i]], axis=1)) for i, x in enumerate(ar)]
        for i, p in enumerate(grp):
            x_uv[:, lanes(p)] = uv[i]
            x_tk[:, lanes(p)] = tk[i]
            x_yv[:, lanes(p)] = mr[i][:, 0:LANE] + mv[i][tt:2 * tt]
            x_rq[:, lanes(p)] = rg[i] - mr[i][:, LANE:2 * LANE]

    def chunk(ci, carry):
        row0 = pl.multiple_of(ci * c, c)
        rows = pl.ds(row0, c)
        pairs = range(RWKV_PAIRS)
        if not chain:
            for p in pairs:
                st[p, 0:hd, 0:hd] = s0_ref[ci, 2 * p]
                st[p, hd:2 * hd, hd:2 * hd] = s0_ref[ci, 2 * p + 1]
        s = [st[p] for p in pairs]
        zy = [_mm_nt(jnp.concatenate([x_tk[rows, lanes(p)], x_rq[rows, lanes(p)]], axis=0), s[p]) for p in pairs]
        u = [x_uv[rows, lanes(p)] - zy[p][0:c] for p in pairs]
        for p in pairs:
            x_y[rows, lanes(p)] = zy[p][c:2 * c] + x_yv[rows, lanes(p)]
        upd = [_mm_tn(jnp.concatenate([u[p], x_v[rows, lanes(p)]], axis=0),
                      jnp.concatenate([x_bend[rows, lanes(p)], x_kend[rows, lanes(p)]], axis=0)) for p in pairs]
        for p in pairs:
            s_new = s[p] * x_gam[pl.ds(row0, 1), lanes(p)] + jnp.where(bd, upd[p], 0.0)
            if chain:
                st[p] = s_new
            else:
                s_ref[ci, 2 * p] = s_new[0:hd, 0:hd]
                s_ref[ci, 2 * p + 1] = s_new[hd:2 * hd, hd:2 * hd]
        return carry

    lax.fori_loop(0, tt // c, chunk, 0, unroll=1 if chain else 2)

    y = x_y[...]
    mean = _seg_sum(y) * (1.0 / RWKV_HEAD)
    yc = y - mean
    var = _seg_sum(yc * yc) * (1.0 / RWKV_HEAD)
    yn = yc * lax.rsqrt(var + RWKV_GN_EPS) * lnw_ref[...] + lnb_ref[...] + x_bon[...]
    y_ref[...] = (yn * jax.nn.silu(gate_ref[...])).astype(y_ref.dtype)

    if chain:
        @pl.when(j == nj - 1)
        def _():
            s_ref[0] = st[...]


def _rwkv(h, prev, s0, l0, acc, l, depth, p, nseq, t, chain):
    m = nseq * t
    w = RWKV_WIDTH
    acc_shape = (depth, nseq) + ((RWKV_PAIRS, LANE, LANE) if chain else (RWKV_HEADS, RWKV_HEAD, RWKV_HEAD))
    acc, alias_to = _stacked(acc, acc_shape)
    if chain:
        tt = min(t, 128)
        c = tt
        nj = t // tt
        grid = (nseq, nj)
        row = lambda b, j: b * nj + j
        prev = tuple(x[:, None, :] for x in prev)
        prow = lambda n: pl.BlockSpec((1, 1, n), lambda b, j: (b, 0, 0))
        last_spec = prow
        last_shape = lambda n: jax.ShapeDtypeStruct((nseq, 1, n), f32)
        st_spec = lambda li: pl.BlockSpec((None, 1, RWKV_PAIRS, LANE, LANE), lambda b, j: (li, b, 0, 0, 0))
    else:
        c = t
        tt = 64
        nj = m // tt
        grid = (1, nj)
        row = lambda b, j: j
        prev = tuple(jnp.repeat(x, t, axis=0) for x in prev)
        prow = lambda n: pl.BlockSpec((tt, n), lambda b, j: (j, 0))
        last_spec = lambda n: pl.BlockSpec((tt // c, n), lambda b, j: (j, 0))
        last_shape = lambda n: jax.ShapeDtypeStruct((nseq, n), f32)
        st_spec = lambda li: pl.BlockSpec((None, tt // c, RWKV_HEADS, RWKV_HEAD, RWKV_HEAD),
                                          lambda b, j: (li, j, 0, 0, 0))
    vec = lambda n: pl.BlockSpec((1, n), lambda b, j: (0, 0))
    tok = lambda col: pl.BlockSpec((tt, w), lambda b, j: (row(b, j), col // w))
    big = lambda: pltpu.VMEM((tt, w), f32)
    y, acc, *last = pl.pallas_call(
        functools.partial(_rwkv_kernel, tt=tt, c=c, nj=nj, chain=chain),
        grid=grid,
        in_specs=[tok(COL_RW_R), tok(COL_RW_K), tok(COL_RW_V), tok(COL_RW_GATE),
                  pl.BlockSpec((tt, LANE), lambda b, j: (row(b, j), COL_RW_WA // LANE)),
                  prow(w), prow(w), prow(w), prow(LANE),
                  vec(w), vec(w), vec(w), vec(LANE),
                  vec(w), pl.BlockSpec((LANE, w), lambda b, j: (0, 0)), vec(w),
                  pl.BlockSpec((LANE, w), lambda b, j: (0, 0)), vec(w), vec(w), vec(w), vec(w), vec(w), st_spec(l0),
                  pl.BlockSpec(memory_space=pl.ANY)],
        out_specs=[pl.BlockSpec((tt, w), lambda b, j: (row(b, j), 0)), st_spec(l),
                   last_spec(w), last_spec(w), last_spec(w), last_spec(LANE)],
        out_shape=[jax.ShapeDtypeStruct((m, w), _mix_dtype(t)), jax.ShapeDtypeStruct(acc_shape, f32),
                   last_shape(w), last_shape(w), last_shape(w), last_shape(LANE)],
        input_output_aliases={} if alias_to is None else {23: alias_to},
        scratch_shapes=[pltpu.VMEM((RWKV_PAIRS, LANE, LANE), f32),
                        pltpu.VMEM((SUBLANE, w), f32), pltpu.VMEM((SUBLANE, w), f32),
                        pltpu.VMEM((SUBLANE, w), f32), pltpu.VMEM((SUBLANE, LANE), f32)] + [big() for _ in range(14)],
        compiler_params=_params("parallel" if chain else "arbitrary", "arbitrary"),
        name="rwkv_chain" if chain else "rwkv_tiles",
    )(h, h, h, h, h, *prev, p["mu_r"], p["mu_k"], p["mu_v"], p["mu_wa"], p["w0"], p["w2"], p["a0"], p["a2"],
      p["k_k"], p["k_a"], p["r_k"], p["ln_w"], p["ln_b"], s0, acc)
    shift = jnp.concatenate([x.reshape(nseq, -1) for x in last], axis=1)
    return y, acc, shift


def _xattn_kernel(q_ref, k_ref, v_ref, o_ref, *, nb, tq):
    scale = XA_HEAD_DIM ** -0.5
    probs = [(i, hd) for i in range(nb) for hd in range(XA_HEADS)]
    ln = [slice(hd * XA_HEAD_DIM, (hd + 1) * XA_HEAD_DIM) for hd in range(XA_HEADS)]
    qrows = [slice(i * tq, (i + 1) * tq) for i in range(nb)]
    mrows = [pl.ds(hd, N_MEM, stride=XA_HEADS) for hd in range(XA_HEADS)]
    s = [_mm_nt(q_ref[qrows[i], ln[hd]] * scale, k_ref[i, mrows[hd], :]) for i, hd in probs]
    s = [x - jnp.max(x, axis=-1, keepdims=True) for x in s]
    e = [jnp.exp(x) for x in s]
    pr = [x / jnp.sum(x, axis=-1, keepdims=True) for x in e]
    o = [_mm(pr[n], v_ref[i, mrows[hd], :]) for n, (i, hd) in enumerate(probs)]
    for n, (i, hd) in enumerate(probs):
        o_ref[qrows[i], ln[hd]] = o[n].astype(o_ref.dtype)


def _xattn(h, mem_k, mem_v, l0, nseq, t):
    m = nseq * t
    tq = min(t, 512)
    nj = t // tq
    nb = 8 if (nj == 1 and nseq % 8 == 0) else 1
    mem_spec = pl.BlockSpec((None, nb, N_MEM * XA_HEADS, XA_HEAD_DIM), lambda b, j: (l0, b, 0, 0))
    return pl.pallas_call(
        functools.partial(_xattn_kernel, nb=nb, tq=tq),
        grid=(nseq // nb, nj),
        in_specs=[pl.BlockSpec((nb * tq, XA_WIDTH), lambda b, j: (b * nj + j, COL_XA_Q // XA_WIDTH)),
                  mem_spec, mem_spec],
        out_specs=pl.BlockSpec((nb * tq, XA_WIDTH), lambda b, j: (b * nj + j, 0)),
        out_shape=jax.ShapeDtypeStruct((m, XA_WIDTH), _mix_dtype(t)),
        compiler_params=_params("parallel", "arbitrary"),
        name="xattn",
    )(h, mem_k, mem_v)


_RW0 = 5136
W_IN_SEGMENTS = ((_RW0, _RW0 + 1536), (_RW0 + 1536, _RW0 + 3072), (_RW0 + 3072, _RW0 + 4608), (9872, 11408),
                 (0, 1024), (1024, 2048), (3072, 4096), (4096, 5120), (2048, 2560), (2560, 3072), (11408, 11920),
                 (_RW0 + 4608, _RW0 + 4736), (5120, 5136))


def _reorder_cast_kernel(w_ref, o_ref):
    dst = 0
    for a, b in W_IN_SEGMENTS:
        o_ref[dst:dst + (b - a), :] = w_ref[a:b, :].astype(bf16)
        dst += b - a
    o_ref[dst:, :] = jnp.zeros((H_COLS - dst, o_ref.shape[1]), bf16)


def _prep_w_in(w_in, tk=256):
    depth, d, n = w_in.shape
    return pl.pallas_call(
        _reorder_cast_kernel,
        grid=(depth, d // tk),
        in_specs=[pl.BlockSpec((None, n, tk), lambda l, i: (l, 0, i))],
        out_specs=pl.BlockSpec((None, H_COLS, tk), lambda l, i: (l, 0, i)),
        out_shape=jax.ShapeDtypeStruct((depth, H_COLS, d), bf16),
        compiler_params=_params("parallel", "parallel"),
        name="w_in_layout",
    )(jnp.swapaxes(w_in, 1, 2))


def _block_diag_s5(b, c):
    gl = S5_GROUPS // S5_BLOCKS
    eye = jnp.eye(gl, dtype=b.dtype)
    bb = b.reshape(S5_BLOCKS, gl, S5_STATE, S5_GROUP)
    bbd = jnp.einsum("kgpn,gh->kgnhp", bb, eye).reshape(S5_BLOCKS, gl * S5_GROUP, gl * S5_STATE)
    cc = c.reshape(S5_BLOCKS, gl, S5_GROUP, S5_STATE)
    cbd = jnp.einsum("kgnp,gh->kgphn", cc, eye).reshape(S5_BLOCKS, gl * S5_STATE, gl * S5_GROUP)
    return bbd, cbd


def _layer_params(l, a):
    row = lambda x: x.reshape(1, -1)
    bre, cre = _block_diag_s5(a["s5_b_re"][l], a["s5_c_re"][l])
    bim, cim = _block_diag_s5(a["s5_b_im"][l], a["s5_c_im"][l])
    s5 = dict(lam_re=row(a["s5_lam_re"][l]), lam_im=row(a["s5_lam_im"][l]),
              log_step=row(jnp.broadcast_to(a["s5_log_step"][l][:, None], (S5_GROUPS, S5_STATE))),
              b_re=bre, b_im=bim, c_re=cre.astype(bf16), c_im=cim.astype(bf16),
              d=row(a["s5_d"][l]), w_glu=a["s5_w_glu"][l].astype(bf16), b_glu=row(a["s5_b_glu"][l]))
    wg = jnp.zeros((LANE, GLA_QK), f32).at[:GLA_GATE_RANK].set(a["gla_w_gate"][l])
    gla = dict(w_gate=wg, b_gate=row(a["gla_b_gate"][l]), norm_g=row(a["gla_norm_g"][l]))
    mu = a["rw_mu"][l]
    w = RWKV_WIDTH
    z64 = jnp.zeros((64, w), f32)
    rw = dict(mu_r=row(mu[0:w]), mu_k=row(mu[w:2 * w]), mu_v=row(mu[2 * w:3 * w]), mu_wa=row(mu[3 * w:]),
              w0=row(a["rw_w0"][l]), w2=jnp.concatenate([a["rw_w2"][l], z64], 0).astype(bf16),
              a0=row(a["rw_a0"][l]), a2=jnp.concatenate([z64, a["rw_a2"][l]], 0).astype(bf16),
              k_k=row(a["rw_k_k"][l]), k_a=row(a["rw_k_a"][l]), r_k=row(a["rw_r_k"][l]),
              ln_w=row(a["rw_ln_w"][l]), ln_b=row(a["rw_ln_b"][l]))
    return s5, gla, rw


def _unpack_rwkv_state(s):
    b = s.shape[0]
    s = s.reshape(b, RWKV_PAIRS, 2, RWKV_HEAD, 2, RWKV_HEAD)
    return jnp.stack([s[:, :, 0, :, 0, :], s[:, :, 1, :, 1, :]], axis=2).reshape(b, RWKV_HEADS, RWKV_HEAD, RWKV_HEAD)


def _split_shift(x):
    w = RWKV_WIDTH
    return tuple(x[:, a:b] for a, b in ((0, w), (w, 2 * w), (2 * w, 3 * w), (3 * w, 3 * w + LANE)))


def _mixer_layer(x, xn, nseq, t, mem_k, mem_v, s_s5, s_gla, s_rw, l0, acc_gla, acc_rw, l, depth, prev, chain, lp,
                 w_in_t, w_out_b, g_pre, g_post, g_next):
    s5p, glap, rwp = lp
    h = _norm_mm(x, g_pre, w_in_t, l) if xn is None else _proj(xn, w_in_t, l)
    y_s5, s5_state = _s5(h, s_s5, l0, s5p, nseq, t, chain)
    y_gla, s_gla_new = _gla(h, s_gla, l0, acc_gla, l, depth, glap, nseq, t, chain)
    y_rw, s_rw_new, shift = _rwkv(h, prev, s_rw, l0, acc_rw, l, depth, rwp, nseq, t, chain)
    y_xa = _xattn(h, mem_k, mem_v, l0, nseq, t)
    x_new, xn_new = _outproj((y_s5, y_gla, y_rw, y_xa), w_out_b, l, g_post, x, g_next)
    return x_new, xn_new, s5_state, s_gla_new, s_rw_new, shift


def kernel(x_prompt, x_sample, mem_prompt, cache_mem_k, cache_mem_v, state_s5, state_gla, state_rwkv, state_rwkv_shift, g_pre, g_post, w_in, w_out, s5_lam_re, s5_lam_im, s5_log_step, s5_b_re, s5_b_im, s5_c_re, s5_c_im, s5_d, s5_w_glu, s5_b_glu, gla_w_gate, gla_b_gate, gla_norm_g, rw_mu, rw_w0, rw_w2, rw_a0, rw_a2, rw_k_k, rw_k_a, rw_r_k, rw_ln_w, rw_ln_b, mem_g, w_mk, w_mv):
    raw = dict(s5_lam_re=s5_lam_re, s5_lam_im=s5_lam_im, s5_log_step=s5_log_step, s5_b_re=s5_b_re, s5_b_im=s5_b_im,
               s5_c_re=s5_c_re, s5_c_im=s5_c_im, s5_d=s5_d, s5_w_glu=s5_w_glu, s5_b_glu=s5_b_glu,
               gla_w_gate=gla_w_gate, gla_b_gate=gla_b_gate, gla_norm_g=gla_norm_g, rw_mu=rw_mu, rw_w0=rw_w0,
               rw_w2=rw_w2, rw_a0=rw_a0, rw_a2=rw_a2, rw_k_k=rw_k_k, rw_k_a=rw_k_a, rw_r_k=rw_r_k,
               rw_ln_w=rw_ln_w, rw_ln_b=rw_ln_b)
    bp, tp, d = x_prompt.shape
    bs, ts, _ = x_sample.shape
    depth = w_in.shape[0]
    w_in_t = _prep_w_in(w_in)
    w_out_b = w_out.astype(bf16)
    w_kv_t = jnp.swapaxes(jnp.concatenate([w_mk, w_mv], axis=2), 1, 2).astype(bf16)
    mem_rows = N_MEM * XA_HEADS

    xp = x_prompt.reshape(bp * tp, d)
    xs = x_sample.reshape(bs * ts, d)
    mem2d = mem_prompt.reshape(bp * N_MEM, d)
    zp = lambda *s: jnp.zeros(s, f32)
    prev_p = _split_shift(zp(bp, RWKV_SHIFT_COLS))

    gla_p = gla_s = rw_p = rw_s = xnp = xns = None
    zero_gla = zp(1, bp, GLA_HEADS, GLA_DK, GLA_DV)
    zero_rw = zp(1, bp, RWKV_PAIRS, LANE, LANE)
    zero_s5 = zp(1, bp, 2, S5_NS)
    s5_in = state_s5.reshape(depth, bs, S5_NS * 2).transpose(0, 2, 1)
    mem_k_s = cache_mem_k.reshape(depth, bs, mem_rows, XA_HEAD_DIM)
    mem_v_s = cache_mem_v.reshape(depth, bs, mem_rows, XA_HEAD_DIM)

    outs = [[] for _ in range(6)]
    for l in range(depth):
        lp = _layer_params(l, raw)
        gp, gq = g_pre[l].reshape(1, d), g_post[l].reshape(1, d)
        gn = g_pre[l + 1].reshape(1, d) if l + 1 < depth else None
        kv = _norm_mm(mem2d, mem_g[l].reshape(1, d), w_kv_t, l)
        k_mem = kv[:, :XA_WIDTH].reshape(bp, N_MEM, XA_WIDTH)
        v_mem = kv[:, XA_WIDTH:].reshape(bp, N_MEM, XA_WIDTH)
        xp, xnp, h1, gla_p, rw_p, p1 = _mixer_layer(
            xp, xnp, bp, tp, k_mem.reshape(1, bp, mem_rows, XA_HEAD_DIM), v_mem.reshape(1, bp, mem_rows, XA_HEAD_DIM),
            zero_s5, zero_gla, zero_rw, 0, gla_p, rw_p, l, depth,
            prev_p, True, lp, w_in_t, w_out_b, gp, gq, gn)
        xs, xns, h2, gla_s, rw_s, p2 = _mixer_layer(
            xs, xns, bs, ts, mem_k_s, mem_v_s,
            s5_in, state_gla, state_rwkv, l, gla_s, rw_s, l, depth,
            _split_shift(state_rwkv_shift[l]), False, lp, w_in_t, w_out_b, gp, gq, gn)
        vals = (k_mem.reshape(bp, N_MEM, XA_HEADS, XA_HEAD_DIM), v_mem.reshape(bp, N_MEM, XA_HEADS, XA_HEAD_DIM),
                h1, h2, p1, p2)
        for o, val in zip(outs, vals):
            o.append(val)

    mk, mv, s5_p, s5_s, sh_p, sh_s = (jnp.stack(o) for o in outs)
    s5_p = s5_p.transpose(0, 1, 3, 2).reshape(depth, bp, S5_GROUPS, S5_STATE, 2)
    s5_s = s5_s.transpose(0, 2, 1).reshape(depth, bs, S5_GROUPS, S5_STATE, 2)
    rw_p_heads = _unpack_rwkv_state(rw_p.reshape(depth * bp, RWKV_PAIRS, LANE, LANE))
    rw_p_heads = rw_p_heads.reshape(depth, bp, RWKV_HEADS, RWKV_HEAD, RWKV_HEAD)
    return (xp.reshape(bp, tp, d), xs.reshape(bs, ts, d), mk, mv, s5_p, s5_s, gla_p, gla_s, rw_p_heads, rw_s,
            sh_p, sh_s)
```

```python
import functools
import math

import jax
import jax.numpy as jnp
from jax import lax
from jax.experimental import pallas as pl
from jax.experimental.pallas import tpu as pltpu

f32 = jnp.float32
bf16 = jnp.bfloat16

D_MODEL = 4096
N_MEM = 256
NORM_EPS = 1e-6

S5_WIDTH = 1024
S5_GROUP = 16
S5_GROUPS = 64
S5_STATE = 64
S5_NS = S5_GROUPS * S5_STATE
S5_BLOCKS = 8
S5_LANES = 512

GLA_WIDTH = 1024
GLA_HEADS = 4
GLA_DV = 256
GLA_DK = 128
GLA_QK = 512
GLA_GATE_RANK = 16
GLA_GATE_TAU = 16.0

RWKV_WIDTH = 1536
RWKV_HEAD = 64
RWKV_HEADS = 24
RWKV_PAIRS = 12
RWKV_GROUP = 12
RWKV_GN_EPS = RWKV_HEAD * 1e-5
RWKV_SHIFT_COLS = 3 * RWKV_WIDTH + 128

XA_HEADS = 4
XA_HEAD_DIM = 128
XA_WIDTH = 512

LANE = 128
SUBLANE = 8
VMEM_LIMIT = 56 * 1024 * 1024

H_COLS = 12288
COL_RW_R, COL_RW_K, COL_RW_V, COL_RW_GATE = 0, 1536, 3072, 4608
COL_S5_U, COL_S5_GATE, COL_G_V, COL_G_GATE = 6144, 7168, 8192, 9216
COL_G_Q, COL_G_K, COL_XA_Q = 10240, 10752, 11264
COL_RW_WA, COL_G_LR = 11776, 11904


def _mm(a, b):
    return jnp.dot(a.astype(bf16), b.astype(bf16), preferred_element_type=f32)


def _mm_nt(a, b):
    return lax.dot_general(a.astype(bf16), b.astype(bf16), (((1,), (1,)), ((), ())), preferred_element_type=f32)


def _mm_tn(a, b):
    return lax.dot_general(a.astype(bf16), b.astype(bf16), (((0,), (0,)), ((), ())), preferred_element_type=f32)


def _split2(x):
    hi = x.astype(bf16)
    return hi, (x - hi.astype(f32)).astype(bf16)


def _mm_mask_l(mask01, x):
    hi, lo = _split2(x)
    m = mask01.astype(bf16)
    return jnp.dot(m, hi, preferred_element_type=f32) + jnp.dot(m, lo, preferred_element_type=f32)


def _mix_dtype(t):
    return bf16 if t % (2 * SUBLANE) == 0 else f32


def _params(*sem):
    return pltpu.CompilerParams(dimension_semantics=sem, vmem_limit_bytes=VMEM_LIMIT)


def _chunk_tril(n, c):
    r = lax.broadcasted_iota(jnp.int32, (n, n), 0)
    q = lax.broadcasted_iota(jnp.int32, (n, n), 1)
    return ((r // c) == (q // c)) & (q <= r)


def _norm_mm_kernel(x_ref, g_ref, w_ref, o_ref, xn_ref):
    @pl.when(pl.program_id(1) == 0)
    def _():
        rows = 256
        for r0 in range(0, x_ref.shape[0], rows):
            x = x_ref[r0:r0 + rows, :]
            ms = jnp.mean(x * x, axis=-1, keepdims=True)
            xn_ref[r0:r0 + rows, :] = (x * lax.rsqrt(ms + NORM_EPS) * g_ref[...]).astype(bf16)

    o_ref[...] = lax.dot_general(xn_ref[...], w_ref[...], (((1,), (1,)), ((), ())), preferred_element_type=f32)


def _norm_mm(x, g, wt, l):
    m, d = x.shape
    n = wt.shape[1]
    tm = min(m, 1024)
    tn = 768 if n % 768 == 0 else 512
    xmode = dict(pipeline_mode=pl.Buffered(1))
    return pl.pallas_call(
        _norm_mm_kernel,
        grid=(m // tm, n // tn),
        in_specs=[pl.BlockSpec((tm, d), lambda i, j: (i, 0), **xmode),
                  pl.BlockSpec((1, d), lambda i, j: (0, 0)),
                  pl.BlockSpec((None, tn, d), lambda i, j: (l, j, 0))],
        out_specs=pl.BlockSpec((tm, tn), lambda i, j: (i, j)),
        out_shape=jax.ShapeDtypeStruct((m, n), f32),
        scratch_shapes=[pltpu.VMEM((tm, d), bf16)],
        compiler_params=_params("parallel", "arbitrary"),
        name="norm_mm",
    )(x, g, wt)


def _proj_kernel(xn_ref, w_ref, o_ref):
    o_ref[...] = lax.dot_general(xn_ref[...], w_ref[...], (((1,), (1,)), ((), ())), preferred_element_type=f32)


def _proj(xn, wt, l):
    m, d = xn.shape
    n = wt.shape[1]
    tm = min(m, 1024)
    tn = 768 if n % 768 == 0 else 512
    return pl.pallas_call(
        _proj_kernel,
        grid=(m // tm, n // tn),
        in_specs=[pl.BlockSpec((tm, d), lambda i, j: (i, 0)),
                  pl.BlockSpec((None, tn, d), lambda i, j: (l, j, 0))],
        out_specs=pl.BlockSpec((tm, tn), lambda i, j: (i, j)),
        out_shape=jax.ShapeDtypeStruct((m, n), f32),
        compiler_params=_params("parallel", "arbitrary"),
        name="proj",
    )(xn, wt)


def _outproj_kernel(a1, a2, a3, a4, w_ref, g_ref, gn_ref, x_ref, o_ref, *xn_ref, nj, tn):
    j = pl.program_id(1)
    acc = None
    k0 = 0
    for a in (a1, a2, a3, a4):
        part = _mm(a[...], w_ref[k0:k0 + a.shape[1], :])
        acc = part if acc is None else acc + part
        k0 += a.shape[1]
    o_ref[:, pl.ds(pl.multiple_of(j * tn, tn), tn)] = acc

    @pl.when(j == nj - 1)
    def _():
        cols = [slice(k * tn, (k + 1) * tn) for k in range(nj)]
        ssq = None
        for sl in cols:
            p = o_ref[:, sl]
            s = jnp.sum(p * p, axis=-1, keepdims=True)
            ssq = s if ssq is None else ssq + s
        inv = lax.rsqrt(ssq * (1.0 / D_MODEL) + NORM_EPS)
        ssq = None
        for sl in cols:
            new = x_ref[:, sl] + o_ref[:, sl] * inv * g_ref[:, sl]
            o_ref[:, sl] = new
            s = jnp.sum(new * new, axis=-1, keepdims=True)
            ssq = s if ssq is None else ssq + s
        if xn_ref:
            inv = lax.rsqrt(ssq * (1.0 / D_MODEL) + NORM_EPS)
            for sl in cols:
                xn_ref[0][:, sl] = (o_ref[:, sl] * inv * gn_ref[:, sl]).astype(bf16)


def _outproj(parts, w, l, g, x, g_next, tn=512):
    m, d = x.shape
    nj = d // tn
    tm = 512 if parts[0].dtype == bf16 else 256
    emit = g_next is not None
    row_spec = pl.BlockSpec((tm, d), lambda i, j: (i, 0))
    in_specs = [pl.BlockSpec((tm, p.shape[1]), lambda i, j: (i, 0)) for p in parts]
    in_specs += [pl.BlockSpec((None, d, tn), lambda i, j: (l, 0, j))]
    in_specs += [pl.BlockSpec((1, d), lambda i, j: (0, 0)), pl.BlockSpec((1, d), lambda i, j: (0, 0)),
                 pl.BlockSpec((tm, d), lambda i, j: (i, 0), pipeline_mode=pl.Buffered(1))]
    out = pl.pallas_call(
        functools.partial(_outproj_kernel, nj=nj, tn=tn),
        grid=(m // tm, nj),
        in_specs=in_specs,
        out_specs=[row_spec, row_spec] if emit else [row_spec],
        out_shape=[jax.ShapeDtypeStruct((m, d), f32)] + ([jax.ShapeDtypeStruct((m, d), bf16)] if emit else []),
        compiler_params=_params("parallel", "arbitrary"),
        name="outproj",
    )(*parts, w, g, g_next if emit else g, x)
    return (out[0], out[1]) if emit else (out[0], None)


def _mm4_kernel(a1, a2, a3, a4, w_ref, o_ref):
    acc = None
    k0 = 0
    for a in (a1, a2, a3, a4):
        part = _mm(a[...], w_ref[k0:k0 + a.shape[1], :])
        acc = part if acc is None else acc + part
        k0 += a.shape[1]
    o_ref[...] = acc


def _post_kernel(p_ref, x_ref, g_ref, gn_ref, o_ref, *xn_ref):
    p = p_ref[...]
    inv = lax.rsqrt(jnp.mean(p * p, axis=-1, keepdims=True) + NORM_EPS)
    new = x_ref[...] + p * inv * g_ref[...]
    o_ref[...] = new
    if xn_ref:
        inv2 = lax.rsqrt(jnp.mean(new * new, axis=-1, keepdims=True) + NORM_EPS)
        xn_ref[0][...] = (new * inv2 * gn_ref[...]).astype(bf16)


def _outproj_split(parts, w, l, g, x, g_next, tm=1024, tn=1024, tr=256):
    m, d = x.shape
    emit = g_next is not None
    proj = pl.pallas_call(
        _mm4_kernel,
        grid=(m // tm, d // tn),
        in_specs=[pl.BlockSpec((tm, p.shape[1]), lambda i, j: (i, 0)) for p in parts]
        + [pl.BlockSpec((None, d, tn), lambda i, j: (l, 0, j))],
        out_specs=pl.BlockSpec((tm, tn), lambda i, j: (i, j)),
        out_shape=jax.ShapeDtypeStruct((m, d), f32),
        compiler_params=_params("parallel", "arbitrary"),
        name="mm4",
    )(*parts, w)
    row = pl.BlockSpec((tr, d), lambda i: (i, 0))
    vec = pl.BlockSpec((1, d), lambda i: (0, 0))
    out = pl.pallas_call(
        _post_kernel,
        grid=(m // tr,),
        in_specs=[row, row, vec, vec],
        out_specs=[row, row] if emit else [row],
        out_shape=[jax.ShapeDtypeStruct((m, d), f32)] + ([jax.ShapeDtypeStruct((m, d), bf16)] if emit else []),
        compiler_params=_params("parallel"),
        name="post",
    )(proj, x, g, g_next if emit else g)
    return (out[0], out[1]) if emit else (out[0], None)


def _s5_kernel(u_ref, gate_ref, h0_ref, lre_ref, lim_ref, step_ref, bre_ref, bim_ref, cre_ref, cim_ref,
               d_ref, wglu_ref, bglu_ref, y_ref, hl_ref, sre, sim, car, cai, pwr, pwi, bbr, bbi, *, tt, nj, chain):
    j = pl.program_id(1)
    nrt = tt // SUBLANE

    @pl.when(j == 0)
    def _():
        lr = lre_ref[...]
        li = lim_ref[...]
        st = jnp.exp(step_ref[...])
        kidx = (lax.broadcasted_iota(jnp.int32, (SUBLANE, S5_NS), 0) + 1).astype(f32)
        mag = jnp.exp(kidx * (lr * st))
        ang = kidx * (li * st)
        p_re = mag * jnp.cos(ang)
        p_im = mag * jnp.sin(ang)
        pwr[...] = p_re
        pwi[...] = p_im
        ab_re = p_re[0:1]
        ab_im = p_im[0:1]
        den = lr * lr + li * li
        f_re = ((ab_re - 1.0) * lr + ab_im * li) / den
        f_im = (ab_im * lr - (ab_re - 1.0) * li) / den
        for blk in range(S5_BLOCKS):
            cols = slice(blk * S5_LANES, (blk + 1) * S5_LANES)
            fr = f_re[:, cols]
            fi = f_im[:, cols]
            bbr[blk] = (fr * bre_ref[blk] - fi * bim_ref[blk]).astype(bf16)
            bbi[blk] = (fr * bim_ref[blk] + fi * bre_ref[blk]).astype(bf16)

    pw_re = pwr[...]
    pw_im = pwi[...]
    u = u_ref[...]
    for blk in range(S5_BLOCKS):
        cols = slice(blk * S5_LANES, (blk + 1) * S5_LANES)
        ub = u[:, blk * LANE:(blk + 1) * LANE].astype(bf16)
        sre[:, cols] = jnp.dot(ub, bbr[blk], preferred_element_type=f32)
        sim[:, cols] = jnp.dot(ub, bbi[blk], preferred_element_type=f32)

    @pl.when(j == 0)
    def _():
        if chain:
            car[...] = jnp.broadcast_to(h0_ref[0, 0:1], (SUBLANE, S5_NS))
            cai[...] = jnp.broadcast_to(h0_ref[0, 1:2], (SUBLANE, S5_NS))
        else:
            car[...] = h0_ref[pl.ds(0, S5_NS, stride=2), :].T
            cai[...] = h0_ref[pl.ds(1, S5_NS, stride=2), :].T

    rowi = lax.broadcasted_iota(jnp.int32, (SUBLANE, S5_LANES), 0)
    for ch in range(S5_NS // S5_LANES):
        cols = slice(ch * S5_LANES, (ch + 1) * S5_LANES)
        p_re = pw_re[:, cols]
        p_im = pw_im[:, cols]
        steps = [(d, jnp.where(rowi >= d, p_re[d - 1:d], 0.0), jnp.where(rowi >= d, p_im[d - 1:d], 0.0))
                 for d in (1, 2, 4)]

        def body(r, carry, cols=cols, p_re=p_re, p_im=p_im, steps=steps):
            rows = pl.ds(pl.multiple_of(r * SUBLANE, SUBLANE), SUBLANE)
            xr = sre[rows, cols]
            xi = sim[rows, cols]
            for d, ar, ai in steps:
                sr = pltpu.roll(xr, d, 0)
                si = pltpu.roll(xi, d, 0)
                xr, xi = xr + ar * sr - ai * si, xi + ar * si + ai * sr
            if chain:
                cr, ci = carry
            else:
                seq = pl.ds(j * nrt + r, 1)
                cr = jnp.broadcast_to(car[seq, cols], (SUBLANE, S5_LANES))
                ci = jnp.broadcast_to(cai[seq, cols], (SUBLANE, S5_LANES))
            xr, xi = xr + p_re * cr - p_im * ci, xi + p_re * ci + p_im * cr
            sre[rows, cols] = xr
            sim[rows, cols] = xi
            if chain:
                return (jnp.broadcast_to(xr[SUBLANE - 1:SUBLANE], (SUBLANE, S5_LANES)),
                        jnp.broadcast_to(xi[SUBLANE - 1:SUBLANE], (SUBLANE, S5_LANES)))
            car[seq, cols] = xr[SUBLANE - 1:SUBLANE]
            cai[seq, cols] = xi[SUBLANE - 1:SUBLANE]
            return carry

        if chain:
            cr, ci = lax.fori_loop(0, nrt, body, (car[:, cols], cai[:, cols]))
            car[:, cols] = cr
            cai[:, cols] = ci
        else:
            lax.fori_loop(0, nrt, body, 0)

    if chain:
        hl_ref[0, 0:1] = car[0:1, :]
        hl_ref[0, 1:2] = cai[0:1, :]
    else:
        @pl.when(j == nj - 1)
        def _():
            hl_ref[pl.ds(0, S5_NS, stride=2), :] = car[...].T
            hl_ref[pl.ds(1, S5_NS, stride=2), :] = cai[...].T

    ys = []
    for blk in range(S5_BLOCKS):
        cols = slice(blk * S5_LANES, (blk + 1) * S5_LANES)
        ys.append(_mm(sre[:, cols], cre_ref[blk]) - _mm(sim[:, cols], cim_ref[blk]))
    y = jnp.concatenate(ys, axis=1) + d_ref[...] * u
    z = jax.nn.gelu(y)
    gl = jax.nn.sigmoid(_mm(z, wglu_ref[...]) + bglu_ref[...])
    y_ref[...] = (z * gl * jax.nn.silu(gate_ref[...])).astype(y_ref.dtype)


def _s5(h, h0, l0, p, nseq, t, chain):
    m = nseq * t
    if chain:
        tt = min(t, 512)
        nj = t // tt
        grid = (nseq, nj)
        row = lambda b, j: b * nj + j
        h_in = pl.BlockSpec((None, 1, 2, S5_NS), lambda b, j: (l0, b, 0, 0))
        h_out = pl.BlockSpec((1, 2, S5_NS), lambda b, j: (b, 0, 0))
        h_shape = jax.ShapeDtypeStruct((nseq, 2, S5_NS), f32)
        carry_rows = SUBLANE
    else:
        assert t == SUBLANE and nseq == LANE
        tt = 64
        nj = m // tt
        grid = (1, nj)
        row = lambda b, j: j
        h_in = pl.BlockSpec((None, 2 * S5_NS, nseq), lambda b, j: (l0, 0, 0))
        h_out = pl.BlockSpec((2 * S5_NS, nseq), lambda b, j: (0, 0))
        h_shape = jax.ShapeDtypeStruct((2 * S5_NS, nseq), f32)
        carry_rows = nseq
    cu, cg = COL_S5_U // S5_WIDTH, COL_S5_GATE // S5_WIDTH
    vec = lambda n: pl.BlockSpec((1, n), lambda b, j: (0, 0))
    full3 = lambda s: pl.BlockSpec(s, lambda b, j: (0, 0, 0))
    return pl.pallas_call(
        functools.partial(_s5_kernel, tt=tt, nj=nj, chain=chain),
        grid=grid,
        in_specs=[pl.BlockSpec((tt, S5_WIDTH), lambda b, j: (row(b, j), cu)),
                  pl.BlockSpec((tt, S5_WIDTH), lambda b, j: (row(b, j), cg)),
                  h_in, vec(S5_NS), vec(S5_NS), vec(S5_NS),
                  full3((S5_BLOCKS, LANE, S5_LANES)), full3((S5_BLOCKS, LANE, S5_LANES)),
                  full3((S5_BLOCKS, S5_LANES, LANE)), full3((S5_BLOCKS, S5_LANES, LANE)),
                  vec(S5_WIDTH), pl.BlockSpec((S5_WIDTH, S5_WIDTH), lambda b, j: (0, 0)), vec(S5_WIDTH)],
        out_specs=[pl.BlockSpec((tt, S5_WIDTH), lambda b, j: (row(b, j), 0)), h_out],
        out_shape=[jax.ShapeDtypeStruct((m, S5_WIDTH), _mix_dtype(t)), h_shape],
        scratch_shapes=[pltpu.VMEM((tt, S5_NS), f32), pltpu.VMEM((tt, S5_NS), f32),
                        pltpu.VMEM((carry_rows, S5_NS), f32), pltpu.VMEM((carry_rows, S5_NS), f32),
                        pltpu.VMEM((SUBLANE, S5_NS), f32), pltpu.VMEM((SUBLANE, S5_NS), f32),
                        pltpu.VMEM((S5_BLOCKS, LANE, S5_LANES), bf16), pltpu.VMEM((S5_BLOCKS, LANE, S5_LANES), bf16)],
        compiler_params=_params("parallel" if chain else "arbitrary", "arbitrary"),
        name="s5_chain" if chain else "s5_tiles",
    )(h, h, h0, p["lam_re"], p["lam_im"], p["log_step"], p["b_re"], p["b_im"], p["c_re"], p["c_im"],
      p["d"], p["w_glu"], p["b_glu"])


def _gla_kernel(q_ref, k_ref, v_ref, gate_ref, glr_ref, wg_ref, bg_ref, ng_ref, s0_ref, acc_ref, y_ref, s_ref,
                st_ref, b_ref, *, tt, c, nj, chain):
    del acc_ref
    j = pl.program_id(1)

    if chain:
        @pl.when(j == 0)
        def _():
            for hd in range(GLA_HEADS):
                st_ref[hd] = s0_ref[0, hd].T

    la = jax.nn.log_sigmoid(_mm(glr_ref[...], wg_ref[...]) + bg_ref[...]) * (1.0 / GLA_GATE_TAU)
    b_ref[...] = _mm_mask_l(jnp.where(_chunk_tril(tt, c), 1.0, 0.0), la)

    rowi = lax.broadcasted_iota(jnp.int32, (c, GLA_DK), 0)
    lanei = lax.broadcasted_iota(jnp.int32, (c, GLA_DK), 1)
    ones = jnp.ones((GLA_DK, GLA_DK), bf16)
    vzero = jnp.zeros((GLA_DK - c, GLA_DV), f32)
    scale = GLA_DK ** -0.5
    heads = range(GLA_HEADS)
    kl = [slice(hd * GLA_DK, (hd + 1) * GLA_DK) for hd in heads]
    vl = [slice(hd * GLA_DV, (hd + 1) * GLA_DV) for hd in heads]

    def chunk(ci, carry):
        rows = pl.ds(pl.multiple_of(ci * c, c), c)
        qh = [q_ref[rows, kl[hd]] * scale for hd in heads]
        kh = [k_ref[rows, kl[hd]] for hd in heads]
        vh = [v_ref[rows, vl[hd]] for hd in heads]
        bh = [b_ref[rows, kl[hd]] for hd in heads]
        prod = [jnp.concatenate(
            [qh[hd] * kh[hd][jj:jj + 1] * jnp.exp(jnp.where(rowi >= jj, bh[hd] - bh[hd][jj:jj + 1], -jnp.inf))
             for jj in range(c)], axis=0) for hd in heads]
        sums = [_mm(prod[hd], ones) for hd in heads]
        att = []
        for hd in heads:
            a = jnp.zeros((c, GLA_DK), f32)
            for jj in range(c):
                a = a + jnp.where(lanei == jj, sums[hd][jj * c:(jj + 1) * c], 0.0)
            att.append(a)
        s_t = [st_ref[hd] if chain else s0_ref[ci, hd].T for hd in heads]
        o_in = [_mm(att[hd], jnp.concatenate([vh[hd], vzero], axis=0)) for hd in heads]
        o_st = [_mm_nt(qh[hd] * jnp.exp(bh[hd]), s_t[hd]) for hd in heads]
        b_last = [bh[hd][c - 1:c] for hd in heads]
        upd = [_mm_tn(vh[hd], kh[hd] * jnp.exp(b_last[hd] - bh[hd])) for hd in heads]
        for hd in heads:
            s_new = s_t[hd] * jnp.exp(b_last[hd]) + upd[hd]
            if chain:
                st_ref[hd] = s_new
            else:
                s_ref[ci, hd] = s_new.T
            o = o_in[hd] + o_st[hd]
            on = o * lax.rsqrt(jnp.mean(o * o, axis=-1, keepdims=True) + NORM_EPS) * ng_ref[:, vl[hd]]
            y_ref[rows, vl[hd]] = (on * jax.nn.silu(gate_ref[rows, vl[hd]])).astype(y_ref.dtype)
        return carry

    lax.fori_loop(0, tt // c, chunk, 0, unroll=math.gcd(tt // c, 8))

    if chain:
        @pl.when(j == nj - 1)
        def _():
            for hd in range(GLA_HEADS):
                s_ref[0, hd] = st_ref[hd].T


def _stacked(acc, shape):
    if acc is None:
        return jnp.zeros((SUBLANE, LANE), f32), None
    assert acc.shape == shape
    return acc, 1


def _gla(h, s0, l0, acc, l, depth, p, nseq, t, chain):
    m = nseq * t
    acc_shape = (depth, nseq, GLA_HEADS, GLA_DK, GLA_DV)
    acc, alias_to = _stacked(acc, acc_shape)
    if chain:
        c = 16
        tt = min(t, 128)
        nj = t // tt
        grid = (nseq, nj)
        row = lambda b, j: b * nj + j
        st_spec = lambda li: pl.BlockSpec((None, 1, GLA_HEADS, GLA_DK, GLA_DV), lambda b, j: (li, b, 0, 0, 0))
    else:
        c = t
        tt = 8 * c
        nj = m // tt
        grid = (1, nj)
        row = lambda b, j: j
        st_spec = lambda li: pl.BlockSpec((None, tt // c, GLA_HEADS, GLA_DK, GLA_DV), lambda b, j: (li, j, 0, 0, 0))
    vec = lambda n: pl.BlockSpec((1, n), lambda b, j: (0, 0))
    return pl.pallas_call(
        functools.partial(_gla_kernel, tt=tt, c=c, nj=nj, chain=chain),
        grid=grid,
        in_specs=[pl.BlockSpec((tt, GLA_QK), lambda b, j: (row(b, j), COL_G_Q // GLA_QK)),
                  pl.BlockSpec((tt, GLA_QK), lambda b, j: (row(b, j), COL_G_K // GLA_QK)),
                  pl.BlockSpec((tt, GLA_WIDTH), lambda b, j: (row(b, j), COL_G_V // GLA_WIDTH)),
                  pl.BlockSpec((tt, GLA_WIDTH), lambda b, j: (row(b, j), COL_G_GATE // GLA_WIDTH)),
                  pl.BlockSpec((tt, LANE), lambda b, j: (row(b, j), COL_G_LR // LANE)),
                  pl.BlockSpec((LANE, GLA_QK), lambda b, j: (0, 0)), vec(GLA_QK), vec(GLA_WIDTH), st_spec(l0),
                  pl.BlockSpec(memory_space=pl.ANY)],
        out_specs=[pl.BlockSpec((tt, GLA_WIDTH), lambda b, j: (row(b, j), 0)), st_spec(l)],
        out_shape=[jax.ShapeDtypeStruct((m, GLA_WIDTH), _mix_dtype(t)), jax.ShapeDtypeStruct(acc_shape, f32)],
        scratch_shapes=[pltpu.VMEM((GLA_HEADS, GLA_DV, GLA_DK), f32), pltpu.VMEM((tt, GLA_QK), f32)],
        input_output_aliases={} if alias_to is None else {9: alias_to},
        compiler_params=_params("parallel" if chain else "arbitrary", "arbitrary"),
        name="gla_chain" if chain else "gla_tiles",
    )(h, h, h, h, h, p["w_gate"], p["b_gate"], p["norm_g"], s0, acc)


def _seg_sum(x):
    n = x.shape[0]
    w = 2 * LANE
    lane = lax.broadcasted_iota(jnp.int32, (w, w), 1)
    sub = lax.broadcasted_iota(jnp.int32, (w, w), 0)
    ones_bd = jnp.where((lane // RWKV_HEAD) == (sub // RWKV_HEAD), 1.0, 0.0)
    stacked = jnp.concatenate([x[:, i * w:(i + 1) * w] for i in range(RWKV_WIDTH // w)], axis=0)
    s = _mm(stacked, ones_bd)
    return jnp.concatenate([s[i * n:(i + 1) * n] for i in range(RWKV_WIDTH // w)], axis=1)


def _rwkv_kernel(r_ref, k_ref, v_ref, gate_ref, wa_ref, pr_ref, pk_ref, pv_ref, pwa_ref,
                 mur_ref, muk_ref, muv_ref, muwa_ref, w0_ref, w2_ref, a0_ref, a2_ref, kk_ref, ka_ref, rk_ref,
                 lnw_ref, lnb_ref, s0_ref, acc_ref, y_ref, s_ref, lr_ref, lk_ref, lv_ref, lwa_ref,
                 st, c_r, c_k, c_v, c_wa, x_kkg, x_rg, x_bi, x_ki, x_bend, x_kend, x_gam, x_v, x_bon,
                 x_tk, x_rq, x_uv, x_yv, x_y, *, tt, c, nj, chain):
    del acc_ref
    j = pl.program_id(1)
    hd = RWKV_HEAD

    if chain:
        @pl.when(j == 0)
        def _():
            st[...] = s0_ref[0]
            c_r[...] = jnp.broadcast_to(pr_ref[0], c_r.shape)
            c_k[...] = jnp.broadcast_to(pk_ref[0], c_k.shape)
            c_v[...] = jnp.broadcast_to(pv_ref[0], c_v.shape)
            c_wa[...] = jnp.broadcast_to(pwa_ref[0], c_wa.shape)
    else:
        @pl.when(j == 0)
        def _():
            st[...] = jnp.zeros(st.shape, f32)

    def shift(x_ref, c_ref, p_ref, mu_ref, last_ref):
        x = x_ref[...]
        rowid = lax.broadcasted_iota(jnp.int32, x.shape, 0)
        rolled = pltpu.roll(x, 1, 0)
        if chain:
            prev = jnp.where(rowid == 0, c_ref[0:1, :], rolled)
            c_ref[...] = jnp.broadcast_to(x[tt - 1:tt, :], c_ref.shape)
            last_ref[0] = x[tt - 1:tt, :]
        else:
            prev = jnp.where(rowid % c == 0, p_ref[...], rolled)
            for i in range(tt // c):
                last_ref[i:i + 1, :] = x[(i + 1) * c - 1:(i + 1) * c, :]
        return x + (prev - x) * mu_ref[...]

    r = shift(r_ref, c_r, pr_ref, mur_ref, lr_ref)
    k = shift(k_ref, c_k, pk_ref, muk_ref, lk_ref)
    v = shift(v_ref, c_v, pv_ref, muv_ref, lv_ref)
    wa = shift(wa_ref, c_wa, pwa_ref, muwa_ref, lwa_ref)

    w_log = -jax.nn.softplus(-(w0_ref[...] + _mm(jnp.tanh(wa), w2_ref[...]))) - 0.5
    lw = -jnp.exp(w_log)
    a = jax.nn.sigmoid(a0_ref[...] + _mm(wa, a2_ref[...]))

    lane = lax.broadcasted_iota(jnp.int32, (LANE, LANE), 1)
    sub = lax.broadcasted_iota(jnp.int32, (LANE, LANE), 0)
    bd = (lane // RWKV_HEAD) == (sub // RWKV_HEAD)

    kk = k * kk_ref[...]
    kk = kk / jnp.maximum(jnp.sqrt(_seg_sum(kk * kk)), 1e-12)
    k2 = k * (1.0 + (a - 1.0) * ka_ref[...])
    bv = kk * a
    ri = lax.broadcasted_iota(jnp.int32, (tt, tt), 0)
    qi = lax.broadcasted_iota(jnp.int32, (tt, tt), 1)
    same = (ri // c) == (qi // c)
    strict = same & (qi < ri)
    incl = same & (qi <= ri)
    eye = jnp.where(qi == ri, 1.0, 0.0).astype(f32)
    cl = _mm_mask_l(jnp.where(incl, 1.0, 0.0), lw)
    if c == tt:
        cll = jnp.broadcast_to(cl[tt - 1:tt, :], cl.shape)
    else:
        cll = _mm_mask_l(jnp.where(same, 1.0, 0.0), lw)
    ginv = jnp.exp(-cl)
    gend = jnp.exp(cll - cl)
    x_kkg[...] = kk * jnp.exp(cl - lw)
    x_rg[...] = r * jnp.exp(cl)
    x_bi[...] = bv * ginv
    x_ki[...] = k2 * ginv
    x_bend[...] = bv * gend
    x_kend[...] = k2 * gend
    x_gam[...] = jnp.exp(cll)
    x_v[...] = v
    x_bon[...] = _seg_sum(r * k2 * rk_ref[...]) * v

    head_a = lax.broadcasted_iota(jnp.int32, (tt, LANE), 1) < RWKV_HEAD

    zero_tt = jnp.zeros((tt, tt), f32)

    def tri_inv(ams):
        def bdiag(x):
            return jnp.concatenate([jnp.concatenate([x[:, 0:tt], zero_tt], axis=1),
                                    jnp.concatenate([zero_tt, x[:, tt:2 * tt]], axis=1)], axis=0)

        if tt % LANE:
            ns = [-am for am in ams]
            ts = [eye + n for n in ns]
            if c > 2:
                ns = [_mm(n, n) for n in ns]
                width = 4
                while width < c:
                    both = [_mm(jnp.concatenate([t, n], axis=0), n) for t, n in zip(ts, ns)]
                    ts = [t + b[0:tt] for t, b in zip(ts, both)]
                    ns = [b[tt:2 * tt] for b in both]
                    width *= 2
                ts = [t + _mm(t, n) for t, n in zip(ts, ns)]
            return [(ts[i], ts[i + 1]) for i in range(0, len(ts), 2)]
        ns = [jnp.concatenate([-ams[i], -ams[i + 1]], axis=1) for i in range(0, len(ams), 2)]
        eye2 = jnp.concatenate([eye, eye], axis=1)
        ts = [eye2 + n for n in ns]
        if c > 2:
            ns = [_mm(n, bdiag(n)) for n in ns]
            width = 4
            while width < c:
                both = [_mm(jnp.concatenate([t, n], axis=0), bdiag(n)) for t, n in zip(ts, ns)]
                ts = [t + b[0:tt] for t, b in zip(ts, both)]
                ns = [b[tt:2 * tt] for b in both]
                width *= 2
            last = [_mm(t, bdiag(n)) for t, n in zip(ts, ns)]
            ts = [t + m for t, m in zip(ts, last)]
        return ts

    wide = tt % LANE == 0

    def join(xa, xb):
        return jnp.concatenate([xa, xb], axis=1) if wide else (xa, xb)

    def pair_mm(lhs, rhs):
        m = (lax.broadcasted_iota(jnp.int32, (1, rhs.shape[1]), 1) % LANE) < RWKV_HEAD
        if wide:
            return _mm(lhs, jnp.concatenate([jnp.where(m, rhs, 0.0), jnp.where(m, 0.0, rhs)], axis=0))
        return jnp.where(m, _mm(lhs[0], rhs), _mm(lhs[1], rhs))

    def lanes(p):
        return slice(p * LANE, (p + 1) * LANE)

    for g0 in range(0, RWKV_PAIRS, RWKV_GROUP):
        grp = list(range(g0, g0 + RWKV_GROUP))
        kkg = [x_kkg[:, lanes(p)] for p in grp]
        rg = [x_rg[:, lanes(p)] for p in grp]
        vp = [x_v[:, lanes(p)] for p in grp]
        lhs = [jnp.concatenate([jnp.where(head_a, a, 0.0), jnp.where(head_a, 0.0, a),
                                jnp.where(head_a, b, 0.0), jnp.where(head_a, 0.0, b)], axis=0)
               for a, b in zip(kkg, rg)]
        if tt % LANE == 0:
            pbk = [_mm_nt(x, jnp.concatenate([x_bi[:, lanes(p)], x_ki[:, lanes(p)]], axis=0))
                   for x, p in zip(lhs, grp)]
            pb = [x[:, 0:tt] for x in pbk]
            pk = [x[:, tt:2 * tt] for x in pbk]
        else:
            pb = [_mm_nt(x, x_bi[:, lanes(p)]) for x, p in zip(lhs, grp)]
            pk = [_mm_nt(x, x_ki[:, lanes(p)]) for x, p in zip(lhs, grp)]
        def blk(x, hh, mask):
            return jnp.where(mask, x[hh * tt:(hh + 1) * tt], 0.0)

        tinv = tri_inv([blk(x, hh, strict) for x in pb for hh in (0, 1)])
        ar = [join(blk(x, 2, incl), blk(x, 3, incl)) for x in pb]
        bmr = [join(jnp.concatenate([blk(x, 0, strict), blk(x, 2, incl)], axis=0),
                    jnp.concatenate([blk(x, 1, strict), blk(x, 3, incl)], axis=0)) for x in pk]
        mv = [pair_mm(x, vp[i]) for i, x in enumerate(bmr)]
        mt = [pair_mm(x, jnp.concatenate([mv[i][0:tt], kkg[i]], axis=1)) for i, x in enumerate(tinv)]
        uv = [-x[:, 0:LANE] for x in mt]
        tk = [x[:, LANE:2 * LANE] for x in mt]
        mr = [pair_mm(x, jnp.concatenate([uv[i], tk[i]], axis=1)) for i, x in enumerate(ar)]
        for i, p in enumerate(grp):
            x_uv[:, lanes(p)] = uv[i]
            x_tk[:, lanes(p)] = tk[i]
            x_yv[:, lanes(p)] = mr[i][:, 0:LANE] + mv[i][tt:2 * tt]
            x_rq[:, lanes(p)] = rg[i] - mr[i][:, LANE:2 * LANE]

    def chunk(ci, carry):
        row0 = pl.multiple_of(ci * c, c)
        rows = pl.ds(row0, c)
        pairs = range(RWKV_PAIRS)
        if not chain:
            for p in pairs:
                st[p, 0:hd, 0:hd] = s0_ref[ci, 2 * p]
                st[p, hd:2 * hd, hd:2 * hd] = s0_ref[ci, 2 * p + 1]
        s = [st[p] for p in pairs]
        zy = [_mm_nt(jnp.concatenate([x_tk[rows, lanes(p)], x_rq[rows, lanes(p)]], axis=0), s[p]) for p in pairs]
        u = [x_uv[rows, lanes(p)] - zy[p][0:c] for p in pairs]
        for p in pairs:
            x_y[rows, lanes(p)] = zy[p][c:2 * c] + x_yv[rows, lanes(p)]
        upd = [_mm_tn(jnp.concatenate([u[p], x_v[rows, lanes(p)]], axis=0),
                      jnp.concatenate([x_bend[rows, lanes(p)], x_kend[rows, lanes(p)]], axis=0)) for p in pairs]
        for p in pairs:
            s_new = s[p] * x_gam[pl.ds(row0, 1), lanes(p)] + jnp.where(bd, upd[p], 0.0)
            if chain:
                st[p] = s_new
            else:
                s_ref[ci, 2 * p] = s_new[0:hd, 0:hd]
                s_ref[ci, 2 * p + 1] = s_new[hd:2 * hd, hd:2 * hd]
        return carry

    lax.fori_loop(0, tt // c, chunk, 0, unroll=1 if chain else 2)

    y = x_y[...]
    mean = _seg_sum(y) * (1.0 / RWKV_HEAD)
    yc = y - mean
    var = _seg_sum(yc * yc) * (1.0 / RWKV_HEAD)
    yn = yc * lax.rsqrt(var + RWKV_GN_EPS) * lnw_ref[...] + lnb_ref[...] + x_bon[...]
    y_ref[...] = (yn * jax.nn.silu(gate_ref[...])).astype(y_ref.dtype)

    if chain:
        @pl.when(j == nj - 1)
        def _():
            s_ref[0] = st[...]


def _rwkv(h, prev, s0, l0, acc, l, depth, p, nseq, t, chain):
    m = nseq * t
    w = RWKV_WIDTH
    acc_shape = (depth, nseq) + ((RWKV_PAIRS, LANE, LANE) if chain else (RWKV_HEADS, RWKV_HEAD, RWKV_HEAD))
    acc, alias_to = _stacked(acc, acc_shape)
    if chain:
        tt = min(t, 128)
        c = tt
        nj = t // tt
        grid = (nseq, nj)
        row = lambda b, j: b * nj + j
        prev = tuple(x[:, None, :] for x in prev)
        prow = lambda n: pl.BlockSpec((1, 1, n), lambda b, j: (b, 0, 0))
        last_spec = prow
        last_shape = lambda n: jax.ShapeDtypeStruct((nseq, 1, n), f32)
        st_spec = lambda li: pl.BlockSpec((None, 1, RWKV_PAIRS, LANE, LANE), lambda b, j: (li, b, 0, 0, 0))
    else:
        c = t
        tt = 64
        nj = m // tt
        grid = (1, nj)
        row = lambda b, j: j
        prev = tuple(jnp.repeat(x, t, axis=0) for x in prev)
        prow = lambda n: pl.BlockSpec((tt, n), lambda b, j: (j, 0))
        last_spec = lambda n: pl.BlockSpec((tt // c, n), lambda b, j: (j, 0))
        last_shape = lambda n: jax.ShapeDtypeStruct((nseq, n), f32)
        st_spec = lambda li: pl.BlockSpec((None, tt // c, RWKV_HEADS, RWKV_HEAD, RWKV_HEAD),
                                          lambda b, j: (li, j, 0, 0, 0))
    vec = lambda n: pl.BlockSpec((1, n), lambda b, j: (0, 0))
    tok = lambda col: pl.BlockSpec((tt, w), lambda b, j: (row(b, j), col // w))
    big = lambda: pltpu.VMEM((tt, w), f32)
    y, acc, *last = pl.pallas_call(
        functools.partial(_rwkv_kernel, tt=tt, c=c, nj=nj, chain=chain),
        grid=grid,
        in_specs=[tok(COL_RW_R), tok(COL_RW_K), tok(COL_RW_V), tok(COL_RW_GATE),
                  pl.BlockSpec((tt, LANE), lambda b, j: (row(b, j), COL_RW_WA // LANE)),
                  prow(w), prow(w), prow(w), prow(LANE),
                  vec(w), vec(w), vec(w), vec(LANE),
                  vec(w), pl.BlockSpec((LANE, w), lambda b, j: (0, 0)), vec(w),
                  pl.BlockSpec((LANE, w), lambda b, j: (0, 0)), vec(w), vec(w), vec(w), vec(w), vec(w), st_spec(l0),
                  pl.BlockSpec(memory_space=pl.ANY)],
        out_specs=[pl.BlockSpec((tt, w), lambda b, j: (row(b, j), 0)), st_spec(l),
                   last_spec(w), last_spec(w), last_spec(w), last_spec(LANE)],
        out_shape=[jax.ShapeDtypeStruct((m, w), _mix_dtype(t)), jax.ShapeDtypeStruct(acc_shape, f32),
                   last_shape(w), last_shape(w), last_shape(w), last_shape(LANE)],
        input_output_aliases={} if alias_to is None else {23: alias_to},
        scratch_shapes=[pltpu.VMEM((RWKV_PAIRS, LANE, LANE), f32),
                        pltpu.VMEM((SUBLANE, w), f32), pltpu.VMEM((SUBLANE, w), f32),
                        pltpu.VMEM((SUBLANE, w), f32), pltpu.VMEM((SUBLANE, LANE), f32)] + [big() for _ in range(14)],
        compiler_params=_params("parallel" if chain else "arbitrary", "arbitrary"),
        name="rwkv_chain" if chain else "rwkv_tiles",
    )(h, h, h, h, h, *prev, p["mu_r"], p["mu_k"], p["mu_v"], p["mu_wa"], p["w0"], p["w2"], p["a0"], p["a2"],
      p["k_k"], p["k_a"], p["r_k"], p["ln_w"], p["ln_b"], s0, acc)
    shift = jnp.concatenate([x.reshape(nseq, -1) for x in last], axis=1)
    return y, acc, shift


def _xattn_kernel(q_ref, k_ref, v_ref, o_ref, *, nb, tq):
    scale = XA_HEAD_DIM ** -0.5
    probs = [(i, hd) for i in range(nb) for hd in range(XA_HEADS)]
    ln = [slice(hd * XA_HEAD_DIM, (hd + 1) * XA_HEAD_DIM) for hd in range(XA_HEADS)]
    qrows = [slice(i * tq, (i + 1) * tq) for i in range(nb)]
    mrows = [pl.ds(hd, N_MEM, stride=XA_HEADS) for hd in range(XA_HEADS)]
    s = [_mm_nt(q_ref[qrows[i], ln[hd]] * scale, k_ref[i, mrows[hd], :]) for i, hd in probs]
    s = [x - jnp.max(x, axis=-1, keepdims=True) for x in s]
    e = [jnp.exp(x) for x in s]
    pr = [x / jnp.sum(x, axis=-1, keepdims=True) for x in e]
    o = [_mm(pr[n], v_ref[i, mrows[hd], :]) for n, (i, hd) in enumerate(probs)]
    for n, (i, hd) in enumerate(probs):
        o_ref[qrows[i], ln[hd]] = o[n].astype(o_ref.dtype)


def _xattn(h, mem_k, mem_v, l0, nseq, t):
    m = nseq * t
    tq = min(t, 512)
    nj = t // tq
    nb = 8 if (nj == 1 and nseq % 8 == 0) else 1
    mem_spec = pl.BlockSpec((None, nb, N_MEM * XA_HEADS, XA_HEAD_DIM), lambda b, j: (l0, b, 0, 0))
    return pl.pallas_call(
        functools.partial(_xattn_kernel, nb=nb, tq=tq),
        grid=(nseq // nb, nj),
        in_specs=[pl.BlockSpec((nb * tq, XA_WIDTH), lambda b, j: (b * nj + j, COL_XA_Q // XA_WIDTH)),
                  mem_spec, mem_spec],
        out_specs=pl.BlockSpec((nb * tq, XA_WIDTH), lambda b, j: (b * nj + j, 0)),
        out_shape=jax.ShapeDtypeStruct((m, XA_WIDTH), _mix_dtype(t)),
        compiler_params=_params("parallel", "arbitrary"),
        name="xattn",
    )(h, mem_k, mem_v)


_RW0 = 5136
W_IN_SEGMENTS = ((_RW0, _RW0 + 1536), (_RW0 + 1536, _RW0 + 3072), (_RW0 + 3072, _RW0 + 4608), (9872, 11408),
                 (0, 1024), (1024, 2048), (3072, 4096), (4096, 5120), (2048, 2560), (2560, 3072), (11408, 11920),
                 (_RW0 + 4608, _RW0 + 4736), (5120, 5136))


def _reorder_cast_kernel(w_ref, o_ref):
    dst = 0
    for a, b in W_IN_SEGMENTS:
        o_ref[dst:dst + (b - a), :] = w_ref[a:b, :].astype(bf16)
        dst += b - a
    o_ref[dst:, :] = jnp.zeros((H_COLS - dst, o_ref.shape[1]), bf16)


def _prep_w_in(w_in, tk=256):
    depth, d, n = w_in.shape
    return pl.pallas_call(
        _reorder_cast_kernel,
        grid=(depth, d // tk),
        in_specs=[pl.BlockSpec((None, n, tk), lambda l, i: (l, 0, i))],
        out_specs=pl.BlockSpec((None, H_COLS, tk), lambda l, i: (l, 0, i)),
        out_shape=jax.ShapeDtypeStruct((depth, H_COLS, d), bf16),
        compiler_params=_params("parallel", "parallel"),
        name="w_in_layout",
    )(jnp.swapaxes(w_in, 1, 2))


def _block_diag_s5(b, c):
    gl = S5_GROUPS // S5_BLOCKS
    eye = jnp.eye(gl, dtype=b.dtype)
    bb = b.reshape(S5_BLOCKS, gl, S5_STATE, S5_GROUP)
    bbd = jnp.einsum("kgpn,gh->kgnhp", bb, eye).reshape(S5_BLOCKS, gl * S5_GROUP, gl * S5_STATE)
    cc = c.reshape(S5_BLOCKS, gl, S5_GROUP, S5_STATE)
    cbd = jnp.einsum("kgnp,gh->kgphn", cc, eye).reshape(S5_BLOCKS, gl * S5_STATE, gl * S5_GROUP)
    return bbd, cbd


def _layer_params(l, a):
    row = lambda x: x.reshape(1, -1)
    bre, cre = _block_diag_s5(a["s5_b_re"][l], a["s5_c_re"][l])
    bim, cim = _block_diag_s5(a["s5_b_im"][l], a["s5_c_im"][l])
    s5 = dict(lam_re=row(a["s5_lam_re"][l]), lam_im=row(a["s5_lam_im"][l]),
              log_step=row(jnp.broadcast_to(a["s5_log_step"][l][:, None], (S5_GROUPS, S5_STATE))),
              b_re=bre, b_im=bim, c_re=cre.astype(bf16), c_im=cim.astype(bf16),
              d=row(a["s5_d"][l]), w_glu=a["s5_w_glu"][l].astype(bf16), b_glu=row(a["s5_b_glu"][l]))
    wg = jnp.zeros((LANE, GLA_QK), f32).at[:GLA_GATE_RANK].set(a["gla_w_gate"][l])
    gla = dict(w_gate=wg, b_gate=row(a["gla_b_gate"][l]), norm_g=row(a["gla_norm_g"][l]))
    mu = a["rw_mu"][l]
    w = RWKV_WIDTH
    z64 = jnp.zeros((64, w), f32)
    rw = dict(mu_r=row(mu[0:w]), mu_k=row(mu[w:2 * w]), mu_v=row(mu[2 * w:3 * w]), mu_wa=row(mu[3 * w:]),
              w0=row(a["rw_w0"][l]), w2=jnp.concatenate([a["rw_w2"][l], z64], 0).astype(bf16),
              a0=row(a["rw_a0"][l]), a2=jnp.concatenate([z64, a["rw_a2"][l]], 0).astype(bf16),
              k_k=row(a["rw_k_k"][l]), k_a=row(a["rw_k_a"][l]), r_k=row(a["rw_r_k"][l]),
              ln_w=row(a["rw_ln_w"][l]), ln_b=row(a["rw_ln_b"][l]))
    return s5, gla, rw


def _unpack_rwkv_state(s):
    b = s.shape[0]
    s = s.reshape(b, RWKV_PAIRS, 2, RWKV_HEAD, 2, RWKV_HEAD)
    return jnp.stack([s[:, :, 0, :, 0, :], s[:, :, 1, :, 1, :]], axis=2).reshape(b, RWKV_HEADS, RWKV_HEAD, RWKV_HEAD)


def _split_shift(x):
    w = RWKV_WIDTH
    return tuple(x[:, a:b] for a, b in ((0, w), (w, 2 * w), (2 * w, 3 * w), (3 * w, 3 * w + LANE)))


def _mixer_layer(x, xn, nseq, t, mem_k, mem_v, s_s5, s_gla, s_rw, l0, acc_gla, acc_rw, l, depth, prev, chain, lp,
                 w_in_t, w_out_b, g_pre, g_post, g_next):
    s5p, glap, rwp = lp
    h = _norm_mm(x, g_pre, w_in_t, l) if xn is None else _proj(xn, w_in_t, l)
    y_s5, s5_state = _s5(h, s_s5, l0, s5p, nseq, t, chain)
    y_gla, s_gla_new = _gla(h, s_gla, l0, acc_gla, l, depth, glap, nseq, t, chain)
    y_rw, s_rw_new, shift = _rwkv(h, prev, s_rw, l0, acc_rw, l, depth, rwp, nseq, t, chain)
    y_xa = _xattn(h, mem_k, mem_v, l0, nseq, t)
    out_fn = _outproj_split if nseq * t >= 4096 else _outproj
    x_new, xn_new = out_fn((y_s5, y_gla, y_rw, y_xa), w_out_b, l, g_post, x, g_next)
    return x_new, xn_new, s5_state, s_gla_new, s_rw_new, shift


def kernel(x_prompt, x_sample, mem_prompt, cache_mem_k, cache_mem_v, state_s5, state_gla, state_rwkv, state_rwkv_shift, g_pre, g_post, w_in, w_out, s5_lam_re, s5_lam_im, s5_log_step, s5_b_re, s5_b_im, s5_c_re, s5_c_im, s5_d, s5_w_glu, s5_b_glu, gla_w_gate, gla_b_gate, gla_norm_g, rw_mu, rw_w0, rw_w2, rw_a0, rw_a2, rw_k_k, rw_k_a, rw_r_k, rw_ln_w, rw_ln_b, mem_g, w_mk, w_mv):
    raw = dict(s5_lam_re=s5_lam_re, s5_lam_im=s5_lam_im, s5_log_step=s5_log_step, s5_b_re=s5_b_re, s5_b_im=s5_b_im,
               s5_c_re=s5_c_re, s5_c_im=s5_c_im, s5_d=s5_d, s5_w_glu=s5_w_glu, s5_b_glu=s5_b_glu,
               gla_w_gate=gla_w_gate, gla_b_gate=gla_b_gate, gla_norm_g=gla_norm_g, rw_mu=rw_mu, rw_w0=rw_w0,
               rw_w2=rw_w2, rw_a0=rw_a0, rw_a2=rw_a2, rw_k_k=rw_k_k, rw_k_a=rw_k_a, rw_r_k=rw_r_k,
               rw_ln_w=rw_ln_w, rw_ln_b=rw_ln_b)
    bp, tp, d = x_prompt.shape
    bs, ts, _ = x_sample.shape
    depth = w_in.shape[0]
    w_in_t = _prep_w_in(w_in)
    w_out_b = w_out.astype(bf16)
    w_kv_t = jnp.swapaxes(jnp.concatenate([w_mk, w_mv], axis=2), 1, 2).astype(bf16)
    mem_rows = N_MEM * XA_HEADS

    xp = x_prompt.reshape(bp * tp, d)
    xs = x_sample.reshape(bs * ts, d)
    mem2d = mem_prompt.reshape(bp * N_MEM, d)
    zp = lambda *s: jnp.zeros(s, f32)
    prev_p = _split_shift(zp(bp, RWKV_SHIFT_COLS))

    gla_p = gla_s = rw_p = rw_s = xnp = xns = None
    zero_gla = zp(1, bp, GLA_HEADS, GLA_DK, GLA_DV)
    zero_rw = zp(1, bp, RWKV_PAIRS, LANE, LANE)
    zero_s5 = zp(1, bp, 2, S5_NS)
    s5_in = state_s5.reshape(depth, bs, S5_NS * 2).transpose(0, 2, 1)
    mem_k_s = cache_mem_k.reshape(depth, bs, mem_rows, XA_HEAD_DIM)
    mem_v_s = cache_mem_v.reshape(depth, bs, mem_rows, XA_HEAD_DIM)

    outs = [[] for _ in range(6)]
    for l in range(depth):
        lp = _layer_params(l, raw)
        gp, gq = g_pre[l].reshape(1, d), g_post[l].reshape(1, d)
        gn = g_pre[l + 1].reshape(1, d) if l + 1 < depth else None
        kv = _norm_mm(mem2d, mem_g[l].reshape(1, d), w_kv_t, l)
        k_mem = kv[:, :XA_WIDTH].reshape(bp, N_MEM, XA_WIDTH)
        v_mem = kv[:, XA_WIDTH:].reshape(bp, N_MEM, XA_WIDTH)
        xp, xnp, h1, gla_p, rw_p, p1 = _mixer_layer(
            xp, xnp, bp, tp, k_mem.reshape(1, bp, mem_rows, XA_HEAD_DIM), v_mem.reshape(1, bp, mem_rows, XA_HEAD_DIM),
            zero_s5, zero_gla, zero_rw, 0, gla_p, rw_p, l, depth,
            prev_p, True, lp, w_in_t, w_out_b, gp, gq, gn)
        xs, xns, h2, gla_s, rw_s, p2 = _mixer_layer(
            xs, xns, bs, ts, mem_k_s, mem_v_s,
            s5_in, state_gla, state_rwkv, l, gla_s, rw_s, l, depth,
            _split_shift(state_rwkv_shift[l]), False, lp, w_in_t, w_out_b, gp, gq, gn)
        vals = (k_mem.reshape(bp, N_MEM, XA_HEADS, XA_HEAD_DIM), v_mem.reshape(bp, N_MEM, XA_HEADS, XA_HEAD_DIM),
                h1, h2, p1, p2)
        for o, val in zip(outs, vals):
            o.append(val)

    mk, mv, s5_p, s5_s, sh_p, sh_s = (jnp.stack(o) for o in outs)
    s5_p = s5_p.transpose(0, 1, 3, 2).reshape(depth, bp, S5_GROUPS, S5_STATE, 2)
    s5_s = s5_s.transpose(0, 2, 1).reshape(depth, bs, S5_GROUPS, S5_STATE, 2)
    rw_p_heads = _unpack_rwkv_state(rw_p.reshape(depth * bp, RWKV_PAIRS, LANE, LANE))
    rw_p_heads = rw_p_heads.reshape(depth, bp, RWKV_HEADS, RWKV_HEAD, RWKV_HEAD)
    return (xp.reshape(bp, tp, d), xs.reshape(bs, ts, d), mk, mv, s5_p, s5_s, gla_p, gla_s, rw_p_heads, rw_s,
            sh_p, sh_s)
```

```python
import functools
import math

import jax
import jax.numpy as jnp
from jax import lax
from jax.experimental import pallas as pl
from jax.experimental.pallas import tpu as pltpu

f32 = jnp.float32
bf16 = jnp.bfloat16

D_MODEL = 4096
N_MEM = 256
NORM_EPS = 1e-6

S5_WIDTH = 1024
S5_GROUP = 16
S5_GROUPS = 64
S5_STATE = 64
S5_NS = S5_GROUPS * S5_STATE
S5_BLOCKS = 8
S5_LANES = 512

GLA_WIDTH = 1024
GLA_HEADS = 4
GLA_DV = 256
GLA_DK = 128
GLA_QK = 512
GLA_GATE_RANK = 16
GLA_GATE_TAU = 16.0

RWKV_WIDTH = 1536
RWKV_HEAD = 64
RWKV_HEADS = 24
RWKV_PAIRS = 12
RWKV_GROUP = 12
RWKV_GN_EPS = RWKV_HEAD * 1e-5
RWKV_SHIFT_COLS = 3 * RWKV_WIDTH + 128

XA_HEADS = 4
XA_HEAD_DIM = 128
XA_WIDTH = 512

LANE = 128
SUBLANE = 8
VMEM_LIMIT = 56 * 1024 * 1024

H_COLS = 12288
COL_RW_R, COL_RW_K, COL_RW_V, COL_RW_GATE = 0, 1536, 3072, 4608
COL_S5_U, COL_S5_GATE, COL_G_V, COL_G_GATE = 6144, 7168, 8192, 9216
COL_G_Q, COL_G_K, COL_XA_Q = 10240, 10752, 11264
COL_RW_WA, COL_G_LR = 11776, 11904


def _mm(a, b):
    return jnp.dot(a.astype(bf16), b.astype(bf16), preferred_element_type=f32)


def _mm_nt(a, b):
    return lax.dot_general(a.astype(bf16), b.astype(bf16), (((1,), (1,)), ((), ())), preferred_element_type=f32)


def _mm_tn(a, b):
    return lax.dot_general(a.astype(bf16), b.astype(bf16), (((0,), (0,)), ((), ())), preferred_element_type=f32)


def _split2(x):
    hi = x.astype(bf16)
    return hi, (x - hi.astype(f32)).astype(bf16)


def _mm_mask_l(mask01, x):
    hi, lo = _split2(x)
    m = mask01.astype(bf16)
    return jnp.dot(m, hi, preferred_element_type=f32) + jnp.dot(m, lo, preferred_element_type=f32)


def _mix_dtype(t):
    return bf16 if t % (2 * SUBLANE) == 0 else f32


def _params(*sem):
    return pltpu.CompilerParams(dimension_semantics=sem, vmem_limit_bytes=VMEM_LIMIT)


def _chunk_tril(n, c):
    r = lax.broadcasted_iota(jnp.int32, (n, n), 0)
    q = lax.broadcasted_iota(jnp.int32, (n, n), 1)
    return ((r // c) == (q // c)) & (q <= r)


def _norm_mm_kernel(x_ref, g_ref, w_ref, o_ref, xn_ref):
    @pl.when(pl.program_id(1) == 0)
    def _():
        rows = 256
        for r0 in range(0, x_ref.shape[0], rows):
            x = x_ref[r0:r0 + rows, :]
            ms = jnp.mean(x * x, axis=-1, keepdims=True)
            xn_ref[r0:r0 + rows, :] = (x * lax.rsqrt(ms + NORM_EPS) * g_ref[...]).astype(bf16)

    o_ref[...] = lax.dot_general(xn_ref[...], w_ref[...], (((1,), (1,)), ((), ())), preferred_element_type=f32)


def _norm_mm(x, g, wt, l):
    m, d = x.shape
    n = wt.shape[1]
    tm = min(m, 1024)
    tn = 768 if n % 768 == 0 else 512
    xmode = dict(pipeline_mode=pl.Buffered(1))
    return pl.pallas_call(
        _norm_mm_kernel,
        grid=(m // tm, n // tn),
        in_specs=[pl.BlockSpec((tm, d), lambda i, j: (i, 0), **xmode),
                  pl.BlockSpec((1, d), lambda i, j: (0, 0)),
                  pl.BlockSpec((None, tn, d), lambda i, j: (l, j, 0))],
        out_specs=pl.BlockSpec((tm, tn), lambda i, j: (i, j)),
        out_shape=jax.ShapeDtypeStruct((m, n), f32),
        scratch_shapes=[pltpu.VMEM((tm, d), bf16)],
        compiler_params=_params("parallel", "arbitrary"),
        name="norm_mm",
    )(x, g, wt)


def _proj_kernel(xn_ref, w_ref, o_ref):
    o_ref[...] = lax.dot_general(xn_ref[...], w_ref[...], (((1,), (1,)), ((), ())), preferred_element_type=f32)


def _proj(xn, wt, l):
    m, d = xn.shape
    n = wt.shape[1]
    tm = min(m, 1024)
    tn = 768 if n % 768 == 0 else 512
    return pl.pallas_call(
        _proj_kernel,
        grid=(m // tm, n // tn),
        in_specs=[pl.BlockSpec((tm, d), lambda i, j: (i, 0)),
                  pl.BlockSpec((None, tn, d), lambda i, j: (l, j, 0))],
        out_specs=pl.BlockSpec((tm, tn), lambda i, j: (i, j)),
        out_shape=jax.ShapeDtypeStruct((m, n), f32),
        compiler_params=_params("parallel", "arbitrary"),
        name="proj",
    )(xn, wt)


def _outproj_kernel(a1, a2, a3, a4, w_ref, g_ref, gn_ref, x_ref, o_ref, *xn_ref, nj, tn):
    j = pl.program_id(1)
    acc = None
    k0 = 0
    for a in (a1, a2, a3, a4):
        part = _mm(a[...], w_ref[k0:k0 + a.shape[1], :])
        acc = part if acc is None else acc + part
        k0 += a.shape[1]
    o_ref[:, pl.ds(pl.multiple_of(j * tn, tn), tn)] = acc

    @pl.when(j == nj - 1)
    def _():
        cols = [slice(k * tn, (k + 1) * tn) for k in range(nj)]
        ssq = None
        for sl in cols:
            p = o_ref[:, sl]
            s = jnp.sum(p * p, axis=-1, keepdims=True)
            ssq = s if ssq is None else ssq + s
        inv = lax.rsqrt(ssq * (1.0 / D_MODEL) + NORM_EPS)
        ssq = None
        for sl in cols:
            new = x_ref[:, sl] + o_ref[:, sl] * inv * g_ref[:, sl]
            o_ref[:, sl] = new
            s = jnp.sum(new * new, axis=-1, keepdims=True)
            ssq = s if ssq is None else ssq + s
        if xn_ref:
            inv = lax.rsqrt(ssq * (1.0 / D_MODEL) + NORM_EPS)
            for sl in cols:
                xn_ref[0][:, sl] = (o_ref[:, sl] * inv * gn_ref[:, sl]).astype(bf16)


def _outproj(parts, w, l, g, x, g_next, tn=512):
    m, d = x.shape
    nj = d // tn
    tm = 512 if parts[0].dtype == bf16 else 256
    emit = g_next is not None
    row_spec = pl.BlockSpec((tm, d), lambda i, j: (i, 0))
    in_specs = [pl.BlockSpec((tm, p.shape[1]), lambda i, j: (i, 0)) for p in parts]
    in_specs += [pl.BlockSpec((None, d, tn), lambda i, j: (l, 0, j))]
    in_specs += [pl.BlockSpec((1, d), lambda i, j: (0, 0)), pl.BlockSpec((1, d), lambda i, j: (0, 0)),
                 pl.BlockSpec((tm, d), lambda i, j: (i, 0), pipeline_mode=pl.Buffered(1))]
    out = pl.pallas_call(
        functools.partial(_outproj_kernel, nj=nj, tn=tn),
        grid=(m // tm, nj),
        in_specs=in_specs,
        out_specs=[row_spec, row_spec] if emit else [row_spec],
        out_shape=[jax.ShapeDtypeStruct((m, d), f32)] + ([jax.ShapeDtypeStruct((m, d), bf16)] if emit else []),
        compiler_params=_params("parallel", "arbitrary"),
        name="outproj",
    )(*parts, w, g, g_next if emit else g, x)
    return (out[0], out[1]) if emit else (out[0], None)


def _mm4_kernel(a1, a2, a3, a4, w_ref, o_ref):
    acc = None
    k0 = 0
    for a in (a1, a2, a3, a4):
        part = _mm(a[...], w_ref[k0:k0 + a.shape[1], :])
        acc = part if acc is None else acc + part
        k0 += a.shape[1]
    o_ref[...] = acc


def _post_kernel(p_ref, x_ref, g_ref, gn_ref, o_ref, *xn_ref):
    p = p_ref[...]
    inv = lax.rsqrt(jnp.mean(p * p, axis=-1, keepdims=True) + NORM_EPS)
    new = x_ref[...] + p * inv * g_ref[...]
    o_ref[...] = new
    if xn_ref:
        inv2 = lax.rsqrt(jnp.mean(new * new, axis=-1, keepdims=True) + NORM_EPS)
        xn_ref[0][...] = (new * inv2 * gn_ref[...]).astype(bf16)


def _outproj_split(parts, w, l, g, x, g_next, tm=1024, tn=1024, tr=256):
    m, d = x.shape
    emit = g_next is not None
    if parts[0].dtype != bf16:
        tm = 512
    proj = pl.pallas_call(
        _mm4_kernel,
        grid=(m // tm, d // tn),
        in_specs=[pl.BlockSpec((tm, p.shape[1]), lambda i, j: (i, 0)) for p in parts]
        + [pl.BlockSpec((None, d, tn), lambda i, j: (l, 0, j))],
        out_specs=pl.BlockSpec((tm, tn), lambda i, j: (i, j)),
        out_shape=jax.ShapeDtypeStruct((m, d), f32),
        compiler_params=_params("parallel", "arbitrary"),
        name="mm4",
    )(*parts, w)
    row = pl.BlockSpec((tr, d), lambda i: (i, 0))
    vec = pl.BlockSpec((1, d), lambda i: (0, 0))
    out = pl.pallas_call(
        _post_kernel,
        grid=(m // tr,),
        in_specs=[row, row, vec, vec],
        out_specs=[row, row] if emit else [row],
        out_shape=[jax.ShapeDtypeStruct((m, d), f32)] + ([jax.ShapeDtypeStruct((m, d), bf16)] if emit else []),
        compiler_params=_params("parallel"),
        name="post",
    )(proj, x, g, g_next if emit else g)
    return (out[0], out[1]) if emit else (out[0], None)


def _s5_kernel(u_ref, gate_ref, h0_ref, lre_ref, lim_ref, step_ref, bre_ref, bim_ref, cre_ref, cim_ref,
               d_ref, wglu_ref, bglu_ref, y_ref, hl_ref, sre, sim, car, cai, pwr, pwi, bbr, bbi, *, tt, nj, chain):
    j = pl.program_id(1)
    nrt = tt // SUBLANE

    @pl.when(j == 0)
    def _():
        lr = lre_ref[...]
        li = lim_ref[...]
        st = jnp.exp(step_ref[...])
        kidx = (lax.broadcasted_iota(jnp.int32, (SUBLANE, S5_NS), 0) + 1).astype(f32)
        mag = jnp.exp(kidx * (lr * st))
        ang = kidx * (li * st)
        p_re = mag * jnp.cos(ang)
        p_im = mag * jnp.sin(ang)
        pwr[...] = p_re
        pwi[...] = p_im
        ab_re = p_re[0:1]
        ab_im = p_im[0:1]
        den = lr * lr + li * li
        f_re = ((ab_re - 1.0) * lr + ab_im * li) / den
        f_im = (ab_im * lr - (ab_re - 1.0) * li) / den
        for blk in range(S5_BLOCKS):
            cols = slice(blk * S5_LANES, (blk + 1) * S5_LANES)
            fr = f_re[:, cols]
            fi = f_im[:, cols]
            bbr[blk] = (fr * bre_ref[blk] - fi * bim_ref[blk]).astype(bf16)
            bbi[blk] = (fr * bim_ref[blk] + fi * bre_ref[blk]).astype(bf16)

    pw_re = pwr[...]
    pw_im = pwi[...]
    u = u_ref[...]
    for blk in range(S5_BLOCKS):
        cols = slice(blk * S5_LANES, (blk + 1) * S5_LANES)
        ub = u[:, blk * LANE:(blk + 1) * LANE].astype(bf16)
        sre[:, cols] = jnp.dot(ub, bbr[blk], preferred_element_type=f32)
        sim[:, cols] = jnp.dot(ub, bbi[blk], preferred_element_type=f32)

    @pl.when(j == 0)
    def _():
        if chain:
            car[...] = jnp.broadcast_to(h0_ref[0, 0:1], (SUBLANE, S5_NS))
            cai[...] = jnp.broadcast_to(h0_ref[0, 1:2], (SUBLANE, S5_NS))
        else:
            car[...] = h0_ref[pl.ds(0, S5_NS, stride=2), :].T
            cai[...] = h0_ref[pl.ds(1, S5_NS, stride=2), :].T

    rowi = lax.broadcasted_iota(jnp.int32, (SUBLANE, S5_LANES), 0)
    for ch in range(S5_NS // S5_LANES):
        cols = slice(ch * S5_LANES, (ch + 1) * S5_LANES)
        p_re = pw_re[:, cols]
        p_im = pw_im[:, cols]
        steps = [(d, jnp.where(rowi >= d, p_re[d - 1:d], 0.0), jnp.where(rowi >= d, p_im[d - 1:d], 0.0))
                 for d in (1, 2, 4)]

        def body(r, carry, cols=cols, p_re=p_re, p_im=p_im, steps=steps):
            rows = pl.ds(pl.multiple_of(r * SUBLANE, SUBLANE), SUBLANE)
            xr = sre[rows, cols]
            xi = sim[rows, cols]
            for d, ar, ai in steps:
                sr = pltpu.roll(xr, d, 0)
                si = pltpu.roll(xi, d, 0)
                xr, xi = xr + ar * sr - ai * si, xi + ar * si + ai * sr
            if chain:
                cr, ci = carry
            else:
                seq = pl.ds(j * nrt + r, 1)
                cr = jnp.broadcast_to(car[seq, cols], (SUBLANE, S5_LANES))
                ci = jnp.broadcast_to(cai[seq, cols], (SUBLANE, S5_LANES))
            xr, xi = xr + p_re * cr - p_im * ci, xi + p_re * ci + p_im * cr
            sre[rows, cols] = xr
            sim[rows, cols] = xi
            if chain:
                return (jnp.broadcast_to(xr[SUBLANE - 1:SUBLANE], (SUBLANE, S5_LANES)),
                        jnp.broadcast_to(xi[SUBLANE - 1:SUBLANE], (SUBLANE, S5_LANES)))
            car[seq, cols] = xr[SUBLANE - 1:SUBLANE]
            cai[seq, cols] = xi[SUBLANE - 1:SUBLANE]
            return carry

        if chain:
            cr, ci = lax.fori_loop(0, nrt, body, (car[:, cols], cai[:, cols]))
            car[:, cols] = cr
            cai[:, cols] = ci
        else:
            lax.fori_loop(0, nrt, body, 0)

    if chain:
        hl_ref[0, 0:1] = car[0:1, :]
        hl_ref[0, 1:2] = cai[0:1, :]
    else:
        @pl.when(j == nj - 1)
        def _():
            hl_ref[pl.ds(0, S5_NS, stride=2), :] = car[...].T
            hl_ref[pl.ds(1, S5_NS, stride=2), :] = cai[...].T

    ys = []
    for blk in range(S5_BLOCKS):
        cols = slice(blk * S5_LANES, (blk + 1) * S5_LANES)
        ys.append(_mm(sre[:, cols], cre_ref[blk]) - _mm(sim[:, cols], cim_ref[blk]))
    y = jnp.concatenate(ys, axis=1) + d_ref[...] * u
    z = jax.nn.gelu(y)
    gl = jax.nn.sigmoid(_mm(z, wglu_ref[...]) + bglu_ref[...])
    y_ref[...] = (z * gl * jax.nn.silu(gate_ref[...])).astype(y_ref.dtype)


def _s5(h, h0, l0, p, nseq, t, chain):
    m = nseq * t
    if chain:
        tt = min(t, 512)
        nj = t // tt
        grid = (nseq, nj)
        row = lambda b, j: b * nj + j
        h_in = pl.BlockSpec((None, 1, 2, S5_NS), lambda b, j: (l0, b, 0, 0))
        h_out = pl.BlockSpec((1, 2, S5_NS), lambda b, j: (b, 0, 0))
        h_shape = jax.ShapeDtypeStruct((nseq, 2, S5_NS), f32)
        carry_rows = SUBLANE
    else:
        assert t == SUBLANE and nseq == LANE
        tt = 64
        nj = m // tt
        grid = (1, nj)
        row = lambda b, j: j
        h_in = pl.BlockSpec((None, 2 * S5_NS, nseq), lambda b, j: (l0, 0, 0))
        h_out = pl.BlockSpec((2 * S5_NS, nseq), lambda b, j: (0, 0))
        h_shape = jax.ShapeDtypeStruct((2 * S5_NS, nseq), f32)
        carry_rows = nseq
    cu, cg = COL_S5_U // S5_WIDTH, COL_S5_GATE // S5_WIDTH
    vec = lambda n: pl.BlockSpec((1, n), lambda b, j: (0, 0))
    full3 = lambda s: pl.BlockSpec(s, lambda b, j: (0, 0, 0))
    return pl.pallas_call(
        functools.partial(_s5_kernel, tt=tt, nj=nj, chain=chain),
        grid=grid,
        in_specs=[pl.BlockSpec((tt, S5_WIDTH), lambda b, j: (row(b, j), cu)),
                  pl.BlockSpec((tt, S5_WIDTH), lambda b, j: (row(b, j), cg)),
                  h_in, vec(S5_NS), vec(S5_NS), vec(S5_NS),
                  full3((S5_BLOCKS, LANE, S5_LANES)), full3((S5_BLOCKS, LANE, S5_LANES)),
                  full3((S5_BLOCKS, S5_LANES, LANE)), full3((S5_BLOCKS, S5_LANES, LANE)),
                  vec(S5_WIDTH), pl.BlockSpec((S5_WIDTH, S5_WIDTH), lambda b, j: (0, 0)), vec(S5_WIDTH)],
        out_specs=[pl.BlockSpec((tt, S5_WIDTH), lambda b, j: (row(b, j), 0)), h_out],
        out_shape=[jax.ShapeDtypeStruct((m, S5_WIDTH), _mix_dtype(t)), h_shape],
        scratch_shapes=[pltpu.VMEM((tt, S5_NS), f32), pltpu.VMEM((tt, S5_NS), f32),
                        pltpu.VMEM((carry_rows, S5_NS), f32), pltpu.VMEM((carry_rows, S5_NS), f32),
                        pltpu.VMEM((SUBLANE, S5_NS), f32), pltpu.VMEM((SUBLANE, S5_NS), f32),
                        pltpu.VMEM((S5_BLOCKS, LANE, S5_LANES), bf16), pltpu.VMEM((S5_BLOCKS, LANE, S5_LANES), bf16)],
        compiler_params=_params("parallel" if chain else "arbitrary", "arbitrary"),
        name="s5_chain" if chain else "s5_tiles",
    )(h, h, h0, p["lam_re"], p["lam_im"], p["log_step"], p["b_re"], p["b_im"], p["c_re"], p["c_im"],
      p["d"], p["w_glu"], p["b_glu"])


def _gla_kernel(q_ref, k_ref, v_ref, gate_ref, glr_ref, wg_ref, bg_ref, ng_ref, s0_ref, acc_ref, y_ref, s_ref,
                st_ref, b_ref, *, tt, c, nj, chain):
    del acc_ref
    j = pl.program_id(1)

    if chain:
        @pl.when(j == 0)
        def _():
            for hd in range(GLA_HEADS):
                st_ref[hd] = s0_ref[0, hd].T

    la = jax.nn.log_sigmoid(_mm(glr_ref[...], wg_ref[...]) + bg_ref[...]) * (1.0 / GLA_GATE_TAU)
    b_ref[...] = _mm_mask_l(jnp.where(_chunk_tril(tt, c), 1.0, 0.0), la)

    rowi = lax.broadcasted_iota(jnp.int32, (c, GLA_DK), 0)
    lanei = lax.broadcasted_iota(jnp.int32, (c, GLA_DK), 1)
    ones = jnp.ones((GLA_DK, GLA_DK), bf16)
    vzero = jnp.zeros((GLA_DK - c, GLA_DV), f32)
    scale = GLA_DK ** -0.5
    heads = range(GLA_HEADS)
    kl = [slice(hd * GLA_DK, (hd + 1) * GLA_DK) for hd in heads]
    vl = [slice(hd * GLA_DV, (hd + 1) * GLA_DV) for hd in heads]

    def chunk(ci, carry):
        rows = pl.ds(pl.multiple_of(ci * c, c), c)
        qh = [q_ref[rows, kl[hd]] * scale for hd in heads]
        kh = [k_ref[rows, kl[hd]] for hd in heads]
        vh = [v_ref[rows, vl[hd]] for hd in heads]
        bh = [b_ref[rows, kl[hd]] for hd in heads]
        prod = [jnp.concatenate(
            [qh[hd] * kh[hd][jj:jj + 1] * jnp.exp(jnp.where(rowi >= jj, bh[hd] - bh[hd][jj:jj + 1], -jnp.inf))
             for jj in range(c)], axis=0) for hd in heads]
        sums = [_mm(prod[hd], ones) for hd in heads]
        att = []
        for hd in heads:
            a = jnp.zeros((c, GLA_DK), f32)
            for jj in range(c):
                a = a + jnp.where(lanei == jj, sums[hd][jj * c:(jj + 1) * c], 0.0)
            att.append(a)
        s_t = [st_ref[hd] if chain else s0_ref[ci, hd].T for hd in heads]
        o_in = [_mm(att[hd], jnp.concatenate([vh[hd], vzero], axis=0)) for hd in heads]
        o_st = [_mm_nt(qh[hd] * jnp.exp(bh[hd]), s_t[hd]) for hd in heads]
        b_last = [bh[hd][c - 1:c] for hd in heads]
        upd = [_mm_tn(vh[hd], kh[hd] * jnp.exp(b_last[hd] - bh[hd])) for hd in heads]
        for hd in heads:
            s_new = s_t[hd] * jnp.exp(b_last[hd]) + upd[hd]
            if chain:
                st_ref[hd] = s_new
            else:
                s_ref[ci, hd] = s_new.T
            o = o_in[hd] + o_st[hd]
            on = o * lax.rsqrt(jnp.mean(o * o, axis=-1, keepdims=True) + NORM_EPS) * ng_ref[:, vl[hd]]
            y_ref[rows, vl[hd]] = (on * jax.nn.silu(gate_ref[rows, vl[hd]])).astype(y_ref.dtype)
        return carry

    lax.fori_loop(0, tt // c, chunk, 0, unroll=math.gcd(tt // c, 8))

    if chain:
        @pl.when(j == nj - 1)
        def _():
            for hd in range(GLA_HEADS):
                s_ref[0, hd] = st_ref[hd].T


def _stacked(acc, shape):
    if acc is None:
        return jnp.zeros((SUBLANE, LANE), f32), None
    assert acc.shape == shape
    return acc, 1


def _gla(h, s0, l0, acc, l, depth, p, nseq, t, chain):
    m = nseq * t
    acc_shape = (depth, nseq, GLA_HEADS, GLA_DK, GLA_DV)
    acc, alias_to = _stacked(acc, acc_shape)
    if chain:
        c = 16
        tt = min(t, 128)
        nj = t // tt
        grid = (nseq, nj)
        row = lambda b, j: b * nj + j
        st_spec = lambda li: pl.BlockSpec((None, 1, GLA_HEADS, GLA_DK, GLA_DV), lambda b, j: (li, b, 0, 0, 0))
    else:
        c = t
        tt = 8 * c
        nj = m // tt
        grid = (1, nj)
        row = lambda b, j: j
        st_spec = lambda li: pl.BlockSpec((None, tt // c, GLA_HEADS, GLA_DK, GLA_DV), lambda b, j: (li, j, 0, 0, 0))
    vec = lambda n: pl.BlockSpec((1, n), lambda b, j: (0, 0))
    return pl.pallas_call(
        functools.partial(_gla_kernel, tt=tt, c=c, nj=nj, chain=chain),
        grid=grid,
        in_specs=[pl.BlockSpec((tt, GLA_QK), lambda b, j: (row(b, j), COL_G_Q // GLA_QK)),
                  pl.BlockSpec((tt, GLA_QK), lambda b, j: (row(b, j), COL_G_K // GLA_QK)),
                  pl.BlockSpec((tt, GLA_WIDTH), lambda b, j: (row(b, j), COL_G_V // GLA_WIDTH)),
                  pl.BlockSpec((tt, GLA_WIDTH), lambda b, j: (row(b, j), COL_G_GATE // GLA_WIDTH)),
                  pl.BlockSpec((tt, LANE), lambda b, j: (row(b, j), COL_G_LR // LANE)),
                  pl.BlockSpec((LANE, GLA_QK), lambda b, j: (0, 0)), vec(GLA_QK), vec(GLA_WIDTH), st_spec(l0),
                  pl.BlockSpec(memory_space=pl.ANY)],
        out_specs=[pl.BlockSpec((tt, GLA_WIDTH), lambda b, j: (row(b, j), 0)), st_spec(l)],
        out_shape=[jax.ShapeDtypeStruct((m, GLA_WIDTH), _mix_dtype(t)), jax.ShapeDtypeStruct(acc_shape, f32)],
        scratch_shapes=[pltpu.VMEM((GLA_HEADS, GLA_DV, GLA_DK), f32), pltpu.VMEM((tt, GLA_QK), f32)],
        input_output_aliases={} if alias_to is None else {9: alias_to},
        compiler_params=_params("parallel" if chain else "arbitrary", "arbitrary"),
        name="gla_chain" if chain else "gla_tiles",
    )(h, h, h, h, h, p["w_gate"], p["b_gate"], p["norm_g"], s0, acc)


def _seg_sum(x):
    n = x.shape[0]
    w = 2 * LANE
    lane = lax.broadcasted_iota(jnp.int32, (w, w), 1)
    sub = lax.broadcasted_iota(jnp.int32, (w, w), 0)
    ones_bd = jnp.where((lane // RWKV_HEAD) == (sub // RWKV_HEAD), 1.0, 0.0)
    stacked = jnp.concatenate([x[:, i * w:(i + 1) * w] for i in range(RWKV_WIDTH // w)], axis=0)
    s = _mm(stacked, ones_bd)
    return jnp.concatenate([s[i * n:(i + 1) * n] for i in range(RWKV_WIDTH // w)], axis=1)


def _rwkv_kernel(r_ref, k_ref, v_ref, gate_ref, wa_ref, pr_ref, pk_ref, pv_ref, pwa_ref,
                 mur_ref, muk_ref, muv_ref, muwa_ref, w0_ref, w2_ref, a0_ref, a2_ref, kk_ref, ka_ref, rk_ref,
                 lnw_ref, lnb_ref, s0_ref, acc_ref, y_ref, s_ref, lr_ref, lk_ref, lv_ref, lwa_ref,
                 st, c_r, c_k, c_v, c_wa, x_kkg, x_rg, x_bi, x_ki, x_bend, x_kend, x_gam, x_v, x_bon,
                 x_tk, x_rq, x_uv, x_yv, x_y, *, tt, c, nj, chain):
    del acc_ref
    j = pl.program_id(1)
    hd = RWKV_HEAD

    if chain:
        @pl.when(j == 0)
        def _():
            st[...] = s0_ref[0]
            c_r[...] = jnp.broadcast_to(pr_ref[0], c_r.shape)
            c_k[...] = jnp.broadcast_to(pk_ref[0], c_k.shape)
            c_v[...] = jnp.broadcast_to(pv_ref[0], c_v.shape)
            c_wa[...] = jnp.broadcast_to(pwa_ref[0], c_wa.shape)
    else:
        @pl.when(j == 0)
        def _():
            st[...] = jnp.zeros(st.shape, f32)

    def shift(x_ref, c_ref, p_ref, mu_ref, last_ref):
        x = x_ref[...]
        rowid = lax.broadcasted_iota(jnp.int32, x.shape, 0)
        rolled = pltpu.roll(x, 1, 0)
        if chain:
            prev = jnp.where(rowid == 0, c_ref[0:1, :], rolled)
            c_ref[...] = jnp.broadcast_to(x[tt - 1:tt, :], c_ref.shape)
            last_ref[0] = x[tt - 1:tt, :]
        else:
            prev = jnp.where(rowid % c == 0, p_ref[...], rolled)
            for i in range(tt // c):
                last_ref[i:i + 1, :] = x[(i + 1) * c - 1:(i + 1) * c, :]
        return x + (prev - x) * mu_ref[...]

    r = shift(r_ref, c_r, pr_ref, mur_ref, lr_ref)
    k = shift(k_ref, c_k, pk_ref, muk_ref, lk_ref)
    v = shift(v_ref, c_v, pv_ref, muv_ref, lv_ref)
    wa = shift(wa_ref, c_wa, pwa_ref, muwa_ref, lwa_ref)

    w_log = -jax.nn.softplus(-(w0_ref[...] + _mm(jnp.tanh(wa), w2_ref[...]))) - 0.5
    lw = -jnp.exp(w_log)
    a = jax.nn.sigmoid(a0_ref[...] + _mm(wa, a2_ref[...]))

    lane = lax.broadcasted_iota(jnp.int32, (LANE, LANE), 1)
    sub = lax.broadcasted_iota(jnp.int32, (LANE, LANE), 0)
    bd = (lane // RWKV_HEAD) == (sub // RWKV_HEAD)

    kk = k * kk_ref[...]
    kk = kk / jnp.maximum(jnp.sqrt(_seg_sum(kk * kk)), 1e-12)
    k2 = k * (1.0 + (a - 1.0) * ka_ref[...])
    bv = kk * a
    ri = lax.broadcasted_iota(jnp.int32, (tt, tt), 0)
    qi = lax.broadcasted_iota(jnp.int32, (tt, tt), 1)
    same = (ri // c) == (qi // c)
    strict = same & (qi < ri)
    incl = same & (qi <= ri)
    eye = jnp.where(qi == ri, 1.0, 0.0).astype(f32)
    cl = _mm_mask_l(jnp.where(incl, 1.0, 0.0), lw)
    if c == tt:
        cll = jnp.broadcast_to(cl[tt - 1:tt, :], cl.shape)
    else:
        cll = _mm_mask_l(jnp.where(same, 1.0, 0.0), lw)
    ginv = jnp.exp(-cl)
    gend = jnp.exp(cll - cl)
    x_kkg[...] = kk * jnp.exp(cl - lw)
    x_rg[...] = r * jnp.exp(cl)
    x_bi[...] = bv * ginv
    x_ki[...] = k2 * ginv
    x_bend[...] = bv * gend
    x_kend[...] = k2 * gend
    x_gam[...] = jnp.exp(cll)
    x_v[...] = v
    x_bon[...] = _seg_sum(r * k2 * rk_ref[...]) * v

    head_a = lax.broadcasted_iota(jnp.int32, (tt, LANE), 1) < RWKV_HEAD

    zero_tt = jnp.zeros((tt, tt), f32)

    def tri_inv(ams):
        def bdiag(x):
            return jnp.concatenate([jnp.concatenate([x[:, 0:tt], zero_tt], axis=1),
                                    jnp.concatenate([zero_tt, x[:, tt:2 * tt]], axis=1)], axis=0)

        if tt % LANE:
            ns = [-am for am in ams]
            ts = [eye + n for n in ns]
            if c > 2:
                ns = [_mm(n, n) for n in ns]
                width = 4
                while width < c:
                    both = [_mm(jnp.concatenate([t, n], axis=0), n) for t, n in zip(ts, ns)]
                    ts = [t + b[0:tt] for t, b in zip(ts, both)]
                    ns = [b[tt:2 * tt] for b in both]
                    width *= 2
                ts = [t + _mm(t, n) for t, n in zip(ts, ns)]
            return [(ts[i], ts[i + 1]) for i in range(0, len(ts), 2)]
        ns = [jnp.concatenate([-ams[i], -ams[i + 1]], axis=1) for i in range(0, len(ams), 2)]
        eye2 = jnp.concatenate([eye, eye], axis=1)
        ts = [eye2 + n for n in ns]
        if c > 2:
            ns = [_mm(n, bdiag(n)) for n in ns]
            width = 4
            while width < c:
                both = [_mm(jnp.concatenate([t, n], axis=0), bdiag(n)) for t, n in zip(ts, ns)]
                ts = [t + b[0:tt] for t, b in zip(ts, both)]
                ns = [b[tt:2 * tt] for b in both]
                width *= 2
            last = [_mm(t, bdiag(n)) for t, n in zip(ts, ns)]
            ts = [t + m for t, m in zip(ts, last)]
        return ts

    wide = tt % LANE == 0

    def join(xa, xb):
        return jnp.concatenate([xa, xb], axis=1) if wide else (xa, xb)

    def pair_mm(lhs, rhs):
        m = (lax.broadcasted_iota(jnp.int32, (1, rhs.shape[1]), 1) % LANE) < RWKV_HEAD
        if wide:
            return _mm(lhs, jnp.concatenate([jnp.where(m, rhs, 0.0), jnp.where(m, 0.0, rhs)], axis=0))
        return jnp.where(m, _mm(lhs[0], rhs), _mm(lhs[1], rhs))

    def lanes(p):
        return slice(p * LANE, (p + 1) * LANE)

    for g0 in range(0, RWKV_PAIRS, RWKV_GROUP):
        grp = list(range(g0, g0 + RWKV_GROUP))
        kkg = [x_kkg[:, lanes(p)] for p in grp]
        rg = [x_rg[:, lanes(p)] for p in grp]
        vp = [x_v[:, lanes(p)] for p in grp]
        lhs = [jnp.concatenate([jnp.where(head_a, a, 0.0), jnp.where(head_a, 0.0, a),
                                jnp.where(head_a, b, 0.0), jnp.where(head_a, 0.0, b)], axis=0)
               for a, b in zip(kkg, rg)]
        if tt % LANE == 0:
            pbk = [_mm_nt(x, jnp.concatenate([x_bi[:, lanes(p)], x_ki[:, lanes(p)]], axis=0))
                   for x, p in zip(lhs, grp)]
            pb = [x[:, 0:tt] for x in pbk]
            pk = [x[:, tt:2 * tt] for x in pbk]
        else:
            pb = [_mm_nt(x, x_bi[:, lanes(p)]) for x, p in zip(lhs, grp)]
            pk = [_mm_nt(x, x_ki[:, lanes(p)]) for x, p in zip(lhs, grp)]
        def blk(x, hh, mask):
            return jnp.where(mask, x[hh * tt:(hh + 1) * tt], 0.0)

        tinv = tri_inv([blk(x, hh, strict) for x in pb for hh in (0, 1)])
        ar = [join(blk(x, 2, incl), blk(x, 3, incl)) for x in pb]
        bmr = [join(jnp.concatenate([blk(x, 0, strict), blk(x, 2, incl)], axis=0),
                    jnp.concatenate([blk(x, 1, strict), blk(x, 3, incl)], axis=0)) for x in pk]
        mv = [pair_mm(x, vp[i]) for i, x in enumerate(bmr)]
        mt = [pair_mm(x, jnp.concatenate([mv[i][0:tt], kkg[i]], axis=1)) for i, x in enumerate(tinv)]
        uv = [-x[:, 0:LANE] for x in mt]
        tk = [x[:, LANE:2 * LANE] for x in mt]
        mr = [pair_mm(x, jnp.concatenate([uv[i], tk[i]], axis=1)) for i, x in enumerate(ar)]
        for i, p in enumerate(grp):
            x_uv[:, lanes(p)] = uv[i]
            x_tk[:, lanes(p)] = tk[i]
            x_yv[:, lanes(p)] = mr[i][:, 0:LANE] + mv[i][tt:2 * tt]
            x_rq[:, lanes(p)] = rg[i] - mr[i][:, LANE:2 * LANE]

    def chunk(ci, carry):
        row0 = pl.multiple_of(ci * c, c)
        rows = pl.ds(row0, c)
        pairs = range(RWKV_PAIRS)
        if not chain:
            for p in pairs:
                st[p, 0:hd, 0:hd] = s0_ref[ci, 2 * p]
                st[p, hd:2 * hd, hd:2 * hd] = s0_ref[ci, 2 * p + 1]
        s = [st[p] for p in pairs]
        zy = [_mm_nt(jnp.concatenate([x_tk[rows, lanes(p)], x_rq[rows, lanes(p)]], axis=0), s[p]) for p in pairs]
        u = [x_uv[rows, lanes(p)] - zy[p][0:c] for p in pairs]
        for p in pairs:
            x_y[rows, lanes(p)] = zy[p][c:2 * c] + x_yv[rows, lanes(p)]
        upd = [_mm_tn(jnp.concatenate([u[p], x_v[rows, lanes(p)]], axis=0),
                      jnp.concatenate([x_bend[rows, lanes(p)], x_kend[rows, lanes(p)]], axis=0)) for p in pairs]
        for p in pairs:
            s_new = s[p] * x_gam[pl.ds(row0, 1), lanes(p)] + jnp.where(bd, upd[p], 0.0)
            if chain:
                st[p] = s_new
            else:
                s_ref[ci, 2 * p] = s_new[0:hd, 0:hd]
                s_ref[ci, 2 * p + 1] = s_new[hd:2 * hd, hd:2 * hd]
        return carry

    lax.fori_loop(0, tt // c, chunk, 0, unroll=1 if chain else 2)

    y = x_y[...]
    mean = _seg_sum(y) * (1.0 / RWKV_HEAD)
    yc = y - mean
    var = _seg_sum(yc * yc) * (1.0 / RWKV_HEAD)
    yn = yc * lax.rsqrt(var + RWKV_GN_EPS) * lnw_ref[...] + lnb_ref[...] + x_bon[...]
    y_ref[...] = (yn * jax.nn.silu(gate_ref[...])).astype(y_ref.dtype)

    if chain:
        @pl.when(j == nj - 1)
        def _():
            s_ref[0] = st[...]


def _rwkv(h, prev, s0, l0, acc, l, depth, p, nseq, t, chain):
    m = nseq * t
    w = RWKV_WIDTH
    acc_shape = (depth, nseq) + ((RWKV_PAIRS, LANE, LANE) if chain else (RWKV_HEADS, RWKV_HEAD, RWKV_HEAD))
    acc, alias_to = _stacked(acc, acc_shape)
    if chain:
        tt = min(t, 128)
        c = tt
        nj = t // tt
        grid = (nseq, nj)
        row = lambda b, j: b * nj + j
        prev = tuple(x[:, None, :] for x in prev)
        prow = lambda n: pl.BlockSpec((1, 1, n), lambda b, j: (b, 0, 0))
        last_spec = prow
        last_shape = lambda n: jax.ShapeDtypeStruct((nseq, 1, n), f32)
        st_spec = lambda li: pl.BlockSpec((None, 1, RWKV_PAIRS, LANE, LANE), lambda b, j: (li, b, 0, 0, 0))
    else:
        c = t
        tt = 64
        nj = m // tt
        grid = (1, nj)
        row = lambda b, j: j
        prev = tuple(jnp.repeat(x, t, axis=0) for x in prev)
        prow = lambda n: pl.BlockSpec((tt, n), lambda b, j: (j, 0))
        last_spec = lambda n: pl.BlockSpec((tt // c, n), lambda b, j: (j, 0))
        last_shape = lambda n: jax.ShapeDtypeStruct((nseq, n), f32)
        st_spec = lambda li: pl.BlockSpec((None, tt // c, RWKV_HEADS, RWKV_HEAD, RWKV_HEAD),
                                          lambda b, j: (li, j, 0, 0, 0))
    vec = lambda n: pl.BlockSpec((1, n), lambda b, j: (0, 0))
    tok = lambda col: pl.BlockSpec((tt, w), lambda b, j: (row(b, j), col // w))
    big = lambda: pltpu.VMEM((tt, w), f32)
    y, acc, *last = pl.pallas_call(
        functools.partial(_rwkv_kernel, tt=tt, c=c, nj=nj, chain=chain),
        grid=grid,
        in_specs=[tok(COL_RW_R), tok(COL_RW_K), tok(COL_RW_V), tok(COL_RW_GATE),
                  pl.BlockSpec((tt, LANE), lambda b, j: (row(b, j), COL_RW_WA // LANE)),
                  prow(w), prow(w), prow(w), prow(LANE),
                  vec(w), vec(w), vec(w), vec(LANE),
                  vec(w), pl.BlockSpec((LANE, w), lambda b, j: (0, 0)), vec(w),
                  pl.BlockSpec((LANE, w), lambda b, j: (0, 0)), vec(w), vec(w), vec(w), vec(w), vec(w), st_spec(l0),
                  pl.BlockSpec(memory_space=pl.ANY)],
        out_specs=[pl.BlockSpec((tt, w), lambda b, j: (row(b, j), 0)), st_spec(l),
                   last_spec(w), last_spec(w), last_spec(w), last_spec(LANE)],
        out_shape=[jax.ShapeDtypeStruct((m, w), _mix_dtype(t)), jax.ShapeDtypeStruct(acc_shape, f32),
                   last_shape(w), last_shape(w), last_shape(w), last_shape(LANE)],
        input_output_aliases={} if alias_to is None else {23: alias_to},
        scratch_shapes=[pltpu.VMEM((RWKV_PAIRS, LANE, LANE), f32),
                        pltpu.VMEM((SUBLANE, w), f32), pltpu.VMEM((SUBLANE, w), f32),
                        pltpu.VMEM((SUBLANE, w), f32), pltpu.VMEM((SUBLANE, LANE), f32)] + [big() for _ in range(14)],
        compiler_params=_params("parallel" if chain else "arbitrary", "arbitrary"),
        name="rwkv_chain" if chain else "rwkv_tiles",
    )(h, h, h, h, h, *prev, p["mu_r"], p["mu_k"], p["mu_v"], p["mu_wa"], p["w0"], p["w2"], p["a0"], p["a2"],
      p["k_k"], p["k_a"], p["r_k"], p["ln_w"], p["ln_b"], s0, acc)
    shift = jnp.concatenate([x.reshape(nseq, -1) for x in last], axis=1)
    return y, acc, shift


def _xattn_kernel(q_ref, k_ref, v_ref, o_ref, *, nb, tq):
    scale = XA_HEAD_DIM ** -0.5
    probs = [(i, hd) for i in range(nb) for hd in range(XA_HEADS)]
    ln = [slice(hd * XA_HEAD_DIM, (hd + 1) * XA_HEAD_DIM) for hd in range(XA_HEADS)]
    qrows = [slice(i * tq, (i + 1) * tq) for i in range(nb)]
    mrows = [pl.ds(hd, N_MEM, stride=XA_HEADS) for hd in range(XA_HEADS)]
    s = [_mm_nt(q_ref[qrows[i], ln[hd]] * scale, k_ref[i, mrows[hd], :]) for i, hd in probs]
    s = [x - jnp.max(x, axis=-1, keepdims=True) for x in s]
    e = [jnp.exp(x) for x in s]
    pr = [x / jnp.sum(x, axis=-1, keepdims=True) for x in e]
    o = [_mm(pr[n], v_ref[i, mrows[hd], :]) for n, (i, hd) in enumerate(probs)]
    for n, (i, hd) in enumerate(probs):
        o_ref[qrows[i], ln[hd]] = o[n].astype(o_ref.dtype)


def _xattn(h, mem_k, mem_v, l0, nseq, t):
    m = nseq * t
    tq = min(t, 512)
    nj = t // tq
    nb = 8 if (nj == 1 and nseq % 8 == 0) else 1
    mem_spec = pl.BlockSpec((None, nb, N_MEM * XA_HEADS, XA_HEAD_DIM), lambda b, j: (l0, b, 0, 0))
    return pl.pallas_call(
        functools.partial(_xattn_kernel, nb=nb, tq=tq),
        grid=(nseq // nb, nj),
        in_specs=[pl.BlockSpec((nb * tq, XA_WIDTH), lambda b, j: (b * nj + j, COL_XA_Q // XA_WIDTH)),
                  mem_spec, mem_spec],
        out_specs=pl.BlockSpec((nb * tq, XA_WIDTH), lambda b, j: (b * nj + j, 0)),
        out_shape=jax.ShapeDtypeStruct((m, XA_WIDTH), _mix_dtype(t)),
        compiler_params=_params("parallel", "arbitrary"),
        name="xattn",
    )(h, mem_k, mem_v)


_RW0 = 5136
W_IN_SEGMENTS = ((_RW0, _RW0 + 1536), (_RW0 + 1536, _RW0 + 3072), (_RW0 + 3072, _RW0 + 4608), (9872, 11408),
                 (0, 1024), (1024, 2048), (3072, 4096), (4096, 5120), (2048, 2560), (2560, 3072), (11408, 11920),
                 (_RW0 + 4608, _RW0 + 4736), (5120, 5136))


def _reorder_cast_kernel(w_ref, o_ref):
    dst = 0
    for a, b in W_IN_SEGMENTS:
        o_ref[dst:dst + (b - a), :] = w_ref[a:b, :].astype(bf16)
        dst += b - a
    o_ref[dst:, :] = jnp.zeros((H_COLS - dst, o_ref.shape[1]), bf16)


def _prep_w_in(w_in, tk=256):
    depth, d, n = w_in.shape
    return pl.pallas_call(
        _reorder_cast_kernel,
        grid=(depth, d // tk),
        in_specs=[pl.BlockSpec((None, n, tk), lambda l, i: (l, 0, i))],
        out_specs=pl.BlockSpec((None, H_COLS, tk), lambda l, i: (l, 0, i)),
        out_shape=jax.ShapeDtypeStruct((depth, H_COLS, d), bf16),
        compiler_params=_params("parallel", "parallel"),
        name="w_in_layout",
    )(jnp.swapaxes(w_in, 1, 2))


def _block_diag_s5(b, c):
    gl = S5_GROUPS // S5_BLOCKS
    eye = jnp.eye(gl, dtype=b.dtype)
    bb = b.reshape(S5_BLOCKS, gl, S5_STATE, S5_GROUP)
    bbd = jnp.einsum("kgpn,gh->kgnhp", bb, eye).reshape(S5_BLOCKS, gl * S5_GROUP, gl * S5_STATE)
    cc = c.reshape(S5_BLOCKS, gl, S5_GROUP, S5_STATE)
    cbd = jnp.einsum("kgnp,gh->kgphn", cc, eye).reshape(S5_BLOCKS, gl * S5_STATE, gl * S5_GROUP)
    return bbd, cbd


def _layer_params(l, a):
    row = lambda x: x.reshape(1, -1)
    bre, cre = _block_diag_s5(a["s5_b_re"][l], a["s5_c_re"][l])
    bim, cim = _block_diag_s5(a["s5_b_im"][l], a["s5_c_im"][l])
    s5 = dict(lam_re=row(a["s5_lam_re"][l]), lam_im=row(a["s5_lam_im"][l]),
              log_step=row(jnp.broadcast_to(a["s5_log_step"][l][:, None], (S5_GROUPS, S5_STATE))),
              b_re=bre, b_im=bim, c_re=cre.astype(bf16), c_im=cim.astype(bf16),
              d=row(a["s5_d"][l]), w_glu=a["s5_w_glu"][l].astype(bf16), b_glu=row(a["s5_b_glu"][l]))
    wg = jnp.zeros((LANE, GLA_QK), f32).at[:GLA_GATE_RANK].set(a["gla_w_gate"][l])
    gla = dict(w_gate=wg, b_gate=row(a["gla_b_gate"][l]), norm_g=row(a["gla_norm_g"][l]))
    mu = a["rw_mu"][l]
    w = RWKV_WIDTH
    z64 = jnp.zeros((64, w), f32)
    rw = dict(mu_r=row(mu[0:w]), mu_k=row(mu[w:2 * w]), mu_v=row(mu[2 * w:3 * w]), mu_wa=row(mu[3 * w:]),
              w0=row(a["rw_w0"][l]), w2=jnp.concatenate([a["rw_w2"][l], z64], 0).astype(bf16),
              a0=row(a["rw_a0"][l]), a2=jnp.concatenate([z64, a["rw_a2"][l]], 0).astype(bf16),
              k_k=row(a["rw_k_k"][l]), k_a=row(a["rw_k_a"][l]), r_k=row(a["rw_r_k"][l]),
              ln_w=row(a["rw_ln_w"][l]), ln_b=row(a["rw_ln_b"][l]))
    return s5, gla, rw


def _unpack_rwkv_state(s):
    b = s.shape[0]
    s = s.reshape(b, RWKV_PAIRS, 2, RWKV_HEAD, 2, RWKV_HEAD)
    return jnp.stack([s[:, :, 0, :, 0, :], s[:, :, 1, :, 1, :]], axis=2).reshape(b, RWKV_HEADS, RWKV_HEAD, RWKV_HEAD)


def _split_shift(x):
    w = RWKV_WIDTH
    return tuple(x[:, a:b] for a, b in ((0, w), (w, 2 * w), (2 * w, 3 * w), (3 * w, 3 * w + LANE)))


def _mixer_layer(x, xn, nseq, t, mem_k, mem_v, s_s5, s_gla, s_rw, l0, acc_gla, acc_rw, l, depth, prev, chain, lp,
                 w_in_t, w_out_b, g_pre, g_post, g_next):
    s5p, glap, rwp = lp
    h = _norm_mm(x, g_pre, w_in_t, l) if xn is None else _proj(xn, w_in_t, l)
    y_s5, s5_state = _s5(h, s_s5, l0, s5p, nseq, t, chain)
    y_gla, s_gla_new = _gla(h, s_gla, l0, acc_gla, l, depth, glap, nseq, t, chain)
    y_rw, s_rw_new, shift = _rwkv(h, prev, s_rw, l0, acc_rw, l, depth, rwp, nseq, t, chain)
    y_xa = _xattn(h, mem_k, mem_v, l0, nseq, t)
    x_new, xn_new = _outproj_split((y_s5, y_gla, y_rw, y_xa), w_out_b, l, g_post, x, g_next)
    return x_new, xn_new, s5_state, s_gla_new, s_rw_new, shift


def kernel(x_prompt, x_sample, mem_prompt, cache_mem_k, cache_mem_v, state_s5, state_gla, state_rwkv, state_rwkv_shift, g_pre, g_post, w_in, w_out, s5_lam_re, s5_lam_im, s5_log_step, s5_b_re, s5_b_im, s5_c_re, s5_c_im, s5_d, s5_w_glu, s5_b_glu, gla_w_gate, gla_b_gate, gla_norm_g, rw_mu, rw_w0, rw_w2, rw_a0, rw_a2, rw_k_k, rw_k_a, rw_r_k, rw_ln_w, rw_ln_b, mem_g, w_mk, w_mv):
    raw = dict(s5_lam_re=s5_lam_re, s5_lam_im=s5_lam_im, s5_log_step=s5_log_step, s5_b_re=s5_b_re, s5_b_im=s5_b_im,
               s5_c_re=s5_c_re, s5_c_im=s5_c_im, s5_d=s5_d, s5_w_glu=s5_w_glu, s5_b_glu=s5_b_glu,
               gla_w_gate=gla_w_gate, gla_b_gate=gla_b_gate, gla_norm_g=gla_norm_g, rw_mu=rw_mu, rw_w0=rw_w0,
               rw_w2=rw_w2, rw_a0=rw_a0, rw_a2=rw_a2, rw_k_k=rw_k_k, rw_k_a=rw_k_a, rw_r_k=rw_r_k,
               rw_ln_w=rw_ln_w, rw_ln_b=rw_ln_b)
    bp, tp, d = x_prompt.shape
    bs, ts, _ = x_sample.shape
    depth = w_in.shape[0]
    w_in_t = _prep_w_in(w_in)
    w_out_b = w_out.astype(bf16)
    w_kv_t = jnp.swapaxes(jnp.concatenate([w_mk, w_mv], axis=2), 1, 2).astype(bf16)
    mem_rows = N_MEM * XA_HEADS

    xp = x_prompt.reshape(bp * tp, d)
    xs = x_sample.reshape(bs * ts, d)
    mem2d = mem_prompt.reshape(bp * N_MEM, d)
    zp = lambda *s: jnp.zeros(s, f32)
    prev_p = _split_shift(zp(bp, RWKV_SHIFT_COLS))

    gla_p = gla_s = rw_p = rw_s = xnp = xns = None
    zero_gla = zp(1, bp, GLA_HEADS, GLA_DK, GLA_DV)
    zero_rw = zp(1, bp, RWKV_PAIRS, LANE, LANE)
    zero_s5 = zp(1, bp, 2, S5_NS)
    s5_in = state_s5.reshape(depth, bs, S5_NS * 2).transpose(0, 2, 1)
    mem_k_s = cache_mem_k.reshape(depth, bs, mem_rows, XA_HEAD_DIM)
    mem_v_s = cache_mem_v.reshape(depth, bs, mem_rows, XA_HEAD_DIM)

    outs = [[] for _ in range(6)]
    for l in range(depth):
        lp = _layer_params(l, raw)
        gp, gq = g_pre[l].reshape(1, d), g_post[l].reshape(1, d)
        gn = g_pre[l + 1].reshape(1, d) if l + 1 < depth else None
        kv = _norm_mm(mem2d, mem_g[l].reshape(1, d), w_kv_t, l)
        k_mem = kv[:, :XA_WIDTH].reshape(bp, N_MEM, XA_WIDTH)
        v_mem = kv[:, XA_WIDTH:].reshape(bp, N_MEM, XA_WIDTH)
        xp, xnp, h1, gla_p, rw_p, p1 = _mixer_layer(
            xp, xnp, bp, tp, k_mem.reshape(1, bp, mem_rows, XA_HEAD_DIM), v_mem.reshape(1, bp, mem_rows, XA_HEAD_DIM),
            zero_s5, zero_gla, zero_rw, 0, gla_p, rw_p, l, depth,
            prev_p, True, lp, w_in_t, w_out_b, gp, gq, gn)
        xs, xns, h2, gla_s, rw_s, p2 = _mixer_layer(
            xs, xns, bs, ts, mem_k_s, mem_v_s,
            s5_in, state_gla, state_rwkv, l, gla_s, rw_s, l, depth,
            _split_shift(state_rwkv_shift[l]), False, lp, w_in_t, w_out_b, gp, gq, gn)
        vals = (k_mem.reshape(bp, N_MEM, XA_HEADS, XA_HEAD_DIM), v_mem.reshape(bp, N_MEM, XA_HEADS, XA_HEAD_DIM),
                h1, h2, p1, p2)
        for o, val in zip(outs, vals):
            o.append(val)

    mk, mv, s5_p, s5_s, sh_p, sh_s = (jnp.stack(o) for o in outs)
    s5_p = s5_p.transpose(0, 1, 3, 2).reshape(depth, bp, S5_GROUPS, S5_STATE, 2)
    s5_s = s5_s.transpose(0, 2, 1).reshape(depth, bs, S5_GROUPS, S5_STATE, 2)
    rw_p_heads = _unpack_rwkv_state(rw_p.reshape(depth * bp, RWKV_PAIRS, LANE, LANE))
    rw_p_heads = rw_p_heads.reshape(depth, bp, RWKV_HEADS, RWKV_HEAD, RWKV_HEAD)
    return (xp.reshape(bp, tp, d), xs.reshape(bs, ts, d), mk, mv, s5_p, s5_s, gla_p, gla_s, rw_p_heads, rw_s,
            sh_p, sh_s)
```
